```python
import jax, jax.numpy as jnp
from jax import lax
import numpy as np

D_MODEL = 2048
BATCH = 4
SEQ = 2048
DEPTH = 1

W_CONV = D_MODEL // 2
CONV_GROUPS = 8
W_MLSTM = D_MODEL - W_CONV
N_HEADS_M = 4
HEAD_DIM_M = W_MLSTM // N_HEADS_M
CONV_WIDTH = 3
CHUNK = 64
D_FF = 5632
LN_EPS = 1e-5
HEAD_NORM_EPS = 1e-6
N_IN = 3 * W_CONV + 4 * W_MLSTM + 2 * N_HEADS_M

kernel_name = "hybrid_shortconv_mlstm_convffn_deepnorm"


def causal_dwconv(u, w, b=None):
    s = u.shape[1]
    up = jnp.pad(u, ((0, 0), (CONV_WIDTH - 1, 0), (0, 0)))
    y = w[0] * up[:, 0:s]
    for j in range(1, CONV_WIDTH):
        y = y + w[j] * up[:, j:j + s]
    return y if b is None else y + b


def layer_norm(x, g, b):
    xf = x.astype(jnp.float32)
    mu = jnp.mean(xf, axis=-1, keepdims=True)
    var = jnp.mean(jnp.square(xf - mu), axis=-1, keepdims=True)
    y = (xf - mu) * lax.rsqrt(var + LN_EPS) * g.astype(jnp.float32) + b.astype(jnp.float32)
    return y.astype(x.dtype)


def head_layer_norm(h, g):
    mu = jnp.mean(h, axis=-1, keepdims=True)
    var = jnp.mean(jnp.square(h - mu), axis=-1, keepdims=True)
    return (h - mu) * lax.rsqrt(var + HEAD_NORM_EPS) * g.astype(jnp.float32)


def mlstm_chunkwise(q, k, v, i_pre, f_pre):
    f32 = jnp.float32
    bsz, s, nh, dh = q.shape
    nc = s // CHUNK

    def to_chunks(t):
        return t.astype(f32).reshape(bsz, nc, CHUNK, nh, dh).transpose(1, 0, 3, 2, 4)

    def gate_chunks(t):
        return t.astype(f32).reshape(bsz, nc, CHUNK, nh).transpose(1, 0, 3, 2)

    qc = to_chunks(q) * (dh ** -0.5)
    kc = to_chunks(k)
    vc = to_chunks(v)
    ic = gate_chunks(i_pre)
    lfc = gate_chunks(jax.nn.log_sigmoid(f_pre.astype(f32)))
    causal = jnp.tril(jnp.ones((CHUNK, CHUNK), dtype=bool))

    def step(carry, xs):
        c_state, n_state, m_state = carry
        qb, kb, vb, ib, lfb = xs
        bcum = jnp.cumsum(lfb, axis=-1)
        g_tot = bcum[..., -1]
        log_d = bcum[..., :, None] - bcum[..., None, :] + ib[..., None, :]
        log_d = jnp.where(causal, log_d, -jnp.inf)
        inter_log = bcum + m_state[..., None]
        m_t = jnp.maximum(inter_log, jnp.max(log_d, axis=-1))
        d_w = jnp.exp(log_d - m_t[..., None])
        scores = jnp.einsum('bhtd,bhsd->bhts', qb, kb) * d_w
        inter_w = jnp.exp(inter_log - m_t)
        num = (jnp.einsum('bhts,bhse->bhte', scores, vb)
               + inter_w[..., None] * jnp.einsum('bhtd,bhde->bhte', qb, c_state))
        den = jnp.sum(scores, axis=-1) + inter_w * jnp.einsum('bhtd,bhd->bht', qb, n_state)
        h = num / jnp.maximum(jnp.abs(den), jnp.exp(-m_t))[..., None]
        w_log = g_tot[..., None] - bcum + ib
        m_new = jnp.maximum(g_tot + m_state, jnp.max(w_log, axis=-1))
        w = jnp.exp(w_log - m_new[..., None])
        decay = jnp.exp(g_tot + m_state - m_new)
        c_new = decay[..., None, None] * c_state + jnp.einsum('bhs,bhsd,bhse->bhde', w, kb, vb)
        n_new = decay[..., None] * n_state + jnp.einsum('bhs,bhsd->bhd', w, kb)
        return (c_new, n_new, m_new), h

    init = (jnp.zeros((bsz, nh, dh, dh), f32),
            jnp.zeros((bsz, nh, dh), f32),
            jnp.zeros((bsz, nh), f32))
    _, h = lax.scan(step, init, (qc, kc, vc, ic, lfc))
    return h.transpose(1, 0, 3, 2, 4).reshape(bsz, s, nh, dh)


def hybrid_mixer(x, w_in, b_gates, w_sc_conv, mh_gain, w_out):
    bsz, s, _ = x.shape
    proj = jnp.einsum('bsd,dn->bsn', x, w_in)
    splits = [W_CONV, 2 * W_CONV, 3 * W_CONV,
              3 * W_CONV + W_MLSTM, 3 * W_CONV + 2 * W_MLSTM,
              3 * W_CONV + 3 * W_MLSTM, 3 * W_CONV + 4 * W_MLSTM]
    c_b, c_c, c_h, q, k, v, o, gates = jnp.split(proj, splits, axis=-1)
    y_conv = c_b * causal_dwconv(c_c * c_h, w_sc_conv)
    gates = gates + b_gates
    i_pre, f_pre = gates[..., :N_HEADS_M], gates[..., N_HEADS_M:]
    hd = (bsz, s, N_HEADS_M, HEAD_DIM_M)
    h = mlstm_chunkwise(q.reshape(hd), k.reshape(hd), v.reshape(hd), i_pre, f_pre)
    h = head_layer_norm(h, mh_gain.reshape(N_HEADS_M, HEAD_DIM_M))
    h = jax.nn.sigmoid(o.astype(jnp.float32)).reshape(hd) * h
    y_mlstm = h.reshape(bsz, s, W_MLSTM).astype(x.dtype)
    y = jnp.concatenate([y_conv, y_mlstm], axis=-1)
    return jnp.einsum('bsd,de->bse', y, w_out)


def conv_ffn(x, w_up, w_ffn_conv, b_ffn_conv, w_down):
    hid = jnp.einsum('bsd,df->bsf', x, w_up)
    hid = causal_dwconv(hid, w_ffn_conv, b_ffn_conv)
    val, gate = hid[..., :D_FF], hid[..., D_FF:]
    return jnp.einsum('bsf,fd->bsd', jax.nn.silu(gate) * val, w_down)


def setup_inputs(seed: int = 0) -> dict:
    key = jax.random.key(seed)
    ks = jax.random.split(key, 16)
    beta = (8 * DEPTH) ** -0.25
    nrm = jax.random.normal
    x = nrm(ks[0], (BATCH, SEQ, D_MODEL), jnp.float32)
    col_scale = jnp.ones((N_IN,), jnp.float32)
    col_scale = col_scale.at[2 * W_CONV:3 * W_CONV].set(beta)
    col_scale = col_scale.at[3 * W_CONV + 2 * W_MLSTM:3 * W_CONV + 3 * W_MLSTM].set(beta)
    w_in = nrm(ks[1], (DEPTH, D_MODEL, N_IN), jnp.float32) * (D_MODEL ** -0.5) * col_scale
    b_i = 0.1 * nrm(ks[2], (DEPTH, N_HEADS_M), jnp.float32)
    b_f = jnp.linspace(3.0, 6.0, N_HEADS_M, dtype=jnp.float32) + 0.1 * nrm(ks[3], (DEPTH, N_HEADS_M), jnp.float32)
    b_gates = jnp.concatenate([b_i, b_f], axis=-1)
    w_sc_conv = nrm(ks[4], (DEPTH, CONV_WIDTH, W_CONV), jnp.float32) * (CONV_WIDTH ** -0.5)
    mh_gain = 1.0 + 0.02 * nrm(ks[5], (DEPTH, W_MLSTM), jnp.float32)
    w_out = nrm(ks[6], (DEPTH, D_MODEL, D_MODEL), jnp.float32) * (D_MODEL ** -0.5) * beta
    ln1_g = 1.0 + 0.02 * nrm(ks[7], (DEPTH, D_MODEL), jnp.float32)
    ln1_b = 0.02 * nrm(ks[8], (DEPTH, D_MODEL), jnp.float32)
    w_up = nrm(ks[9], (DEPTH, D_MODEL, 2 * D_FF), jnp.float32) * (D_MODEL ** -0.5)
    w_ffn_conv = nrm(ks[10], (DEPTH, CONV_WIDTH, 2 * D_FF), jnp.float32) * (CONV_WIDTH ** -0.5)
    b_ffn_conv = 0.02 * nrm(ks[11], (DEPTH, 2 * D_FF), jnp.float32)
    w_down = nrm(ks[12], (DEPTH, D_FF, D_MODEL), jnp.float32) * (D_FF ** -0.5) * beta
    ln2_g = 1.0 + 0.02 * nrm(ks[13], (DEPTH, D_MODEL), jnp.float32)
    ln2_b = 0.02 * nrm(ks[14], (DEPTH, D_MODEL), jnp.float32)
    return {"x": x, "w_in": w_in, "b_gates": b_gates, "w_sc_conv": w_sc_conv,
            "mh_gain": mh_gain, "w_out": w_out, "ln1_g": ln1_g, "ln1_b": ln1_b,
            "w_up": w_up, "w_ffn_conv": w_ffn_conv, "b_ffn_conv": b_ffn_conv,
            "w_down": w_down, "ln2_g": ln2_g, "ln2_b": ln2_b}


def reference(x, w_in, b_gates, w_sc_conv, mh_gain, w_out, ln1_g, ln1_b,
              w_up, w_ffn_conv, b_ffn_conv, w_down, ln2_g, ln2_b):
    alpha = (2 * DEPTH) ** 0.25
    for l in range(DEPTH):
        mix = hybrid_mixer(x, w_in[l], b_gates[l], w_sc_conv[l], mh_gain[l], w_out[l])
        x = layer_norm(alpha * x + mix, ln1_g[l], ln1_b[l])
        ff = conv_ffn(x, w_up[l], w_ffn_conv[l], b_ffn_conv[l], w_down[l])
        x = layer_norm(alpha * x + ff, ln2_g[l], ln2_b[l])
    return x
```

```python
import functools

import jax
import jax.numpy as jnp
from jax import lax
from jax.experimental import pallas as pl
from jax.experimental.pallas import tpu as pltpu

F32 = jnp.float32
BF16 = jnp.bfloat16

N_HEADS_M = 4
CONV_WIDTH = 3
LN_EPS = 1e-5
HEAD_NORM_EPS = 1e-6

SUBLANES = 8
LANES = 128
VMEM_LIMIT_BYTES = 56 * 1024 * 1024

TM_CONV = 512
TC_CONV = 512
TM_QKVO = 1024
TN_QKVO = 1024
MLSTM_CHUNK = 256
TM_OUT = 512
TM_FFN = 512
TF_FFN = 512


def _dot(a, b):
    return jnp.dot(a, b, preferred_element_type=F32)


def _params(semantics):
    return pltpu.CompilerParams(dimension_semantics=semantics,
                                vmem_limit_bytes=VMEM_LIMIT_BYTES)


def _shifted_taps(buf_ref, carry_ref, slot, u, first, rows):
    @pl.when(first)
    def _():
        carry_ref[slot] = jnp.zeros(carry_ref.shape[1:], F32)

    buf_ref[0:SUBLANES, :] = carry_ref[slot]
    buf_ref[SUBLANES:SUBLANES + rows, :] = u
    carry_ref[slot] = u[rows - SUBLANES:rows, :]
    um2 = buf_ref[SUBLANES - 2:SUBLANES - 2 + rows, :]
    um1 = buf_ref[SUBLANES - 1:SUBLANES - 1 + rows, :]
    return um2, um1


def _inproj_conv_kernel(x_ref, wb_ref, wc_ref, wh_ref, cw_ref, o_ref,
                        xb_ref, carry_ref, buf_ref, *, tiles_per_seq):
    i = pl.program_id(0)
    j = pl.program_id(1)
    rows = x_ref.shape[0]

    @pl.when(j == 0)
    def _():
        xb_ref[...] = x_ref[...].astype(BF16)

    xb = xb_ref[...]
    c_b = _dot(xb, wb_ref[...])
    c_c = _dot(xb, wc_ref[...])
    c_h = _dot(xb, wh_ref[...])
    u = c_c * c_h
    first = (i % tiles_per_seq) == 0
    um2, um1 = _shifted_taps(buf_ref, carry_ref, j, u, first, rows)
    w = cw_ref[...]
    y = w[0:1, :] * um2 + w[1:2, :] * um1 + w[2:3, :] * u
    o_ref[...] = (c_b * y).astype(o_ref.dtype)


def _inproj_conv(x2d, w_in_bf, w_sc_conv, *, seq, w_conv):
    m, d = x2d.shape
    tm, tc = TM_CONV, TC_CONV
    nj = w_conv // tc
    kern = functools.partial(_inproj_conv_kernel, tiles_per_seq=seq // tm)
    return pl.pallas_call(
        kern,
        out_shape=jax.ShapeDtypeStruct((m, w_conv), BF16),
        grid=(m // tm, nj),
        in_specs=[
            pl.BlockSpec((tm, d), lambda i, j: (i, 0)),
            pl.BlockSpec((d, tc), lambda i, j: (0, j)),
            pl.BlockSpec((d, tc), lambda i, j: (0, nj + j)),
            pl.BlockSpec((d, tc), lambda i, j: (0, 2 * nj + j)),
            pl.BlockSpec((CONV_WIDTH, tc), lambda i, j: (0, j)),
        ],
        out_specs=pl.BlockSpec((tm, tc), lambda i, j: (i, j)),
        scratch_shapes=[
            pltpu.VMEM((tm, d), BF16),
            pltpu.VMEM((nj, SUBLANES, tc), F32),
            pltpu.VMEM((tm + SUBLANES, tc), F32),
        ],
        compiler_params=_params(("arbitrary", "arbitrary")),
        name="inproj_shortconv",
    )(x2d, w_in_bf, w_in_bf, w_in_bf, w_sc_conv)


def _inproj_qkvo_kernel(x_ref, w_ref, wg_ref, bg_ref, o_ref, g_ref, xb_ref):
    j = pl.program_id(1)

    @pl.when(j == 0)
    def _():
        xb = x_ref[...].astype(BF16)
        xb_ref[...] = xb
        g_ref[...] = _dot(xb, wg_ref[...]) + bg_ref[...]

    o_ref[...] = _dot(xb_ref[...], w_ref[...]).astype(o_ref.dtype)


def _inproj_qkvo(x2d, w_in_bf, w_gates_bf, b_gates_row, *, col0, width):
    m, d = x2d.shape
    tm, tn = TM_QKVO, TN_QKVO
    j0 = col0 // tn
    return pl.pallas_call(
        _inproj_qkvo_kernel,
        out_shape=(jax.ShapeDtypeStruct((m, width), BF16),
                   jax.ShapeDtypeStruct((m, LANES), F32)),
        grid=(m // tm, width // tn),
        in_specs=[
            pl.BlockSpec((tm, d), lambda i, j: (i, 0)),
            pl.BlockSpec((d, tn), lambda i, j: (0, j0 + j)),
            pl.BlockSpec((d, LANES), lambda i, j: (0, 0)),
            pl.BlockSpec((1, LANES), lambda i, j: (0, 0)),
        ],
        out_specs=(pl.BlockSpec((tm, tn), lambda i, j: (i, j)),
                   pl.BlockSpec((tm, LANES), lambda i, j: (i, 0))),
        scratch_shapes=[pltpu.VMEM((tm, d), BF16)],
        compiler_params=_params(("arbitrary", "arbitrary")),
        name="inproj_mlstm",
    )(x2d, w_in_bf, w_gates_bf, b_gates_row)


def _log_sigmoid(x):
    return jnp.minimum(x, 0.0) - jnp.log1p(jnp.exp(-jnp.abs(x)))


def _mlstm_kernel(q_ref, k_ref, v_ref, o_ref, g_ref, gain_ref, y_ref,
                  c_ref, n_ref, m_ref, *, nh, dh):
    chunk = q_ref.shape[0]

    @pl.when(pl.program_id(1) == 0)
    def _():
        c_ref[...] = jnp.zeros(c_ref.shape, F32)
        n_ref[...] = jnp.zeros(n_ref.shape, F32)
        m_ref[...] = jnp.zeros(m_ref.shape, F32)

    g = g_ref[...]
    lf = _log_sigmoid(g)
    row = lax.broadcasted_iota(jnp.int32, (chunk, chunk), 0)
    col = lax.broadcasted_iota(jnp.int32, (chunk, chunk), 1)
    causal = row >= col
    bcum = jnp.dot(causal.astype(F32), lf, precision=lax.Precision.HIGHEST,
                   preferred_element_type=F32)
    g_t = g.T
    bcum_t = bcum.T
    neg_inf = jnp.float32(-jnp.inf)

    for h in range(nh):
        cols = slice(h * dh, (h + 1) * dh)
        q = q_ref[:, cols] * jnp.asarray(dh ** -0.5, BF16)
        k = k_ref[:, cols]
        v = v_ref[:, cols]
        i_col = g[:, h:h + 1]
        b_col = bcum[:, nh + h:nh + h + 1]
        i_row = g_t[h:h + 1, :]
        b_row = bcum_t[nh + h:nh + h + 1, :]
        g_tot = b_col[chunk - 1:chunk, :]
        m_prev = m_ref[h, 0:1, 0:1]
        c_prev = c_ref[h]
        n_prev = n_ref[h, 0:1, :]

        log_d = jnp.where(causal, b_col - (b_row - i_row), neg_inf)
        inter_log = b_col + m_prev
        m_t = jnp.maximum(inter_log, jnp.max(log_d, axis=-1, keepdims=True))
        d_w = jnp.exp(log_d - m_t)
        scores = lax.dot_general(q, k, (((1,), (1,)), ((), ())),
                                 preferred_element_type=F32) * d_w
        inter_w = jnp.exp(inter_log - m_t)
        num = (_dot(scores.astype(BF16), v)
               + inter_w * _dot(q, c_prev.astype(BF16)))
        qn = jnp.sum(q.astype(F32) * n_prev, axis=-1, keepdims=True)
        den = jnp.sum(scores, axis=-1, keepdims=True) + inter_w * qn
        hh = num / jnp.maximum(jnp.abs(den), jnp.exp(-m_t))

        w_log = g_tot - b_col + i_col
        m_new = jnp.maximum(g_tot + m_prev, jnp.max(w_log, axis=0, keepdims=True))
        w = jnp.exp(w_log - m_new)
        decay = jnp.exp(g_tot + m_prev - m_new)
        kw = k.astype(F32) * w
        c_ref[h] = decay * c_prev + lax.dot_general(
            kw.astype(BF16), v, (((0,), (0,)), ((), ())), preferred_element_type=F32)
        n_ref[h, 0:1, :] = decay * n_prev + jnp.sum(kw, axis=0, keepdims=True)
        m_ref[h] = jnp.broadcast_to(m_new, m_ref.shape[1:])

        mu = jnp.mean(hh, axis=-1, keepdims=True)
        hc = hh - mu
        var = jnp.mean(hc * hc, axis=-1, keepdims=True)
        hn = hc * lax.rsqrt(var + HEAD_NORM_EPS) * gain_ref[:, cols]
        o = o_ref[:, cols].astype(F32)
        y_ref[:, cols] = (jax.nn.sigmoid(o) * hn).astype(y_ref.dtype)


def _mlstm(qkvo, gates, gain_row, *, batch, seq, nh, dh):
    m = qkvo.shape[0]
    w = nh * dh
    chunk = MLSTM_CHUNK
    nc = seq // chunk
    nwb = 1

    def blk(kind):
        return pl.BlockSpec((chunk, w), lambda b, c: (b * nc + c, kind * nwb))

    kern = functools.partial(_mlstm_kernel, nh=nh, dh=dh)
    return pl.pallas_call(
        kern,
        out_shape=jax.ShapeDtypeStruct((m, w), BF16),
        grid=(batch, nc),
        in_specs=[blk(0), blk(1), blk(2), blk(3),
                  pl.BlockSpec((chunk, LANES), lambda b, c: (b * nc + c, 0)),
                  pl.BlockSpec((1, w), lambda b, c: (0, 0))],
        out_specs=pl.BlockSpec((chunk, w), lambda b, c: (b * nc + c, 0)),
        scratch_shapes=[
            pltpu.VMEM((nh, dh, dh), F32),
            pltpu.VMEM((nh, SUBLANES, dh), F32),
            pltpu.VMEM((nh, SUBLANES, LANES), F32),
        ],
        compiler_params=_params(("arbitrary", "arbitrary")),
        name="mlstm_chunkwise",
    )(qkvo, qkvo, qkvo, qkvo, gates, gain_row)


def _layer_norm(z, g, b):
    mu = jnp.mean(z, axis=-1, keepdims=True)
    zc = z - mu
    var = jnp.mean(zc * zc, axis=-1, keepdims=True)
    return zc * lax.rsqrt(var + LN_EPS) * g + b


def _outproj_ln_kernel(x_ref, yc_ref, ym_ref, wc_ref, wm_ref, g_ref, b_ref, o_ref, *, alpha):
    mix = _dot(yc_ref[...], wc_ref[...]) + _dot(ym_ref[...], wm_ref[...])
    z = alpha * x_ref[...] + mix
    o_ref[...] = _layer_norm(z, g_ref[...], b_ref[...])


def _outproj_ln(x2d, y_conv, y_mlstm, w_out_bf, ln_g, ln_b, *, alpha):
    m, d = x2d.shape
    wc = y_conv.shape[1]
    wm = y_mlstm.shape[1]
    assert wc == wm
    tm = TM_OUT
    kern = functools.partial(_outproj_ln_kernel, alpha=alpha)
    return pl.pallas_call(
        kern,
        out_shape=jax.ShapeDtypeStruct((m, d), F32),
        grid=(m // tm,),
        in_specs=[
            pl.BlockSpec((tm, d), lambda i: (i, 0)),
            pl.BlockSpec((tm, wc), lambda i: (i, 0)),
            pl.BlockSpec((tm, wm), lambda i: (i, 0)),
            pl.BlockSpec((wc, d), lambda i: (0, 0)),
            pl.BlockSpec((wm, d), lambda i: (1, 0)),
            pl.BlockSpec((1, d), lambda i: (0, 0)),
            pl.BlockSpec((1, d), lambda i: (0, 0)),
        ],
        out_specs=pl.BlockSpec((tm, d), lambda i: (i, 0)),
        compiler_params=_params(("arbitrary",)),
        name="outproj_ln",
    )(x2d, y_conv, y_mlstm, w_out_bf, w_out_bf, ln_g, ln_b)


def _convffn_ln_kernel(x_ref, wv_ref, wg_ref, cwv_ref, cwg_ref, cbv_ref, cbg_ref, wd_ref,
                       g_ref, b_ref, o_ref, xb_ref, acc_ref, carry_v_ref, carry_g_ref,
                       buf_v_ref, buf_g_ref, *, alpha, tiles_per_seq):
    i = pl.program_id(0)
    f = pl.program_id(1)
    nf = pl.num_programs(1)
    rows = x_ref.shape[0]

    @pl.when(f == 0)
    def _():
        xb_ref[...] = x_ref[...].astype(BF16)

    xb = xb_ref[...]
    first = (i % tiles_per_seq) == 0

    def conv(h, buf_ref, carry_ref, cw_ref, cb_ref):
        hm2, hm1 = _shifted_taps(buf_ref, carry_ref, f, h, first, rows)
        w = cw_ref[...]
        return w[0:1, :] * hm2 + w[1:2, :] * hm1 + w[2:3, :] * h + cb_ref[...]

    val = conv(_dot(xb, wv_ref[...]), buf_v_ref, carry_v_ref, cwv_ref, cbv_ref)
    gate = conv(_dot(xb, wg_ref[...]), buf_g_ref, carry_g_ref, cwg_ref, cbg_ref)
    act = (jax.nn.silu(gate) * val).astype(BF16)
    part = _dot(act, wd_ref[...])

    @pl.when(f == 0)
    def _():
        acc_ref[...] = part

    @pl.when(f > 0)
    def _():
        acc_ref[...] += part

    @pl.when(f == nf - 1)
    def _():
        z = alpha * x_ref[...] + acc_ref[...]
        o_ref[...] = _layer_norm(z, g_ref[...], b_ref[...])


def _convffn_ln(x2d, w_up_bf, w_ffn_conv, b_ffn_conv_row, w_down_bf, ln_g, ln_b, *,
                alpha, seq, d_ff):
    m, d = x2d.shape
    tm, tf = TM_FFN, TF_FFN
    nf = d_ff // tf
    kern = functools.partial(_convffn_ln_kernel, alpha=alpha, tiles_per_seq=seq // tm)
    return pl.pallas_call(
        kern,
        out_shape=jax.ShapeDtypeStruct((m, d), F32),
        grid=(m // tm, nf),
        in_specs=[
            pl.BlockSpec((tm, d), lambda i, f: (i, 0)),
            pl.BlockSpec((d, tf), lambda i, f: (0, f)),
            pl.BlockSpec((d, tf), lambda i, f: (0, nf + f)),
            pl.BlockSpec((CONV_WIDTH, tf), lambda i, f: (0, f)),
            pl.BlockSpec((CONV_WIDTH, tf), lambda i, f: (0, nf + f)),
            pl.BlockSpec((1, tf), lambda i, f: (0, f)),
            pl.BlockSpec((1, tf), lambda i, f: (0, nf + f)),
            pl.BlockSpec((tf, d), lambda i, f: (f, 0)),
            pl.BlockSpec((1, d), lambda i, f: (0, 0)),
            pl.BlockSpec((1, d), lambda i, f: (0, 0)),
        ],
        out_specs=pl.BlockSpec((tm, d), lambda i, f: (i, 0)),
        scratch_shapes=[
            pltpu.VMEM((tm, d), BF16),
            pltpu.VMEM((tm, d), F32),
            pltpu.VMEM((nf, SUBLANES, tf), F32),
            pltpu.VMEM((nf, SUBLANES, tf), F32),
            pltpu.VMEM((tm + SUBLANES, tf), F32),
            pltpu.VMEM((tm + SUBLANES, tf), F32),
        ],
        compiler_params=_params(("arbitrary", "arbitrary")),
        name="convffn_ln",
    )(x2d, w_up_bf, w_up_bf, w_ffn_conv, w_ffn_conv, b_ffn_conv_row, b_ffn_conv_row,
      w_down_bf, ln_g, ln_b)


def kernel(x, w_in, b_gates, w_sc_conv, mh_gain, w_out, ln1_g, ln1_b,
           w_up, w_ffn_conv, b_ffn_conv, w_down, ln2_g, ln2_b):
    batch, seq, d = x.shape
    depth = w_in.shape[0]
    nh = N_HEADS_M
    w_conv = w_sc_conv.shape[-1]
    w_mlstm = mh_gain.shape[-1]
    dh = w_mlstm // nh
    d_ff = w_down.shape[1]
    n_gates = 2 * nh
    gate_col0 = 3 * w_conv + 4 * w_mlstm
    assert w_in.shape[-1] == gate_col0 + n_gates and w_conv + w_mlstm == d
    assert seq % TM_CONV == 0 and seq % TM_FFN == 0 and seq % MLSTM_CHUNK == 0
    assert w_conv % TC_CONV == 0 and d_ff % TF_FFN == 0
    assert (3 * w_conv) % TN_QKVO == 0 and (4 * w_mlstm) % TN_QKVO == 0
    assert (batch * seq) % TM_QKVO == 0 and (batch * seq) % TM_OUT == 0
    alpha = (2 * depth) ** 0.25

    x2d = x.reshape(batch * seq, d)
    for l in range(depth):
        w_in_bf = w_in[l].astype(BF16)
        w_gates_bf = jnp.pad(w_in_bf[:, gate_col0:], ((0, 0), (0, LANES - n_gates)))
        b_gates_row = jnp.pad(b_gates[l], (0, LANES - n_gates)).reshape(1, LANES)

        y_conv = _inproj_conv(x2d, w_in_bf, w_sc_conv[l], seq=seq, w_conv=w_conv)
        qkvo, gates = _inproj_qkvo(x2d, w_in_bf, w_gates_bf, b_gates_row,
                                   col0=3 * w_conv, width=4 * w_mlstm)
        y_mlstm = _mlstm(qkvo, gates, mh_gain[l].reshape(1, w_mlstm),
                         batch=batch, seq=seq, nh=nh, dh=dh)
        x2d = _outproj_ln(x2d, y_conv, y_mlstm, w_out[l].astype(BF16),
                          ln1_g[l].reshape(1, d), ln1_b[l].reshape(1, d), alpha=alpha)
        x2d = _convffn_ln(x2d, w_up[l].astype(BF16), w_ffn_conv[l],
                          b_ffn_conv[l].reshape(1, 2 * d_ff), w_down[l].astype(BF16),
                          ln2_g[l].reshape(1, d), ln2_b[l].reshape(1, d),
                          alpha=alpha, seq=seq, d_ff=d_ff)
    return x2d.reshape(batch, seq, d)
```

```python
import functools

import jax
import jax.numpy as jnp
from jax import lax
from jax.experimental import pallas as pl
from jax.experimental.pallas import tpu as pltpu

F32 = jnp.float32
BF16 = jnp.bfloat16

N_HEADS_M = 4
CONV_WIDTH = 3
LN_EPS = 1e-5
HEAD_NORM_EPS = 1e-6

SUBLANES = 8
LANES = 128
VMEM_LIMIT_BYTES = 56 * 1024 * 1024

TM_CONV = 512
TC_CONV = 512
TM_QKVO = 1024
TN_QKVO = 1024
MLSTM_CHUNK = 256
TM_OUT = 512
TM_FFN = 512
TF_FFN = 512
TS_FFN = 256


def _dot(a, b):
    return jnp.dot(a, b, preferred_element_type=F32)


def _params(semantics, flags=None):
    return pltpu.CompilerParams(dimension_semantics=semantics,
                                vmem_limit_bytes=VMEM_LIMIT_BYTES, flags=flags)


def _shifted_taps(buf_ref, carry_ref, slot, u, first, rows):
    @pl.when(first)
    def _():
        carry_ref[slot] = jnp.zeros(carry_ref.shape[1:], F32)

    buf_ref[0:SUBLANES, :] = carry_ref[slot]
    buf_ref[SUBLANES:SUBLANES + rows, :] = u
    carry_ref[slot] = u[rows - SUBLANES:rows, :]
    um2 = buf_ref[SUBLANES - 2:SUBLANES - 2 + rows, :]
    um1 = buf_ref[SUBLANES - 1:SUBLANES - 1 + rows, :]
    return um2, um1


def _inproj_conv_kernel(x_ref, wb_ref, wc_ref, wh_ref, cw_ref, o_ref,
                        xb_ref, carry_ref, buf_ref, *, tiles_per_seq):
    i = pl.program_id(0)
    j = pl.program_id(1)
    rows = x_ref.shape[0]

    @pl.when(j == 0)
    def _():
        xb_ref[...] = x_ref[...].astype(BF16)

    xb = xb_ref[...]
    c_b = _dot(xb, wb_ref[...])
    c_c = _dot(xb, wc_ref[...])
    c_h = _dot(xb, wh_ref[...])
    u = c_c * c_h
    first = (i % tiles_per_seq) == 0
    um2, um1 = _shifted_taps(buf_ref, carry_ref, j, u, first, rows)
    w = cw_ref[...]
    y = w[0:1, :] * um2 + w[1:2, :] * um1 + w[2:3, :] * u
    o_ref[...] = (c_b * y).astype(o_ref.dtype)


def _inproj_conv(x2d, w_in_bf, w_sc_conv, *, seq, w_conv):
    m, d = x2d.shape
    tm, tc = TM_CONV, TC_CONV
    nj = w_conv // tc
    kern = functools.partial(_inproj_conv_kernel, tiles_per_seq=seq // tm)
    return pl.pallas_call(
        kern,
        out_shape=jax.ShapeDtypeStruct((m, w_conv), BF16),
        grid=(m // tm, nj),
        in_specs=[
            pl.BlockSpec((tm, d), lambda i, j: (i, 0)),
            pl.BlockSpec((d, tc), lambda i, j: (0, j)),
            pl.BlockSpec((d, tc), lambda i, j: (0, nj + j)),
            pl.BlockSpec((d, tc), lambda i, j: (0, 2 * nj + j)),
            pl.BlockSpec((CONV_WIDTH, tc), lambda i, j: (0, j)),
        ],
        out_specs=pl.BlockSpec((tm, tc), lambda i, j: (i, j)),
        scratch_shapes=[
            pltpu.VMEM((tm, d), BF16),
            pltpu.VMEM((nj, SUBLANES, tc), F32),
            pltpu.VMEM((tm + SUBLANES, tc), F32),
        ],
        compiler_params=_params(("arbitrary", "arbitrary")),
        name="inproj_shortconv",
    )(x2d, w_in_bf, w_in_bf, w_in_bf, w_sc_conv)


def _inproj_qkvo_kernel(x_ref, w_ref, wg_ref, bg_ref, o_ref, g_ref, xb_ref):
    j = pl.program_id(1)

    @pl.when(j == 0)
    def _():
        xb = x_ref[...].astype(BF16)
        xb_ref[...] = xb
        g_ref[...] = _dot(xb, wg_ref[...]) + bg_ref[...]

    o_ref[...] = _dot(xb_ref[...], w_ref[...]).astype(o_ref.dtype)


def _inproj_qkvo(x2d, w_in_bf, w_gates_bf, b_gates_row, *, col0, width):
    m, d = x2d.shape
    tm, tn = TM_QKVO, TN_QKVO
    j0 = col0 // tn
    return pl.pallas_call(
        _inproj_qkvo_kernel,
        out_shape=(jax.ShapeDtypeStruct((m, width), BF16),
                   jax.ShapeDtypeStruct((m, LANES), F32)),
        grid=(m // tm, width // tn),
        in_specs=[
            pl.BlockSpec((tm, d), lambda i, j: (i, 0)),
            pl.BlockSpec((d, tn), lambda i, j: (0, j0 + j)),
            pl.BlockSpec((d, LANES), lambda i, j: (0, 0)),
            pl.BlockSpec((1, LANES), lambda i, j: (0, 0)),
        ],
        out_specs=(pl.BlockSpec((tm, tn), lambda i, j: (i, j)),
                   pl.BlockSpec((tm, LANES), lambda i, j: (i, 0))),
        scratch_shapes=[pltpu.VMEM((tm, d), BF16)],
        compiler_params=_params(("arbitrary", "arbitrary")),
        name="inproj_mlstm",
    )(x2d, w_in_bf, w_gates_bf, b_gates_row)


def _log_sigmoid(x):
    return jnp.minimum(x, 0.0) - jnp.log1p(jnp.exp(-jnp.abs(x)))


def _mlstm_kernel(q_ref, k_ref, v_ref, o_ref, g_ref, gain_ref, y_ref,
                  c_ref, n_ref, m_ref, *, nh, dh):
    chunk = q_ref.shape[0]

    @pl.when(pl.program_id(1) == 0)
    def _():
        c_ref[...] = jnp.zeros(c_ref.shape, F32)
        n_ref[...] = jnp.zeros(n_ref.shape, F32)
        m_ref[...] = jnp.zeros(m_ref.shape, F32)

    g = g_ref[...]
    lf = _log_sigmoid(g)
    row = lax.broadcasted_iota(jnp.int32, (chunk, chunk), 0)
    col = lax.broadcasted_iota(jnp.int32, (chunk, chunk), 1)
    causal = row >= col
    bcum = jnp.dot(causal.astype(F32), lf, precision=lax.Precision.HIGHEST,
                   preferred_element_type=F32)
    g_t = g.T
    bcum_t = bcum.T
    neg_inf = jnp.float32(-jnp.inf)

    for h in range(nh):
        cols = slice(h * dh, (h + 1) * dh)
        q = q_ref[:, cols] * jnp.asarray(dh ** -0.5, BF16)
        k = k_ref[:, cols]
        v = v_ref[:, cols]
        i_col = g[:, h:h + 1]
        b_col = bcum[:, nh + h:nh + h + 1]
        i_row = g_t[h:h + 1, :]
        b_row = bcum_t[nh + h:nh + h + 1, :]
        g_tot = b_col[chunk - 1:chunk, :]
        m_prev = m_ref[h, 0:1, 0:1]
        c_prev = c_ref[h]
        n_prev = n_ref[h, 0:1, :]

        log_d = jnp.where(causal, b_col - (b_row - i_row), neg_inf)
        inter_log = b_col + m_prev
        m_t = jnp.maximum(inter_log, jnp.max(log_d, axis=-1, keepdims=True))
        d_w = jnp.exp(log_d - m_t)
        scores = lax.dot_general(q, k, (((1,), (1,)), ((), ())),
                                 preferred_element_type=F32) * d_w
        inter_w = jnp.exp(inter_log - m_t)
        num = (_dot(scores.astype(BF16), v)
               + inter_w * _dot(q, c_prev.astype(BF16)))
        qn = jnp.sum(q.astype(F32) * n_prev, axis=-1, keepdims=True)
        den = jnp.sum(scores, axis=-1, keepdims=True) + inter_w * qn
        hh = num / jnp.maximum(jnp.abs(den), jnp.exp(-m_t))

        w_log = g_tot - b_col + i_col
        m_new = jnp.maximum(g_tot + m_prev, jnp.max(w_log, axis=0, keepdims=True))
        w = jnp.exp(w_log - m_new)
        decay = jnp.exp(g_tot + m_prev - m_new)
        kw = k.astype(F32) * w
        c_ref[h] = decay * c_prev + lax.dot_general(
            kw.astype(BF16), v, (((0,), (0,)), ((), ())), preferred_element_type=F32)
        n_ref[h, 0:1, :] = decay * n_prev + jnp.sum(kw, axis=0, keepdims=True)
        m_ref[h] = jnp.broadcast_to(m_new, m_ref.shape[1:])

        mu = jnp.mean(hh, axis=-1, keepdims=True)
        hc = hh - mu
        var = jnp.mean(hc * hc, axis=-1, keepdims=True)
        hn = hc * lax.rsqrt(var + HEAD_NORM_EPS) * gain_ref[:, cols]
        o = o_ref[:, cols].astype(F32)
        y_ref[:, cols] = (jax.nn.sigmoid(o) * hn).astype(y_ref.dtype)


def _mlstm(qkvo, gates, gain_row, *, batch, seq, nh, dh):
    m = qkvo.shape[0]
    w = nh * dh
    chunk = MLSTM_CHUNK
    nc = seq // chunk
    nwb = 1

    def blk(kind):
        return pl.BlockSpec((chunk, w), lambda b, c: (b * nc + c, kind * nwb))

    kern = functools.partial(_mlstm_kernel, nh=nh, dh=dh)
    return pl.pallas_call(
        kern,
        out_shape=jax.ShapeDtypeStruct((m, w), BF16),
        grid=(batch, nc),
        in_specs=[blk(0), blk(1), blk(2), blk(3),
                  pl.BlockSpec((chunk, LANES), lambda b, c: (b * nc + c, 0)),
                  pl.BlockSpec((1, w), lambda b, c: (0, 0))],
        out_specs=pl.BlockSpec((chunk, w), lambda b, c: (b * nc + c, 0)),
        scratch_shapes=[
            pltpu.VMEM((nh, dh, dh), F32),
            pltpu.VMEM((nh, SUBLANES, dh), F32),
            pltpu.VMEM((nh, SUBLANES, LANES), F32),
        ],
        compiler_params=_params(("arbitrary", "arbitrary")),
        name="mlstm_chunkwise",
    )(qkvo, qkvo, qkvo, qkvo, gates, gain_row)


def _layer_norm(z, g, b):
    mu = jnp.mean(z, axis=-1, keepdims=True)
    zc = z - mu
    var = jnp.mean(zc * zc, axis=-1, keepdims=True)
    return zc * lax.rsqrt(var + LN_EPS) * g + b


def _outproj_ln_kernel(x_ref, yc_ref, ym_ref, wc_ref, wm_ref, g_ref, b_ref, o_ref, *, alpha):
    mix = _dot(yc_ref[...], wc_ref[...]) + _dot(ym_ref[...], wm_ref[...])
    z = alpha * x_ref[...] + mix
    o_ref[...] = _layer_norm(z, g_ref[...], b_ref[...])


def _outproj_ln(x2d, y_conv, y_mlstm, w_out_bf, ln_g, ln_b, *, alpha):
    m, d = x2d.shape
    wc = y_conv.shape[1]
    wm = y_mlstm.shape[1]
    assert wc == wm
    tm = TM_OUT
    kern = functools.partial(_outproj_ln_kernel, alpha=alpha)
    return pl.pallas_call(
        kern,
        out_shape=jax.ShapeDtypeStruct((m, d), F32),
        grid=(m // tm,),
        in_specs=[
            pl.BlockSpec((tm, d), lambda i: (i, 0)),
            pl.BlockSpec((tm, wc), lambda i: (i, 0)),
            pl.BlockSpec((tm, wm), lambda i: (i, 0)),
            pl.BlockSpec((wc, d), lambda i: (0, 0)),
            pl.BlockSpec((wm, d), lambda i: (1, 0)),
            pl.BlockSpec((1, d), lambda i: (0, 0)),
            pl.BlockSpec((1, d), lambda i: (0, 0)),
        ],
        out_specs=pl.BlockSpec((tm, d), lambda i: (i, 0)),
        compiler_params=_params(("arbitrary",)),
        name="outproj_ln",
    )(x2d, y_conv, y_mlstm, w_out_bf, w_out_bf, ln_g, ln_b)


def _convffn_ln_kernel(x_ref, wv_ref, wg_ref, cwv_ref, cwg_ref, cbv_ref, cbg_ref, wd_ref,
                       g_ref, b_ref, o_ref, xb_ref, acc_ref, carry_v_ref, carry_g_ref,
                       buf_v_ref, buf_g_ref, *, alpha, tiles_per_seq, sub):
    i = pl.program_id(0)
    f = pl.program_id(1)
    nf = pl.num_programs(1)
    rows = x_ref.shape[0]
    tf = wv_ref.shape[1]
    first = (i % tiles_per_seq) == 0

    @pl.when(f == 0)
    def _():
        xb_ref[...] = x_ref[...].astype(BF16)
        acc_ref[...] = jnp.zeros(acc_ref.shape, F32)

    @pl.when(first)
    def _():
        carry_v_ref[f] = jnp.zeros(carry_v_ref.shape[1:], F32)
        carry_g_ref[f] = jnp.zeros(carry_g_ref.shape[1:], F32)

    def conv(buf_ref, carry_ref, cw_ref, cb_ref, cs):
        buf_ref[0:SUBLANES, cs] = carry_ref[f, :, cs]
        carry_ref[f, :, cs] = buf_ref[rows:rows + SUBLANES, cs]
        h = buf_ref[SUBLANES:SUBLANES + rows, cs]
        hm1 = buf_ref[SUBLANES - 1:SUBLANES - 1 + rows, cs]
        hm2 = buf_ref[SUBLANES - 2:SUBLANES - 2 + rows, cs]
        w = cw_ref[:, cs]
        return w[0:1, :] * hm2 + w[1:2, :] * hm1 + w[2:3, :] * h + cb_ref[:, cs]

    xb = xb_ref[...]
    col_slices = [slice(c * sub, (c + 1) * sub) for c in range(tf // sub)]
    for cs in col_slices:
        buf_v_ref[SUBLANES:SUBLANES + rows, cs] = _dot(xb, wv_ref[:, cs])
        buf_g_ref[SUBLANES:SUBLANES + rows, cs] = _dot(xb, wg_ref[:, cs])
    acts = []
    for cs in col_slices:
        val = conv(buf_v_ref, carry_v_ref, cwv_ref, cbv_ref, cs)
        gate = conv(buf_g_ref, carry_g_ref, cwg_ref, cbg_ref, cs)
        acts.append((jax.nn.silu(gate) * val).astype(BF16))
    acc_ref[...] += _dot(jnp.concatenate(acts, axis=1), wd_ref[...])

    @pl.when(f == nf - 1)
    def _():
        z = alpha * x_ref[...] + acc_ref[...]
        o_ref[...] = _layer_norm(z, g_ref[...], b_ref[...])


def _convffn_ln(x2d, w_up_bf, w_ffn_conv, b_ffn_conv_row, w_down_bf, ln_g, ln_b, *,
                alpha, seq, d_ff):
    m, d = x2d.shape
    tm, tf = TM_FFN, TF_FFN
    nf = d_ff // tf
    kern = functools.partial(_convffn_ln_kernel, alpha=alpha, tiles_per_seq=seq // tm,
                             sub=TS_FFN)
    return pl.pallas_call(
        kern,
        out_shape=jax.ShapeDtypeStruct((m, d), F32),
        grid=(m // tm, nf),
        in_specs=[
            pl.BlockSpec((tm, d), lambda i, f: (i, 0)),
            pl.BlockSpec((d, tf), lambda i, f: (0, f)),
            pl.BlockSpec((d, tf), lambda i, f: (0, nf + f)),
            pl.BlockSpec((CONV_WIDTH, tf), lambda i, f: (0, f)),
            pl.BlockSpec((CONV_WIDTH, tf), lambda i, f: (0, nf + f)),
            pl.BlockSpec((1, tf), lambda i, f: (0, f)),
            pl.BlockSpec((1, tf), lambda i, f: (0, nf + f)),
            pl.BlockSpec((tf, d), lambda i, f: (f, 0)),
            pl.BlockSpec((1, d), lambda i, f: (0, 0)),
            pl.BlockSpec((1, d), lambda i, f: (0, 0)),
        ],
        out_specs=pl.BlockSpec((tm, d), lambda i, f: (i, 0)),
        scratch_shapes=[
            pltpu.VMEM((tm, d), BF16),
            pltpu.VMEM((tm, d), F32),
            pltpu.VMEM((nf, SUBLANES, tf), F32),
            pltpu.VMEM((nf, SUBLANES, tf), F32),
            pltpu.VMEM((tm + SUBLANES, tf), F32),
            pltpu.VMEM((tm + SUBLANES, tf), F32),
        ],
        compiler_params=_params(("arbitrary", "arbitrary")),
        name="convffn_ln",
    )(x2d, w_up_bf, w_up_bf, w_ffn_conv, w_ffn_conv, b_ffn_conv_row, b_ffn_conv_row,
      w_down_bf, ln_g, ln_b)


def kernel(x, w_in, b_gates, w_sc_conv, mh_gain, w_out, ln1_g, ln1_b,
           w_up, w_ffn_conv, b_ffn_conv, w_down, ln2_g, ln2_b):
    batch, seq, d = x.shape
    depth = w_in.shape[0]
    nh = N_HEADS_M
    w_conv = w_sc_conv.shape[-1]
    w_mlstm = mh_gain.shape[-1]
    dh = w_mlstm // nh
    d_ff = w_down.shape[1]
    n_gates = 2 * nh
    gate_col0 = 3 * w_conv + 4 * w_mlstm
    assert w_in.shape[-1] == gate_col0 + n_gates and w_conv + w_mlstm == d
    assert seq % TM_CONV == 0 and seq % TM_FFN == 0 and seq % MLSTM_CHUNK == 0
    assert w_conv % TC_CONV == 0 and d_ff % TF_FFN == 0
    assert (3 * w_conv) % TN_QKVO == 0 and (4 * w_mlstm) % TN_QKVO == 0
    assert (batch * seq) % TM_QKVO == 0 and (batch * seq) % TM_OUT == 0
    alpha = (2 * depth) ** 0.25

    x2d = x.reshape(batch * seq, d)
    for l in range(depth):
        w_in_bf = w_in[l].astype(BF16)
        w_gates_bf = jnp.pad(w_in_bf[:, gate_col0:], ((0, 0), (0, LANES - n_gates)))
        b_gates_row = jnp.pad(b_gates[l], (0, LANES - n_gates)).reshape(1, LANES)

        y_conv = _inproj_conv(x2d, w_in_bf, w_sc_conv[l], seq=seq, w_conv=w_conv)
        qkvo, gates = _inproj_qkvo(x2d, w_in_bf, w_gates_bf, b_gates_row,
                                   col0=3 * w_conv, width=4 * w_mlstm)
        y_mlstm = _mlstm(qkvo, gates, mh_gain[l].reshape(1, w_mlstm),
                         batch=batch, seq=seq, nh=nh, dh=dh)
        x2d = _outproj_ln(x2d, y_conv, y_mlstm, w_out[l].astype(BF16),
                          ln1_g[l].reshape(1, d), ln1_b[l].reshape(1, d), alpha=alpha)
        x2d = _convffn_ln(x2d, w_up[l].astype(BF16), w_ffn_conv[l],
                          b_ffn_conv[l].reshape(1, 2 * d_ff), w_down[l].astype(BF16),
                          ln2_g[l].reshape(1, d), ln2_b[l].reshape(1, d),
                          alpha=alpha, seq=seq, d_ff=d_ff)
    return x2d.reshape(batch, seq, d)
```

```python
import functools

import jax
import jax.numpy as jnp
from jax import lax
from jax.experimental import pallas as pl
from jax.experimental.pallas import tpu as pltpu

F32 = jnp.float32
BF16 = jnp.bfloat16

N_HEADS_M = 4
CONV_WIDTH = 3
LN_EPS = 1e-5
HEAD_NORM_EPS = 1e-6

SUBLANES = 8
LANES = 128
BF16_SUBLANES = 16
VMEM_LIMIT_BYTES = 56 * 1024 * 1024

TM_CONV = 512
TC_CONV = 512
TM_QKVO = 1024
TN_QKVO = 1024
MLSTM_CHUNK = 256
TM_OUT = 512
TM_FFN = 512
TF_FFN = 512
TS_FFN = 256


def _dot(a, b):
    return jnp.dot(a, b, preferred_element_type=F32)


def _dot_t(a, b_t):
    return lax.dot_general(a, b_t, (((1,), (1,)), ((), ())), preferred_element_type=F32)


def _params(semantics, flags=None):
    return pltpu.CompilerParams(dimension_semantics=semantics,
                                vmem_limit_bytes=VMEM_LIMIT_BYTES, flags=flags)


def _cast_rows_spec(rows, cols, n_steps, step_of):
    assert rows % n_steps == 0 and (rows // n_steps) % BF16_SUBLANES == 0
    return pl.BlockSpec((rows // n_steps, cols), lambda *ids: (step_of(*ids), 0))


def _inproj_conv_kernel(x_ref, wb_ref, wc_ref, wh_ref, cw_ref, wdn_ref, wout_ref,
                        y_ref, xb_ref, wdn_bf_ref, wout_bf_ref,
                        wbf_ref, carry_ref, buf_ref, *, tiles_per_seq):
    i = pl.program_id(1)
    rows = x_ref.shape[0]

    @pl.when(i == 0)
    def _():
        wbf_ref[0] = wb_ref[...].astype(BF16)
        wbf_ref[1] = wc_ref[...].astype(BF16)
        wbf_ref[2] = wh_ref[...].astype(BF16)

    @pl.when(pl.program_id(0) == 0)
    def _():
        xb_ref[...] = x_ref[...].astype(BF16)

    @pl.when(i % tiles_per_seq == 0)
    def _():
        carry_ref[...] = jnp.zeros(carry_ref.shape, F32)

    wdn_bf_ref[...] = wdn_ref[...].astype(BF16)
    wout_bf_ref[...] = wout_ref[...].astype(BF16)

    xb = x_ref[...].astype(BF16)
    c_b = _dot_t(xb, wbf_ref[0])
    c_c = _dot_t(xb, wbf_ref[1])
    c_h = _dot_t(xb, wbf_ref[2])
    u = c_c * c_h
    buf_ref[0:SUBLANES, :] = carry_ref[...]
    buf_ref[SUBLANES:SUBLANES + rows, :] = u
    carry_ref[...] = u[rows - SUBLANES:rows, :]
    um2 = buf_ref[SUBLANES - 2:SUBLANES - 2 + rows, :]
    um1 = buf_ref[SUBLANES - 1:SUBLANES - 1 + rows, :]
    w = cw_ref[...]
    y = w[0:1, :] * um2 + w[1:2, :] * um1 + w[2:3, :] * u
    y_ref[...] = (c_b * y).astype(y_ref.dtype)


def _inproj_conv(x2d, w_in_t, w_sc_conv, w_down, w_out, *, seq, w_conv):
    m, d = x2d.shape
    tm, tc = TM_CONV, TC_CONV
    nj, ni = w_conv // tc, m // tm
    n_steps = nj * ni
    step_of = lambda j, i: j * ni + i
    once = pl.Buffered(1)
    kern = functools.partial(_inproj_conv_kernel, tiles_per_seq=seq // tm)
    return pl.pallas_call(
        kern,
        out_shape=(jax.ShapeDtypeStruct((m, w_conv), BF16),
                   jax.ShapeDtypeStruct((m, d), BF16),
                   jax.ShapeDtypeStruct(w_down.shape, BF16),
                   jax.ShapeDtypeStruct(w_out.shape, BF16)),
        grid=(nj, ni),
        in_specs=[
            pl.BlockSpec((tm, d), lambda j, i: (i, 0)),
            pl.BlockSpec((tc, d), lambda j, i: (j, 0), pipeline_mode=once),
            pl.BlockSpec((tc, d), lambda j, i: (nj + j, 0), pipeline_mode=once),
            pl.BlockSpec((tc, d), lambda j, i: (2 * nj + j, 0), pipeline_mode=once),
            pl.BlockSpec((CONV_WIDTH, tc), lambda j, i: (0, j)),
            _cast_rows_spec(*w_down.shape, n_steps, step_of),
            _cast_rows_spec(*w_out.shape, n_steps, step_of),
        ],
        out_specs=(pl.BlockSpec((tm, tc), lambda j, i: (i, j)),
                   pl.BlockSpec((tm, d), lambda j, i: (jnp.where(j == 0, i, ni - 1), 0)),
                   _cast_rows_spec(*w_down.shape, n_steps, step_of),
                   _cast_rows_spec(*w_out.shape, n_steps, step_of)),
        scratch_shapes=[
            pltpu.VMEM((3, tc, d), BF16),
            pltpu.VMEM((SUBLANES, tc), F32),
            pltpu.VMEM((tm + SUBLANES, tc), F32),
        ],
        compiler_params=_params(("arbitrary", "arbitrary")),
        name="inproj_shortconv",
    )(x2d, w_in_t, w_in_t, w_in_t, w_sc_conv, w_down, w_out)


def _inproj_qkvo_kernel(xb_ref, w_ref, wup_ref, o_ref, wup_bf_ref, wbf_ref):
    @pl.when(pl.program_id(1) == 0)
    def _():
        wbf_ref[...] = w_ref[...].astype(BF16)

    wup_bf_ref[...] = wup_ref[...].astype(BF16)
    o_ref[...] = _dot_t(xb_ref[...], wbf_ref[...]).astype(o_ref.dtype)


def _inproj_qkvo(xb2d, w_in_t, w_up, *, col0, width):
    m, d = xb2d.shape
    tm, tn = TM_QKVO, TN_QKVO
    j0 = col0 // tn
    nj, ni = width // tn, m // tm
    n_steps = nj * ni
    step_of = lambda j, i: j * ni + i
    return pl.pallas_call(
        _inproj_qkvo_kernel,
        out_shape=(jax.ShapeDtypeStruct((m, width), BF16),
                   jax.ShapeDtypeStruct(w_up.shape, BF16)),
        grid=(nj, ni),
        in_specs=[
            pl.BlockSpec((tm, d), lambda j, i: (i, 0)),
            pl.BlockSpec((tn, d), lambda j, i: (j0 + j, 0), pipeline_mode=pl.Buffered(1)),
            _cast_rows_spec(*w_up.shape, n_steps, step_of),
        ],
        out_specs=(pl.BlockSpec((tm, tn), lambda j, i: (i, j)),
                   _cast_rows_spec(*w_up.shape, n_steps, step_of)),
        scratch_shapes=[pltpu.VMEM((tn, d), BF16)],
        compiler_params=_params(("arbitrary", "arbitrary")),
        name="inproj_mlstm",
    )(xb2d, w_in_t, w_up)


def _log_sigmoid(x):
    return jnp.minimum(x, 0.0) - jnp.log1p(jnp.exp(-jnp.abs(x)))


def _mlstm_kernel(q_ref, k_ref, v_ref, o_ref, xb_ref, wg_ref, bg_ref, gain_ref, y_ref,
                  c_ref, n_ref, m_ref, *, nh, dh):
    chunk = q_ref.shape[0]

    @pl.when(pl.program_id(1) == 0)
    def _():
        c_ref[...] = jnp.zeros(c_ref.shape, F32)
        n_ref[...] = jnp.zeros(n_ref.shape, F32)
        m_ref[...] = jnp.zeros(m_ref.shape, F32)

    g = _dot_t(xb_ref[...], wg_ref[...]) + bg_ref[...]
    lf = _log_sigmoid(g)
    row = lax.broadcasted_iota(jnp.int32, (chunk, chunk), 0)
    col = lax.broadcasted_iota(jnp.int32, (chunk, chunk), 1)
    causal = row >= col
    bcum = jnp.dot(causal.astype(F32), lf, precision=lax.Precision.HIGHEST,
                   preferred_element_type=F32)
    g_t = g.T
    bcum_t = bcum.T
    neg_inf = jnp.float32(-jnp.inf)

    for h in range(nh):
        cols = slice(h * dh, (h + 1) * dh)
        q = q_ref[:, cols] * jnp.asarray(dh ** -0.5, BF16)
        k = k_ref[:, cols]
        v = v_ref[:, cols]
        i_col = g[:, h:h + 1]
        b_col = bcum[:, nh + h:nh + h + 1]
        i_row = g_t[h:h + 1, :]
        b_row = bcum_t[nh + h:nh + h + 1, :]
        g_tot = b_col[chunk - 1:chunk, :]
        m_prev = m_ref[h, 0:1, 0:1]
        c_prev = c_ref[h]
        n_prev = n_ref[h, 0:1, :]

        log_d = jnp.where(causal, b_col - (b_row - i_row), neg_inf)
        inter_log = b_col + m_prev
        m_t = jnp.maximum(inter_log, jnp.max(log_d, axis=-1, keepdims=True))
        d_w = jnp.exp(log_d - m_t)
        scores = lax.dot_general(q, k, (((1,), (1,)), ((), ())),
                                 preferred_element_type=F32) * d_w
        inter_w = jnp.exp(inter_log - m_t)
        num = (_dot(scores.astype(BF16), v)
               + inter_w * _dot(q, c_prev.astype(BF16)))
        qn = jnp.sum(q.astype(F32) * n_prev, axis=-1, keepdims=True)
        den = jnp.sum(scores, axis=-1, keepdims=True) + inter_w * qn
        hh = num / jnp.maximum(jnp.abs(den), jnp.exp(-m_t))

        w_log = g_tot - b_col + i_col
        m_new = jnp.maximum(g_tot + m_prev, jnp.max(w_log, axis=0, keepdims=True))
        w = jnp.exp(w_log - m_new)
        decay = jnp.exp(g_tot + m_prev - m_new)
        kw = k.astype(F32) * w
        c_ref[h] = decay * c_prev + lax.dot_general(
            kw.astype(BF16), v, (((0,), (0,)), ((), ())), preferred_element_type=F32)
        n_ref[h, 0:1, :] = decay * n_prev + jnp.sum(kw, axis=0, keepdims=True)
        m_ref[h] = jnp.broadcast_to(m_new, m_ref.shape[1:])

        mu = jnp.mean(hh, axis=-1, keepdims=True)
        hc = hh - mu
        var = jnp.mean(hc * hc, axis=-1, keepdims=True)
        hn = hc * lax.rsqrt(var + HEAD_NORM_EPS) * gain_ref[:, cols]
        o = o_ref[:, cols].astype(F32)
        y_ref[:, cols] = (jax.nn.sigmoid(o) * hn).astype(y_ref.dtype)


def _mlstm(qkvo, xb2d, w_gates_t_bf, b_gates_row, gain_row, *, batch, seq, nh, dh):
    m, d = xb2d.shape
    w = nh * dh
    chunk = MLSTM_CHUNK
    nc = seq // chunk

    def blk(kind):
        return pl.BlockSpec((chunk, w), lambda b, c: (b * nc + c, kind))

    kern = functools.partial(_mlstm_kernel, nh=nh, dh=dh)
    return pl.pallas_call(
        kern,
        out_shape=jax.ShapeDtypeStruct((m, w), BF16),
        grid=(batch, nc),
        in_specs=[blk(0), blk(1), blk(2), blk(3),
                  pl.BlockSpec((chunk, d), lambda b, c: (b * nc + c, 0)),
                  pl.BlockSpec((LANES, d), lambda b, c: (0, 0)),
                  pl.BlockSpec((1, LANES), lambda b, c: (0, 0)),
                  pl.BlockSpec((1, w), lambda b, c: (0, 0))],
        out_specs=pl.BlockSpec((chunk, w), lambda b, c: (b * nc + c, 0)),
        scratch_shapes=[
            pltpu.VMEM((nh, dh, dh), F32),
            pltpu.VMEM((nh, SUBLANES, dh), F32),
            pltpu.VMEM((nh, SUBLANES, LANES), F32),
        ],
        compiler_params=_params(("arbitrary", "arbitrary")),
        name="mlstm_chunkwise",
    )(qkvo, qkvo, qkvo, qkvo, xb2d, w_gates_t_bf, b_gates_row, gain_row)


def _layer_norm(z, g, b):
    mu = jnp.mean(z, axis=-1, keepdims=True)
    zc = z - mu
    var = jnp.mean(zc * zc, axis=-1, keepdims=True)
    return zc * lax.rsqrt(var + LN_EPS) * g + b


def _outproj_ln_kernel(x_ref, yc_ref, ym_ref, wc_ref, wm_ref, g_ref, b_ref, o_ref, *, alpha):
    mix = _dot(yc_ref[...], wc_ref[...]) + _dot(ym_ref[...], wm_ref[...])
    z = alpha * x_ref[...] + mix
    o_ref[...] = _layer_norm(z, g_ref[...], b_ref[...])


def _outproj_ln(x2d, y_conv, y_mlstm, w_out_bf, ln_g, ln_b, *, alpha):
    m, d = x2d.shape
    wc = y_conv.shape[1]
    wm = y_mlstm.shape[1]
    assert wc == wm
    tm = TM_OUT
    kern = functools.partial(_outproj_ln_kernel, alpha=alpha)
    return pl.pallas_call(
        kern,
        out_shape=jax.ShapeDtypeStruct((m, d), F32),
        grid=(m // tm,),
        in_specs=[
            pl.BlockSpec((tm, d), lambda i: (i, 0)),
            pl.BlockSpec((tm, wc), lambda i: (i, 0)),
            pl.BlockSpec((tm, wm), lambda i: (i, 0)),
            pl.BlockSpec((wc, d), lambda i: (0, 0)),
            pl.BlockSpec((wm, d), lambda i: (1, 0)),
            pl.BlockSpec((1, d), lambda i: (0, 0)),
            pl.BlockSpec((1, d), lambda i: (0, 0)),
        ],
        out_specs=pl.BlockSpec((tm, d), lambda i: (i, 0)),
        compiler_params=_params(("arbitrary",)),
        name="outproj_ln",
    )(x2d, y_conv, y_mlstm, w_out_bf, w_out_bf, ln_g, ln_b)


def _convffn_ln_kernel(x_ref, wv_ref, wg_ref, cwv_ref, cwg_ref, cbv_ref, cbg_ref, wd_ref,
                       g_ref, b_ref, o_ref, xb_ref, acc_ref, carry_v_ref, carry_g_ref,
                       buf_v_ref, buf_g_ref, *, alpha, tiles_per_seq, sub):
    i = pl.program_id(0)
    f = pl.program_id(1)
    nf = pl.num_programs(1)
    rows = x_ref.shape[0]
    tf = wv_ref.shape[1]
    first = (i % tiles_per_seq) == 0

    @pl.when(f == 0)
    def _():
        xb_ref[...] = x_ref[...].astype(BF16)
        acc_ref[...] = jnp.zeros(acc_ref.shape, F32)

    @pl.when(first)
    def _():
        carry_v_ref[f] = jnp.zeros(carry_v_ref.shape[1:], F32)
        carry_g_ref[f] = jnp.zeros(carry_g_ref.shape[1:], F32)

    def conv(buf_ref, carry_ref, cw_ref, cb_ref, cs):
        buf_ref[0:SUBLANES, cs] = carry_ref[f, :, cs]
        carry_ref[f, :, cs] = buf_ref[rows:rows + SUBLANES, cs]
        h = buf_ref[SUBLANES:SUBLANES + rows, cs]
        hm1 = buf_ref[SUBLANES - 1:SUBLANES - 1 + rows, cs]
        hm2 = buf_ref[SUBLANES - 2:SUBLANES - 2 + rows, cs]
        w = cw_ref[:, cs]
        return w[0:1, :] * hm2 + w[1:2, :] * hm1 + w[2:3, :] * h + cb_ref[:, cs]

    xb = xb_ref[...]
    col_slices = [slice(c * sub, (c + 1) * sub) for c in range(tf // sub)]
    for cs in col_slices:
        buf_v_ref[SUBLANES:SUBLANES + rows, cs] = _dot(xb, wv_ref[:, cs])
        buf_g_ref[SUBLANES:SUBLANES + rows, cs] = _dot(xb, wg_ref[:, cs])
    acts = []
    for cs in col_slices:
        val = conv(buf_v_ref, carry_v_ref, cwv_ref, cbv_ref, cs)
        gate = conv(buf_g_ref, carry_g_ref, cwg_ref, cbg_ref, cs)
        acts.append((jax.nn.silu(gate) * val).astype(BF16))
    acc_ref[...] += _dot(jnp.concatenate(acts, axis=1), wd_ref[...])

    @pl.when(f == nf - 1)
    def _():
        z = alpha * x_ref[...] + acc_ref[...]
        o_ref[...] = _layer_norm(z, g_ref[...], b_ref[...])


def _convffn_ln(x2d, w_up_bf, w_ffn_conv, b_ffn_conv_row, w_down_bf, ln_g, ln_b, *,
                alpha, seq, d_ff):
    m, d = x2d.shape
    tm, tf = TM_FFN, TF_FFN
    nf = d_ff // tf
    kern = functools.partial(_convffn_ln_kernel, alpha=alpha, tiles_per_seq=seq // tm,
                             sub=TS_FFN)
    return pl.pallas_call(
        kern,
        out_shape=jax.ShapeDtypeStruct((m, d), F32),
        grid=(m // tm, nf),
        in_specs=[
            pl.BlockSpec((tm, d), lambda i, f: (i, 0)),
            pl.BlockSpec((d, tf), lambda i, f: (0, f)),
            pl.BlockSpec((d, tf), lambda i, f: (0, nf + f)),
            pl.BlockSpec((CONV_WIDTH, tf), lambda i, f: (0, f)),
            pl.BlockSpec((CONV_WIDTH, tf), lambda i, f: (0, nf + f)),
            pl.BlockSpec((1, tf), lambda i, f: (0, f)),
            pl.BlockSpec((1, tf), lambda i, f: (0, nf + f)),
            pl.BlockSpec((tf, d), lambda i, f: (f, 0)),
            pl.BlockSpec((1, d), lambda i, f: (0, 0)),
            pl.BlockSpec((1, d), lambda i, f: (0, 0)),
        ],
        out_specs=pl.BlockSpec((tm, d), lambda i, f: (i, 0)),
        scratch_shapes=[
            pltpu.VMEM((tm, d), BF16),
            pltpu.VMEM((tm, d), F32),
            pltpu.VMEM((nf, SUBLANES, tf), F32),
            pltpu.VMEM((nf, SUBLANES, tf), F32),
            pltpu.VMEM((tm + SUBLANES, tf), F32),
            pltpu.VMEM((tm + SUBLANES, tf), F32),
        ],
        compiler_params=_params(("arbitrary", "arbitrary")),
        name="convffn_ln",
    )(x2d, w_up_bf, w_up_bf, w_ffn_conv, w_ffn_conv, b_ffn_conv_row, b_ffn_conv_row,
      w_down_bf, ln_g, ln_b)


def kernel(x, w_in, b_gates, w_sc_conv, mh_gain, w_out, ln1_g, ln1_b,
           w_up, w_ffn_conv, b_ffn_conv, w_down, ln2_g, ln2_b):
    batch, seq, d = x.shape
    depth = w_in.shape[0]
    nh = N_HEADS_M
    w_conv = w_sc_conv.shape[-1]
    w_mlstm = mh_gain.shape[-1]
    dh = w_mlstm // nh
    d_ff = w_down.shape[1]
    n_gates = 2 * nh
    gate_col0 = 3 * w_conv + 4 * w_mlstm
    assert w_in.shape[-1] == gate_col0 + n_gates and w_conv + w_mlstm == d
    assert seq % TM_CONV == 0 and seq % TM_FFN == 0 and seq % MLSTM_CHUNK == 0
    assert w_conv % TC_CONV == 0 and d_ff % TF_FFN == 0
    assert (3 * w_conv) % TN_QKVO == 0 and (4 * w_mlstm) % TN_QKVO == 0
    assert (batch * seq) % TM_QKVO == 0 and (batch * seq) % TM_OUT == 0
    alpha = (2 * depth) ** 0.25

    x2d = x.reshape(batch * seq, d)
    for l in range(depth):
        w_in_t = w_in[l].T
        w_gates_t_bf = jnp.pad(w_in_t[gate_col0:], ((0, LANES - n_gates), (0, 0))).astype(BF16)
        b_gates_row = jnp.pad(b_gates[l], (0, LANES - n_gates)).reshape(1, LANES)

        y_conv, xb2d, w_down_bf, w_out_bf = _inproj_conv(
            x2d, w_in_t, w_sc_conv[l], w_down[l], w_out[l], seq=seq, w_conv=w_conv)
        qkvo, w_up_bf = _inproj_qkvo(xb2d, w_in_t, w_up[l], col0=3 * w_conv, width=4 * w_mlstm)
        y_mlstm = _mlstm(qkvo, xb2d, w_gates_t_bf, b_gates_row, mh_gain[l].reshape(1, w_mlstm),
                         batch=batch, seq=seq, nh=nh, dh=dh)
        x2d = _outproj_ln(x2d, y_conv, y_mlstm, w_out_bf,
                          ln1_g[l].reshape(1, d), ln1_b[l].reshape(1, d), alpha=alpha)
        x2d = _convffn_ln(x2d, w_up_bf, w_ffn_conv[l],
                          b_ffn_conv[l].reshape(1, 2 * d_ff), w_down_bf,
                          ln2_g[l].reshape(1, d), ln2_b[l].reshape(1, d),
                          alpha=alpha, seq=seq, d_ff=d_ff)
    return x2d.reshape(batch, seq, d)
```

```python
import functools

import jax
import jax.numpy as jnp
from jax import lax
from jax.experimental import pallas as pl
from jax.experimental.pallas import tpu as pltpu

F32 = jnp.float32
BF16 = jnp.bfloat16

N_HEADS_M = 4
CONV_WIDTH = 3
LN_EPS = 1e-5
HEAD_NORM_EPS = 1e-6

SUBLANES = 8
LANES = 128
BF16_SUBLANES = 16
VMEM_LIMIT_BYTES = 56 * 1024 * 1024

TM_CONV = 512
TC_CONV = 512
TR_CONV = 256
TM_QKVO = 1024
TN_QKVO = 1024
MLSTM_CHUNK = 256
TM_OUT = 512
TR_OUT = 256
TM_FFN = 512
TF_FFN = 512
TR_FFN = 256


def _dot(a, b):
    return jnp.dot(a, b, preferred_element_type=F32)


def _dot_t(a, b_t):
    return lax.dot_general(a, b_t, (((1,), (1,)), ((), ())), preferred_element_type=F32)


def _params(semantics, flags=None):
    return pltpu.CompilerParams(dimension_semantics=semantics,
                                vmem_limit_bytes=VMEM_LIMIT_BYTES, flags=flags)


def _cast_rows_spec(rows, cols, n_steps, step_of):
    assert rows % n_steps == 0 and (rows // n_steps) % BF16_SUBLANES == 0
    return pl.BlockSpec((rows // n_steps, cols), lambda *ids: (step_of(*ids), 0))


def _inproj_conv_kernel(x_ref, wb_ref, wc_ref, wh_ref, cw_ref, wdn_ref, wout_ref,
                        y_ref, xb_ref, wdn_bf_ref, wout_bf_ref,
                        wbf_ref, carry_ref, buf_ref, *, tiles_per_seq, chunk):
    i = pl.program_id(1)
    rows = x_ref.shape[0]

    @pl.when(i == 0)
    def _():
        wbf_ref[0] = wb_ref[...].astype(BF16)
        wbf_ref[1] = wc_ref[...].astype(BF16)
        wbf_ref[2] = wh_ref[...].astype(BF16)

    @pl.when(pl.program_id(0) == 0)
    def _():
        xb_ref[...] = x_ref[...].astype(BF16)

    @pl.when(i % tiles_per_seq == 0)
    def _():
        carry_ref[...] = jnp.zeros(carry_ref.shape, F32)

    wdn_bf_ref[...] = wdn_ref[...].astype(BF16)
    wout_bf_ref[...] = wout_ref[...].astype(BF16)

    buf_ref[0:SUBLANES, :] = carry_ref[...]
    w = cw_ref[...]
    for r0 in range(0, rows, chunk):
        xb = x_ref[r0:r0 + chunk, :].astype(BF16)
        c_b = _dot_t(xb, wbf_ref[0])
        c_c = _dot_t(xb, wbf_ref[1])
        c_h = _dot_t(xb, wbf_ref[2])
        u = c_c * c_h
        buf_ref[SUBLANES + r0:SUBLANES + r0 + chunk, :] = u
        um2 = buf_ref[SUBLANES - 2 + r0:SUBLANES - 2 + r0 + chunk, :]
        um1 = buf_ref[SUBLANES - 1 + r0:SUBLANES - 1 + r0 + chunk, :]
        y = w[0:1, :] * um2 + w[1:2, :] * um1 + w[2:3, :] * u
        y_ref[r0:r0 + chunk, :] = (c_b * y).astype(y_ref.dtype)
    carry_ref[...] = buf_ref[rows:rows + SUBLANES, :]


def _inproj_conv(x2d, w_in_t, w_sc_conv, w_down, w_out, *, seq, w_conv):
    m, d = x2d.shape
    tm, tc = TM_CONV, TC_CONV
    nj, ni = w_conv // tc, m // tm
    n_steps = nj * ni
    step_of = lambda j, i: j * ni + i
    once = pl.Buffered(1)
    kern = functools.partial(_inproj_conv_kernel, tiles_per_seq=seq // tm, chunk=TR_CONV)
    return pl.pallas_call(
        kern,
        out_shape=(jax.ShapeDtypeStruct((m, w_conv), BF16),
                   jax.ShapeDtypeStruct((m, d), BF16),
                   jax.ShapeDtypeStruct(w_down.shape, BF16),
                   jax.ShapeDtypeStruct(w_out.shape, BF16)),
        grid=(nj, ni),
        in_specs=[
            pl.BlockSpec((tm, d), lambda j, i: (i, 0)),
            pl.BlockSpec((tc, d), lambda j, i: (j, 0), pipeline_mode=once),
            pl.BlockSpec((tc, d), lambda j, i: (nj + j, 0), pipeline_mode=once),
            pl.BlockSpec((tc, d), lambda j, i: (2 * nj + j, 0), pipeline_mode=once),
            pl.BlockSpec((CONV_WIDTH, tc), lambda j, i: (0, j)),
            _cast_rows_spec(*w_down.shape, n_steps, step_of),
            _cast_rows_spec(*w_out.shape, n_steps, step_of),
        ],
        out_specs=(pl.BlockSpec((tm, tc), lambda j, i: (i, j)),
                   pl.BlockSpec((tm, d), lambda j, i: (jnp.where(j == 0, i, ni - 1), 0)),
                   _cast_rows_spec(*w_down.shape, n_steps, step_of),
                   _cast_rows_spec(*w_out.shape, n_steps, step_of)),
        scratch_shapes=[
            pltpu.VMEM((3, tc, d), BF16),
            pltpu.VMEM((SUBLANES, tc), F32),
            pltpu.VMEM((tm + SUBLANES, tc), F32),
        ],
        compiler_params=_params(("arbitrary", "arbitrary")),
        name="inproj_shortconv",
    )(x2d, w_in_t, w_in_t, w_in_t, w_sc_conv, w_down, w_out)


def _inproj_qkvo_kernel(xb_ref, w_ref, wup_ref, o_ref, wup_bf_ref, wbf_ref):
    @pl.when(pl.program_id(1) == 0)
    def _():
        wbf_ref[...] = w_ref[...].astype(BF16)

    wup_bf_ref[...] = wup_ref[...].astype(BF16)
    o_ref[...] = _dot_t(xb_ref[...], wbf_ref[...]).astype(o_ref.dtype)


def _inproj_qkvo(xb2d, w_in_t, w_up, *, col0, width):
    m, d = xb2d.shape
    tm, tn = TM_QKVO, TN_QKVO
    j0 = col0 // tn
    nj, ni = width // tn, m // tm
    n_steps = nj * ni
    step_of = lambda j, i: j * ni + i
    return pl.pallas_call(
        _inproj_qkvo_kernel,
        out_shape=(jax.ShapeDtypeStruct((m, width), BF16),
                   jax.ShapeDtypeStruct(w_up.shape, BF16)),
        grid=(nj, ni),
        in_specs=[
            pl.BlockSpec((tm, d), lambda j, i: (i, 0)),
            pl.BlockSpec((tn, d), lambda j, i: (j0 + j, 0), pipeline_mode=pl.Buffered(1)),
            _cast_rows_spec(*w_up.shape, n_steps, step_of),
        ],
        out_specs=(pl.BlockSpec((tm, tn), lambda j, i: (i, j)),
                   _cast_rows_spec(*w_up.shape, n_steps, step_of)),
        scratch_shapes=[pltpu.VMEM((tn, d), BF16)],
        compiler_params=_params(("arbitrary", "arbitrary")),
        name="inproj_mlstm",
    )(xb2d, w_in_t, w_up)


def _log_sigmoid(x):
    return jnp.minimum(x, 0.0) - jnp.log1p(jnp.exp(-jnp.abs(x)))


def _mlstm_kernel(q_ref, k_ref, v_ref, o_ref, xb_ref, wg_ref, bg_ref, gain_ref, y_ref,
                  c_ref, n_ref, m_ref, *, nh, dh):
    chunk = q_ref.shape[0]

    @pl.when(pl.program_id(1) == 0)
    def _():
        c_ref[...] = jnp.zeros(c_ref.shape, F32)
        n_ref[...] = jnp.zeros(n_ref.shape, F32)
        m_ref[...] = jnp.zeros(m_ref.shape, F32)

    g = _dot_t(xb_ref[...], wg_ref[...]) + bg_ref[...]
    lf = _log_sigmoid(g)
    row = lax.broadcasted_iota(jnp.int32, (chunk, chunk), 0)
    col = lax.broadcasted_iota(jnp.int32, (chunk, chunk), 1)
    causal = row >= col
    bcum = jnp.dot(causal.astype(F32), lf, precision=lax.Precision.HIGHEST,
                   preferred_element_type=F32)
    g_t = g.T
    bcum_t = bcum.T
    neg_inf = jnp.float32(-jnp.inf)

    for h in range(nh):
        cols = slice(h * dh, (h + 1) * dh)
        q = q_ref[:, cols] * jnp.asarray(dh ** -0.5, BF16)
        k = k_ref[:, cols]
        v = v_ref[:, cols]
        i_col = g[:, h:h + 1]
        b_col = bcum[:, nh + h:nh + h + 1]
        i_row = g_t[h:h + 1, :]
        b_row = bcum_t[nh + h:nh + h + 1, :]
        g_tot = b_col[chunk - 1:chunk, :]
        m_prev = m_ref[h, 0:1, 0:1]
        c_prev = c_ref[h]
        n_prev = n_ref[h, 0:1, :]

        log_d = jnp.where(causal, b_col - (b_row - i_row), neg_inf)
        inter_log = b_col + m_prev
        m_t = jnp.maximum(inter_log, jnp.max(log_d, axis=-1, keepdims=True))
        d_w = jnp.exp(log_d - m_t)
        scores = lax.dot_general(q, k, (((1,), (1,)), ((), ())),
                                 preferred_element_type=F32) * d_w
        inter_w = jnp.exp(inter_log - m_t)
        num = (_dot(scores.astype(BF16), v)
               + inter_w * _dot(q, c_prev.astype(BF16)))
        qn = jnp.sum(q.astype(F32) * n_prev, axis=-1, keepdims=True)
        den = jnp.sum(scores, axis=-1, keepdims=True) + inter_w * qn
        hh = num / jnp.maximum(jnp.abs(den), jnp.exp(-m_t))

        w_log = g_tot - b_col + i_col
        m_new = jnp.maximum(g_tot + m_prev, jnp.max(w_log, axis=0, keepdims=True))
        w = jnp.exp(w_log - m_new)
        decay = jnp.exp(g_tot + m_prev - m_new)
        kw = k.astype(F32) * w
        c_ref[h] = decay * c_prev + lax.dot_general(
            kw.astype(BF16), v, (((0,), (0,)), ((), ())), preferred_element_type=F32)
        n_ref[h, 0:1, :] = decay * n_prev + jnp.sum(kw, axis=0, keepdims=True)
        m_ref[h] = jnp.broadcast_to(m_new, m_ref.shape[1:])

        mu = jnp.mean(hh, axis=-1, keepdims=True)
        hc = hh - mu
        var = jnp.mean(hc * hc, axis=-1, keepdims=True)
        hn = hc * lax.rsqrt(var + HEAD_NORM_EPS) * gain_ref[:, cols]
        o = o_ref[:, cols].astype(F32)
        y_ref[:, cols] = (jax.nn.sigmoid(o) * hn).astype(y_ref.dtype)


def _mlstm(qkvo, xb2d, w_gates_t_bf, b_gates_row, gain_row, *, batch, seq, nh, dh):
    m, d = xb2d.shape
    w = nh * dh
    chunk = MLSTM_CHUNK
    nc = seq // chunk

    def blk(kind):
        return pl.BlockSpec((chunk, w), lambda b, c: (b * nc + c, kind))

    kern = functools.partial(_mlstm_kernel, nh=nh, dh=dh)
    return pl.pallas_call(
        kern,
        out_shape=jax.ShapeDtypeStruct((m, w), BF16),
        grid=(batch, nc),
        in_specs=[blk(0), blk(1), blk(2), blk(3),
                  pl.BlockSpec((chunk, d), lambda b, c: (b * nc + c, 0)),
                  pl.BlockSpec((LANES, d), lambda b, c: (0, 0)),
                  pl.BlockSpec((1, LANES), lambda b, c: (0, 0)),
                  pl.BlockSpec((1, w), lambda b, c: (0, 0))],
        out_specs=pl.BlockSpec((chunk, w), lambda b, c: (b * nc + c, 0)),
        scratch_shapes=[
            pltpu.VMEM((nh, dh, dh), F32),
            pltpu.VMEM((nh, SUBLANES, dh), F32),
            pltpu.VMEM((nh, SUBLANES, LANES), F32),
        ],
        compiler_params=_params(("arbitrary", "arbitrary")),
        name="mlstm_chunkwise",
    )(qkvo, qkvo, qkvo, qkvo, xb2d, w_gates_t_bf, b_gates_row, gain_row)


def _layer_norm(z, g, b):
    mu = jnp.mean(z, axis=-1, keepdims=True)
    zc = z - mu
    var = jnp.mean(zc * zc, axis=-1, keepdims=True)
    return zc * lax.rsqrt(var + LN_EPS) * g + b


def _outproj_ln_kernel(x_ref, yc_ref, ym_ref, wc_ref, wm_ref, g_ref, b_ref, o_ref, *,
                       alpha, chunk):
    for r0 in range(0, x_ref.shape[0], chunk):
        rs = slice(r0, r0 + chunk)
        mix = _dot(yc_ref[rs, :], wc_ref[...]) + _dot(ym_ref[rs, :], wm_ref[...])
        z = alpha * x_ref[rs, :] + mix
        o_ref[rs, :] = _layer_norm(z, g_ref[...], b_ref[...])


def _outproj_ln(x2d, y_conv, y_mlstm, w_out_bf, ln_g, ln_b, *, alpha):
    m, d = x2d.shape
    wc = y_conv.shape[1]
    wm = y_mlstm.shape[1]
    assert wc == wm
    tm = TM_OUT
    kern = functools.partial(_outproj_ln_kernel, alpha=alpha, chunk=TR_OUT)
    return pl.pallas_call(
        kern,
        out_shape=jax.ShapeDtypeStruct((m, d), F32),
        grid=(m // tm,),
        in_specs=[
            pl.BlockSpec((tm, d), lambda i: (i, 0)),
            pl.BlockSpec((tm, wc), lambda i: (i, 0)),
            pl.BlockSpec((tm, wm), lambda i: (i, 0)),
            pl.BlockSpec((wc, d), lambda i: (0, 0)),
            pl.BlockSpec((wm, d), lambda i: (1, 0)),
            pl.BlockSpec((1, d), lambda i: (0, 0)),
            pl.BlockSpec((1, d), lambda i: (0, 0)),
        ],
        out_specs=pl.BlockSpec((tm, d), lambda i: (i, 0)),
        compiler_params=_params(("arbitrary",)),
        name="outproj_ln",
    )(x2d, y_conv, y_mlstm, w_out_bf, w_out_bf, ln_g, ln_b)


def _convffn_ln_kernel(x_ref, wv_ref, wg_ref, cwv_ref, cwg_ref, cbv_ref, cbg_ref, wd_ref,
                       g_ref, b_ref, o_ref, xb_ref, acc_ref, carry_v_ref, carry_g_ref,
                       buf_v_ref, buf_g_ref, *, alpha, tiles_per_seq, chunk):
    i = pl.program_id(0)
    f = pl.program_id(1)
    nf = pl.num_programs(1)
    rows = x_ref.shape[0]
    first = (i % tiles_per_seq) == 0

    @pl.when(f == 0)
    def _():
        xb_ref[...] = x_ref[...].astype(BF16)
        acc_ref[...] = jnp.zeros(acc_ref.shape, F32)

    @pl.when(first)
    def _():
        carry_v_ref[f] = jnp.zeros(carry_v_ref.shape[1:], F32)
        carry_g_ref[f] = jnp.zeros(carry_g_ref.shape[1:], F32)

    buf_v_ref[0:SUBLANES, :] = carry_v_ref[f]
    buf_g_ref[0:SUBLANES, :] = carry_g_ref[f]

    def conv(buf_ref, cw_ref, cb_ref, r0):
        h = buf_ref[SUBLANES + r0:SUBLANES + r0 + chunk, :]
        hm1 = buf_ref[SUBLANES - 1 + r0:SUBLANES - 1 + r0 + chunk, :]
        hm2 = buf_ref[SUBLANES - 2 + r0:SUBLANES - 2 + r0 + chunk, :]
        w = cw_ref[...]
        return w[0:1, :] * hm2 + w[1:2, :] * hm1 + w[2:3, :] * h + cb_ref[...]

    row_starts = list(range(0, rows, chunk))
    for r0 in row_starts:
        xb = xb_ref[r0:r0 + chunk, :]
        buf_v_ref[SUBLANES + r0:SUBLANES + r0 + chunk, :] = _dot(xb, wv_ref[...])
        buf_g_ref[SUBLANES + r0:SUBLANES + r0 + chunk, :] = _dot(xb, wg_ref[...])
    for r0 in row_starts:
        val = conv(buf_v_ref, cwv_ref, cbv_ref, r0)
        gate = conv(buf_g_ref, cwg_ref, cbg_ref, r0)
        act = (jax.nn.silu(gate) * val).astype(BF16)
        acc_ref[r0:r0 + chunk, :] += _dot(act, wd_ref[...])
    carry_v_ref[f] = buf_v_ref[rows:rows + SUBLANES, :]
    carry_g_ref[f] = buf_g_ref[rows:rows + SUBLANES, :]

    @pl.when(f == nf - 1)
    def _():
        z = alpha * x_ref[...] + acc_ref[...]
        o_ref[...] = _layer_norm(z, g_ref[...], b_ref[...])


def _convffn_ln(x2d, w_up_bf, w_ffn_conv, b_ffn_conv_row, w_down_bf, ln_g, ln_b, *,
                alpha, seq, d_ff):
    m, d = x2d.shape
    tm, tf = TM_FFN, TF_FFN
    nf = d_ff // tf
    kern = functools.partial(_convffn_ln_kernel, alpha=alpha, tiles_per_seq=seq // tm,
                             chunk=TR_FFN)
    return pl.pallas_call(
        kern,
        out_shape=jax.ShapeDtypeStruct((m, d), F32),
        grid=(m // tm, nf),
        in_specs=[
            pl.BlockSpec((tm, d), lambda i, f: (i, 0)),
            pl.BlockSpec((d, tf), lambda i, f: (0, f)),
            pl.BlockSpec((d, tf), lambda i, f: (0, nf + f)),
            pl.BlockSpec((CONV_WIDTH, tf), lambda i, f: (0, f)),
            pl.BlockSpec((CONV_WIDTH, tf), lambda i, f: (0, nf + f)),
            pl.BlockSpec((1, tf), lambda i, f: (0, f)),
            pl.BlockSpec((1, tf), lambda i, f: (0, nf + f)),
            pl.BlockSpec((tf, d), lambda i, f: (f, 0)),
            pl.BlockSpec((1, d), lambda i, f: (0, 0)),
            pl.BlockSpec((1, d), lambda i, f: (0, 0)),
        ],
        out_specs=pl.BlockSpec((tm, d), lambda i, f: (i, 0)),
        scratch_shapes=[
            pltpu.VMEM((tm, d), BF16),
            pltpu.VMEM((tm, d), F32),
            pltpu.VMEM((nf, SUBLANES, tf), F32),
            pltpu.VMEM((nf, SUBLANES, tf), F32),
            pltpu.VMEM((tm + SUBLANES, tf), F32),
            pltpu.VMEM((tm + SUBLANES, tf), F32),
        ],
        compiler_params=_params(("arbitrary", "arbitrary")),
        name="convffn_ln",
    )(x2d, w_up_bf, w_up_bf, w_ffn_conv, w_ffn_conv, b_ffn_conv_row, b_ffn_conv_row,
      w_down_bf, ln_g, ln_b)


def kernel(x, w_in, b_gates, w_sc_conv, mh_gain, w_out, ln1_g, ln1_b,
           w_up, w_ffn_conv, b_ffn_conv, w_down, ln2_g, ln2_b):
    batch, seq, d = x.shape
    depth = w_in.shape[0]
    nh = N_HEADS_M
    w_conv = w_sc_conv.shape[-1]
    w_mlstm = mh_gain.shape[-1]
    dh = w_mlstm // nh
    d_ff = w_down.shape[1]
    n_gates = 2 * nh
    gate_col0 = 3 * w_conv + 4 * w_mlstm
    assert w_in.shape[-1] == gate_col0 + n_gates and w_conv + w_mlstm == d
    assert seq % TM_CONV == 0 and seq % TM_FFN == 0 and seq % MLSTM_CHUNK == 0
    assert w_conv % TC_CONV == 0 and d_ff % TF_FFN == 0
    assert (3 * w_conv) % TN_QKVO == 0 and (4 * w_mlstm) % TN_QKVO == 0
    assert (batch * seq) % TM_QKVO == 0 and (batch * seq) % TM_OUT == 0
    alpha = (2 * depth) ** 0.25

    x2d = x.reshape(batch * seq, d)
    for l in range(depth):
        w_in_t = w_in[l].T
        w_gates_t_bf = jnp.pad(w_in_t[gate_col0:], ((0, LANES - n_gates), (0, 0))).astype(BF16)
        b_gates_row = jnp.pad(b_gates[l], (0, LANES - n_gates)).reshape(1, LANES)

        y_conv, xb2d, w_down_bf, w_out_bf = _inproj_conv(
            x2d, w_in_t, w_sc_conv[l], w_down[l], w_out[l], seq=seq, w_conv=w_conv)
        qkvo, w_up_bf = _inproj_qkvo(xb2d, w_in_t, w_up[l], col0=3 * w_conv, width=4 * w_mlstm)
        y_mlstm = _mlstm(qkvo, xb2d, w_gates_t_bf, b_gates_row, mh_gain[l].reshape(1, w_mlstm),
                         batch=batch, seq=seq, nh=nh, dh=dh)
        x2d = _outproj_ln(x2d, y_conv, y_mlstm, w_out_bf,
                          ln1_g[l].reshape(1, d), ln1_b[l].reshape(1, d), alpha=alpha)
        x2d = _convffn_ln(x2d, w_up_bf, w_ffn_conv[l],
                          b_ffn_conv[l].reshape(1, 2 * d_ff), w_down_bf,
                          ln2_g[l].reshape(1, d), ln2_b[l].reshape(1, d),
                          alpha=alpha, seq=seq, d_ff=d_ff)
    return x2d.reshape(batch, seq, d)
```

```python
import functools

import jax
import jax.numpy as jnp
from jax import lax
from jax.experimental import pallas as pl
from jax.experimental.pallas import tpu as pltpu

F32 = jnp.float32
BF16 = jnp.bfloat16

N_HEADS_M = 4
CONV_WIDTH = 3
LN_EPS = 1e-5
HEAD_NORM_EPS = 1e-6

SUBLANES = 8
LANES = 128
BF16_SUBLANES = 16
VMEM_LIMIT_BYTES = 56 * 1024 * 1024

TM_CONV = 512
TC_CONV = 512
TR_CONV = 256
TM_QKVO = 1024
TN_QKVO = 1024
MLSTM_CHUNK = 256
TM_OUT = 512
TR_OUT = 256
TM_FFN = 512
TF_FFN = 512
TR_FFN = 256


def _dot(a, b):
    return jnp.dot(a, b, preferred_element_type=F32)


def _dot_t(a, b_t):
    return lax.dot_general(a, b_t, (((1,), (1,)), ((), ())), preferred_element_type=F32)


def _params(semantics, flags=None):
    return pltpu.CompilerParams(dimension_semantics=semantics,
                                vmem_limit_bytes=VMEM_LIMIT_BYTES, flags=flags)


def _cast_rows_spec(rows, cols, n_steps, step_of):
    assert rows % n_steps == 0 and (rows // n_steps) % BF16_SUBLANES == 0
    return pl.BlockSpec((rows // n_steps, cols), lambda *ids: (step_of(*ids), 0))


def _inproj_conv_kernel(x_ref, wb_ref, wc_ref, wh_ref, cw_ref, wdn_ref, wout_ref,
                        y_ref, xb_ref, wdn_bf_ref, wout_bf_ref,
                        wbf_ref, carry_ref, buf_ref, *, tiles_per_seq, chunk):
    i = pl.program_id(1)
    rows = x_ref.shape[0]

    @pl.when(i == 0)
    def _():
        wbf_ref[0] = wb_ref[...].astype(BF16)
        wbf_ref[1] = wc_ref[...].astype(BF16)
        wbf_ref[2] = wh_ref[...].astype(BF16)

    @pl.when(pl.program_id(0) == 0)
    def _():
        xb_ref[...] = x_ref[...].astype(BF16)

    @pl.when(i % tiles_per_seq == 0)
    def _():
        carry_ref[...] = jnp.zeros(carry_ref.shape, F32)

    wdn_bf_ref[...] = wdn_ref[...].astype(BF16)
    wout_bf_ref[...] = wout_ref[...].astype(BF16)

    buf_ref[0:SUBLANES, :] = carry_ref[...]
    w = cw_ref[...]
    for r0 in range(0, rows, chunk):
        xb = x_ref[r0:r0 + chunk, :].astype(BF16)
        c_b = _dot_t(xb, wbf_ref[0])
        c_c = _dot_t(xb, wbf_ref[1])
        c_h = _dot_t(xb, wbf_ref[2])
        u = c_c * c_h
        buf_ref[SUBLANES + r0:SUBLANES + r0 + chunk, :] = u
        um2 = buf_ref[SUBLANES - 2 + r0:SUBLANES - 2 + r0 + chunk, :]
        um1 = buf_ref[SUBLANES - 1 + r0:SUBLANES - 1 + r0 + chunk, :]
        y = w[0:1, :] * um2 + w[1:2, :] * um1 + w[2:3, :] * u
        y_ref[r0:r0 + chunk, :] = (c_b * y).astype(y_ref.dtype)
    carry_ref[...] = buf_ref[rows:rows + SUBLANES, :]


def _inproj_conv(x2d, w_in_t, w_sc_conv, w_down, w_out, *, seq, w_conv):
    m, d = x2d.shape
    tm, tc = TM_CONV, TC_CONV
    nj, ni = w_conv // tc, m // tm
    n_steps = nj * ni
    step_of = lambda j, i: j * ni + i
    once = pl.Buffered(1)
    kern = functools.partial(_inproj_conv_kernel, tiles_per_seq=seq // tm, chunk=TR_CONV)
    return pl.pallas_call(
        kern,
        out_shape=(jax.ShapeDtypeStruct((m, w_conv), BF16),
                   jax.ShapeDtypeStruct((m, d), BF16),
                   jax.ShapeDtypeStruct(w_down.shape, BF16),
                   jax.ShapeDtypeStruct(w_out.shape, BF16)),
        grid=(nj, ni),
        in_specs=[
            pl.BlockSpec((tm, d), lambda j, i: (i, 0)),
            pl.BlockSpec((tc, d), lambda j, i: (j, 0), pipeline_mode=once),
            pl.BlockSpec((tc, d), lambda j, i: (nj + j, 0), pipeline_mode=once),
            pl.BlockSpec((tc, d), lambda j, i: (2 * nj + j, 0), pipeline_mode=once),
            pl.BlockSpec((CONV_WIDTH, tc), lambda j, i: (0, j)),
            _cast_rows_spec(*w_down.shape, n_steps, step_of),
            _cast_rows_spec(*w_out.shape, n_steps, step_of),
        ],
        out_specs=(pl.BlockSpec((tm, tc), lambda j, i: (i, j)),
                   pl.BlockSpec((tm, d), lambda j, i: (jnp.where(j == 0, i, ni - 1), 0)),
                   _cast_rows_spec(*w_down.shape, n_steps, step_of),
                   _cast_rows_spec(*w_out.shape, n_steps, step_of)),
        scratch_shapes=[
            pltpu.VMEM((3, tc, d), BF16),
            pltpu.VMEM((SUBLANES, tc), F32),
            pltpu.VMEM((tm + SUBLANES, tc), F32),
        ],
        compiler_params=_params(("arbitrary", "arbitrary")),
        name="inproj_shortconv",
    )(x2d, w_in_t, w_in_t, w_in_t, w_sc_conv, w_down, w_out)


def _inproj_qkvo_kernel(xb_ref, w_ref, wup_ref, o_ref, wup_bf_ref, wbf_ref):
    @pl.when(pl.program_id(1) == 0)
    def _():
        wbf_ref[...] = w_ref[...].astype(BF16)

    wup_bf_ref[...] = wup_ref[...].astype(BF16)
    o_ref[...] = _dot_t(xb_ref[...], wbf_ref[...]).astype(o_ref.dtype)


def _inproj_qkvo(xb2d, w_in_t, w_up, *, col0, width):
    m, d = xb2d.shape
    tm, tn = TM_QKVO, TN_QKVO
    j0 = col0 // tn
    nj, ni = width // tn, m // tm
    n_steps = nj * ni
    step_of = lambda j, i: j * ni + i
    return pl.pallas_call(
        _inproj_qkvo_kernel,
        out_shape=(jax.ShapeDtypeStruct((m, width), BF16),
                   jax.ShapeDtypeStruct(w_up.shape, BF16)),
        grid=(nj, ni),
        in_specs=[
            pl.BlockSpec((tm, d), lambda j, i: (i, 0)),
            pl.BlockSpec((tn, d), lambda j, i: (j0 + j, 0)),
            _cast_rows_spec(*w_up.shape, n_steps, step_of),
        ],
        out_specs=(pl.BlockSpec((tm, tn), lambda j, i: (i, j)),
                   _cast_rows_spec(*w_up.shape, n_steps, step_of)),
        scratch_shapes=[pltpu.VMEM((tn, d), BF16)],
        compiler_params=_params(("arbitrary", "arbitrary")),
        name="inproj_mlstm",
    )(xb2d, w_in_t, w_up)


def _log_sigmoid(x):
    return jnp.minimum(x, 0.0) - jnp.log1p(jnp.exp(-jnp.abs(x)))


def _mlstm_kernel(q_ref, k_ref, v_ref, o_ref, xb0_ref, xbn_ref, wg_ref, bg_ref, gain_ref, y_ref,
                  c_ref, n_ref, m_ref, gates_ref, gates_t_ref, *, nh, dh):
    chunk = q_ref.shape[0]
    step = pl.program_id(0) * pl.num_programs(1) + pl.program_id(1)

    @pl.when(pl.program_id(1) == 0)
    def _():
        c_ref[...] = jnp.zeros(c_ref.shape, F32)
        n_ref[...] = jnp.zeros(n_ref.shape, F32)
        m_ref[...] = jnp.zeros(m_ref.shape, F32)

    row = lax.broadcasted_iota(jnp.int32, (chunk, chunk), 0)
    col = lax.broadcasted_iota(jnp.int32, (chunk, chunk), 1)
    causal = row >= col

    def gate_terms(xb):
        g = _dot_t(xb, wg_ref[...]) + bg_ref[...]
        lf = _log_sigmoid(g)
        bcum = jnp.dot(causal.astype(F32), lf, precision=lax.Precision.HIGHEST,
                       preferred_element_type=F32)
        gates_ref[0] = g
        gates_ref[1] = bcum
        gates_t_ref[0] = g.T
        gates_t_ref[1] = bcum.T

    @pl.when(step == 0)
    def _():
        gate_terms(xb0_ref[...])

    g = gates_ref[0]
    bcum = gates_ref[1]
    g_t = gates_t_ref[0]
    bcum_t = gates_t_ref[1]
    neg_inf = jnp.float32(-jnp.inf)
    heads = range(nh)
    cols = [slice(h * dh, (h + 1) * dh) for h in heads]

    q = [q_ref[:, cols[h]] * jnp.asarray(dh ** -0.5, BF16) for h in heads]
    k = [k_ref[:, cols[h]] for h in heads]
    v = [v_ref[:, cols[h]] for h in heads]
    i_col = [g[:, h:h + 1] for h in heads]
    b_col = [bcum[:, nh + h:nh + h + 1] for h in heads]
    i_row = [g_t[h:h + 1, :] for h in heads]
    b_row = [bcum_t[nh + h:nh + h + 1, :] for h in heads]
    g_tot = [b_col[h][chunk - 1:chunk, :] for h in heads]
    m_prev = [m_ref[h, 0:1, 0:1] for h in heads]
    c_prev = [c_ref[h] for h in heads]
    n_prev = [n_ref[h, 0:1, :] for h in heads]

    qk = [_dot_t(q[h], k[h]) for h in heads]
    inter = [_dot(q[h], c_prev[h].astype(BF16)) for h in heads]
    qn = [jnp.sum(q[h].astype(F32) * n_prev[h], axis=-1, keepdims=True) for h in heads]

    log_d = [jnp.where(causal, b_col[h] - (b_row[h] - i_row[h]), neg_inf) for h in heads]
    inter_log = [b_col[h] + m_prev[h] for h in heads]
    m_t = [jnp.maximum(inter_log[h], jnp.max(log_d[h], axis=-1, keepdims=True)) for h in heads]
    scores = [qk[h] * jnp.exp(log_d[h] - m_t[h]) for h in heads]
    inter_w = [jnp.exp(inter_log[h] - m_t[h]) for h in heads]
    num = [_dot(scores[h].astype(BF16), v[h]) + inter_w[h] * inter[h] for h in heads]
    den = [jnp.sum(scores[h], axis=-1, keepdims=True) + inter_w[h] * qn[h] for h in heads]
    hh = [num[h] / jnp.maximum(jnp.abs(den[h]), jnp.exp(-m_t[h])) for h in heads]

    w_log = [g_tot[h] - b_col[h] + i_col[h] for h in heads]
    m_new = [jnp.maximum(g_tot[h] + m_prev[h], jnp.max(w_log[h], axis=0, keepdims=True))
             for h in heads]
    decay = [jnp.exp(g_tot[h] + m_prev[h] - m_new[h]) for h in heads]
    kw = [k[h].astype(F32) * jnp.exp(w_log[h] - m_new[h]) for h in heads]
    for h in heads:
        c_ref[h] = decay[h] * c_prev[h] + lax.dot_general(
            kw[h].astype(BF16), v[h], (((0,), (0,)), ((), ())), preferred_element_type=F32)
        n_ref[h, 0:1, :] = decay[h] * n_prev[h] + jnp.sum(kw[h], axis=0, keepdims=True)
        m_ref[h] = jnp.broadcast_to(m_new[h], m_ref.shape[1:])

    mu = [jnp.mean(hh[h], axis=-1, keepdims=True) for h in heads]
    hc = [hh[h] - mu[h] for h in heads]
    var = [jnp.mean(hc[h] * hc[h], axis=-1, keepdims=True) for h in heads]
    for h in heads:
        hn = hc[h] * lax.rsqrt(var[h] + HEAD_NORM_EPS) * gain_ref[:, cols[h]]
        o = o_ref[:, cols[h]].astype(F32)
        y_ref[:, cols[h]] = (jax.nn.sigmoid(o) * hn).astype(y_ref.dtype)

    gate_terms(xbn_ref[...])


def _mlstm(qkvo, xb2d, w_gates_t_bf, b_gates_row, gain_row, *, batch, seq, nh, dh):
    m, d = xb2d.shape
    w = nh * dh
    chunk = MLSTM_CHUNK
    nc = seq // chunk

    def blk(kind):
        return pl.BlockSpec((chunk, w), lambda b, c: (b * nc + c, kind))

    kern = functools.partial(_mlstm_kernel, nh=nh, dh=dh)
    return pl.pallas_call(
        kern,
        out_shape=jax.ShapeDtypeStruct((m, w), BF16),
        grid=(batch, nc),
        in_specs=[blk(0), blk(1), blk(2), blk(3),
                  pl.BlockSpec((chunk, d), lambda b, c: (0, 0)),
                  pl.BlockSpec((chunk, d),
                               lambda b, c: (jnp.minimum(b * nc + c + 1, batch * nc - 1), 0)),
                  pl.BlockSpec((LANES, d), lambda b, c: (0, 0)),
                  pl.BlockSpec((1, LANES), lambda b, c: (0, 0)),
                  pl.BlockSpec((1, w), lambda b, c: (0, 0))],
        out_specs=pl.BlockSpec((chunk, w), lambda b, c: (b * nc + c, 0)),
        scratch_shapes=[
            pltpu.VMEM((nh, dh, dh), F32),
            pltpu.VMEM((nh, SUBLANES, dh), F32),
            pltpu.VMEM((nh, SUBLANES, LANES), F32),
            pltpu.VMEM((2, chunk, LANES), F32),
            pltpu.VMEM((2, LANES, chunk), F32),
        ],
        compiler_params=_params(("arbitrary", "arbitrary")),
        name="mlstm_chunkwise",
    )(qkvo, qkvo, qkvo, qkvo, xb2d, xb2d, w_gates_t_bf, b_gates_row, gain_row)


def _layer_norm(z, g, b):
    mu = jnp.mean(z, axis=-1, keepdims=True)
    zc = z - mu
    var = jnp.mean(zc * zc, axis=-1, keepdims=True)
    return zc * lax.rsqrt(var + LN_EPS) * g + b


def _outproj_ln_kernel(x_ref, yc_ref, ym_ref, wc_ref, wm_ref, g_ref, b_ref, o_ref, *,
                       alpha, chunk):
    for r0 in range(0, x_ref.shape[0], chunk):
        rs = slice(r0, r0 + chunk)
        mix = _dot(yc_ref[rs, :], wc_ref[...]) + _dot(ym_ref[rs, :], wm_ref[...])
        z = alpha * x_ref[rs, :] + mix
        o_ref[rs, :] = _layer_norm(z, g_ref[...], b_ref[...])


def _outproj_ln(x2d, y_conv, y_mlstm, w_out_bf, ln_g, ln_b, *, alpha):
    m, d = x2d.shape
    wc = y_conv.shape[1]
    wm = y_mlstm.shape[1]
    assert wc == wm
    tm = TM_OUT
    kern = functools.partial(_outproj_ln_kernel, alpha=alpha, chunk=TR_OUT)
    return pl.pallas_call(
        kern,
        out_shape=jax.ShapeDtypeStruct((m, d), F32),
        grid=(m // tm,),
        in_specs=[
            pl.BlockSpec((tm, d), lambda i: (i, 0)),
            pl.BlockSpec((tm, wc), lambda i: (i, 0)),
            pl.BlockSpec((tm, wm), lambda i: (i, 0)),
            pl.BlockSpec((wc, d), lambda i: (0, 0)),
            pl.BlockSpec((wm, d), lambda i: (1, 0)),
            pl.BlockSpec((1, d), lambda i: (0, 0)),
            pl.BlockSpec((1, d), lambda i: (0, 0)),
        ],
        out_specs=pl.BlockSpec((tm, d), lambda i: (i, 0)),
        compiler_params=_params(("arbitrary",)),
        name="outproj_ln",
    )(x2d, y_conv, y_mlstm, w_out_bf, w_out_bf, ln_g, ln_b)


def _convffn_ln_kernel(x_ref, wv_ref, wg_ref, cwv_ref, cwg_ref, cbv_ref, cbg_ref, wd_ref,
                       g_ref, b_ref, o_ref, xb_ref, acc_ref, carry_v_ref, carry_g_ref,
                       buf_v_ref, buf_g_ref, *, alpha, tiles_per_seq, chunk):
    i = pl.program_id(0)
    f = pl.program_id(1)
    nf = pl.num_programs(1)
    rows = x_ref.shape[0]
    first = (i % tiles_per_seq) == 0

    @pl.when(f == 0)
    def _():
        xb_ref[...] = x_ref[...].astype(BF16)
        acc_ref[...] = jnp.zeros(acc_ref.shape, F32)

    @pl.when(first)
    def _():
        carry_v_ref[f] = jnp.zeros(carry_v_ref.shape[1:], F32)
        carry_g_ref[f] = jnp.zeros(carry_g_ref.shape[1:], F32)

    buf_v_ref[0:SUBLANES, :] = carry_v_ref[f]
    buf_g_ref[0:SUBLANES, :] = carry_g_ref[f]

    def conv(buf_ref, cw_ref, cb_ref, r0):
        h = buf_ref[SUBLANES + r0:SUBLANES + r0 + chunk, :]
        hm1 = buf_ref[SUBLANES - 1 + r0:SUBLANES - 1 + r0 + chunk, :]
        hm2 = buf_ref[SUBLANES - 2 + r0:SUBLANES - 2 + r0 + chunk, :]
        w = cw_ref[...]
        return w[0:1, :] * hm2 + w[1:2, :] * hm1 + w[2:3, :] * h + cb_ref[...]

    row_starts = list(range(0, rows, chunk))
    for r0 in row_starts:
        xb = xb_ref[r0:r0 + chunk, :]
        buf_v_ref[SUBLANES + r0:SUBLANES + r0 + chunk, :] = _dot(xb, wv_ref[...])
        buf_g_ref[SUBLANES + r0:SUBLANES + r0 + chunk, :] = _dot(xb, wg_ref[...])
    for r0 in row_starts:
        val = conv(buf_v_ref, cwv_ref, cbv_ref, r0)
        gate = conv(buf_g_ref, cwg_ref, cbg_ref, r0)
        act = (jax.nn.silu(gate) * val).astype(BF16)
        acc_ref[r0:r0 + chunk, :] += _dot(act, wd_ref[...])
    carry_v_ref[f] = buf_v_ref[rows:rows + SUBLANES, :]
    carry_g_ref[f] = buf_g_ref[rows:rows + SUBLANES, :]

    @pl.when(f == nf - 1)
    def _():
        z = alpha * x_ref[...] + acc_ref[...]
        o_ref[...] = _layer_norm(z, g_ref[...], b_ref[...])


def _convffn_ln(x2d, w_up_bf, w_ffn_conv, b_ffn_conv_row, w_down_bf, ln_g, ln_b, *,
                alpha, seq, d_ff):
    m, d = x2d.shape
    tm, tf = TM_FFN, TF_FFN
    nf = d_ff // tf
    kern = functools.partial(_convffn_ln_kernel, alpha=alpha, tiles_per_seq=seq // tm,
                             chunk=TR_FFN)
    return pl.pallas_call(
        kern,
        out_shape=jax.ShapeDtypeStruct((m, d), F32),
        grid=(m // tm, nf),
        in_specs=[
            pl.BlockSpec((tm, d), lambda i, f: (i, 0)),
            pl.BlockSpec((d, tf), lambda i, f: (0, f)),
            pl.BlockSpec((d, tf), lambda i, f: (0, nf + f)),
            pl.BlockSpec((CONV_WIDTH, tf), lambda i, f: (0, f)),
            pl.BlockSpec((CONV_WIDTH, tf), lambda i, f: (0, nf + f)),
            pl.BlockSpec((1, tf), lambda i, f: (0, f)),
            pl.BlockSpec((1, tf), lambda i, f: (0, nf + f)),
            pl.BlockSpec((tf, d), lambda i, f: (f, 0)),
            pl.BlockSpec((1, d), lambda i, f: (0, 0)),
            pl.BlockSpec((1, d), lambda i, f: (0, 0)),
        ],
        out_specs=pl.BlockSpec((tm, d), lambda i, f: (i, 0)),
        scratch_shapes=[
            pltpu.VMEM((tm, d), BF16),
            pltpu.VMEM((tm, d), F32),
            pltpu.VMEM((nf, SUBLANES, tf), F32),
            pltpu.VMEM((nf, SUBLANES, tf), F32),
            pltpu.VMEM((tm + SUBLANES, tf), F32),
            pltpu.VMEM((tm + SUBLANES, tf), F32),
        ],
        compiler_params=_params(("arbitrary", "arbitrary")),
        name="convffn_ln",
    )(x2d, w_up_bf, w_up_bf, w_ffn_conv, w_ffn_conv, b_ffn_conv_row, b_ffn_conv_row,
      w_down_bf, ln_g, ln_b)


def kernel(x, w_in, b_gates, w_sc_conv, mh_gain, w_out, ln1_g, ln1_b,
           w_up, w_ffn_conv, b_ffn_conv, w_down, ln2_g, ln2_b):
    batch, seq, d = x.shape
    depth = w_in.shape[0]
    nh = N_HEADS_M
    w_conv = w_sc_conv.shape[-1]
    w_mlstm = mh_gain.shape[-1]
    dh = w_mlstm // nh
    d_ff = w_down.shape[1]
    n_gates = 2 * nh
    gate_col0 = 3 * w_conv + 4 * w_mlstm
    assert w_in.shape[-1] == gate_col0 + n_gates and w_conv + w_mlstm == d
    assert seq % TM_CONV == 0 and seq % TM_FFN == 0 and seq % MLSTM_CHUNK == 0
    assert w_conv % TC_CONV == 0 and d_ff % TF_FFN == 0
    assert (3 * w_conv) % TN_QKVO == 0 and (4 * w_mlstm) % TN_QKVO == 0
    assert (batch * seq) % TM_QKVO == 0 and (batch * seq) % TM_OUT == 0
    alpha = (2 * depth) ** 0.25

    x2d = x.reshape(batch * seq, d)
    for l in range(depth):
        w_in_t = w_in[l].T
        w_gates_t_bf = jnp.pad(w_in_t[gate_col0:], ((0, LANES - n_gates), (0, 0))).astype(BF16)
        b_gates_row = jnp.pad(b_gates[l], (0, LANES - n_gates)).reshape(1, LANES)

        y_conv, xb2d, w_down_bf, w_out_bf = _inproj_conv(
            x2d, w_in_t, w_sc_conv[l], w_down[l], w_out[l], seq=seq, w_conv=w_conv)
        qkvo, w_up_bf = _inproj_qkvo(xb2d, w_in_t, w_up[l], col0=3 * w_conv, width=4 * w_mlstm)
        y_mlstm = _mlstm(qkvo, xb2d, w_gates_t_bf, b_gates_row, mh_gain[l].reshape(1, w_mlstm),
                         batch=batch, seq=seq, nh=nh, dh=dh)
        x2d = _outproj_ln(x2d, y_conv, y_mlstm, w_out_bf,
                          ln1_g[l].reshape(1, d), ln1_b[l].reshape(1, d), alpha=alpha)
        x2d = _convffn_ln(x2d, w_up_bf, w_ffn_conv[l],
                          b_ffn_conv[l].reshape(1, 2 * d_ff), w_down_bf,
                          ln2_g[l].reshape(1, d), ln2_b[l].reshape(1, d),
                          alpha=alpha, seq=seq, d_ff=d_ff)
    return x2d.reshape(batch, seq, d)
```

```python
import functools

import jax
import jax.numpy as jnp
from jax import lax
from jax.experimental import pallas as pl
from jax.experimental.pallas import tpu as pltpu

F32 = jnp.float32
BF16 = jnp.bfloat16

N_HEADS_M = 4
CONV_WIDTH = 3
LN_EPS = 1e-5
HEAD_NORM_EPS = 1e-6

SUBLANES = 8
LANES = 128
BF16_SUBLANES = 16
VMEM_LIMIT_BYTES = 56 * 1024 * 1024

TM_CONV = 512
TC_CONV = 512
TR_CONV = 256
TM_QKVO = 1024
TN_QKVO = 1024
MLSTM_CHUNK = 256
TM_OUT = 512
TR_OUT = 256
TM_FFN = 512
TF_FFN = 512
TR_FFN = 256


def _dot(a, b):
    return jnp.dot(a, b, preferred_element_type=F32)


def _dot_t(a, b_t):
    return lax.dot_general(a, b_t, (((1,), (1,)), ((), ())), preferred_element_type=F32)


def _params(semantics, flags=None):
    return pltpu.CompilerParams(dimension_semantics=semantics,
                                vmem_limit_bytes=VMEM_LIMIT_BYTES, flags=flags)


def _cast_rows_spec(rows, cols, n_steps, step_of):
    assert rows % n_steps == 0 and (rows // n_steps) % BF16_SUBLANES == 0
    return pl.BlockSpec((rows // n_steps, cols), lambda *ids: (step_of(*ids), 0))


def _inproj_conv_kernel(x_ref, wb_ref, wc_ref, wh_ref, cw_ref, wdn_ref, wout_ref,
                        y_ref, xb_ref, wdn_bf_ref, wout_bf_ref,
                        wbf_ref, carry_ref, buf_ref, *, tiles_per_seq, chunk):
    i = pl.program_id(1)
    rows = x_ref.shape[0]

    @pl.when(i == 0)
    def _():
        wbf_ref[0] = wb_ref[...].astype(BF16)
        wbf_ref[1] = wc_ref[...].astype(BF16)
        wbf_ref[2] = wh_ref[...].astype(BF16)

    @pl.when(pl.program_id(0) == 0)
    def _():
        xb_ref[...] = x_ref[...].astype(BF16)

    @pl.when(i % tiles_per_seq == 0)
    def _():
        carry_ref[...] = jnp.zeros(carry_ref.shape, F32)

    wdn_bf_ref[...] = wdn_ref[...].astype(BF16)
    wout_bf_ref[...] = wout_ref[...].astype(BF16)

    buf_ref[0:SUBLANES, :] = carry_ref[...]
    w = cw_ref[...]
    row_starts = list(range(0, rows, chunk))
    xbs = [x_ref[r0:r0 + chunk, :].astype(BF16) for r0 in row_starts]
    for r0, xb in zip(row_starts, xbs):
        u = _dot_t(xb, wbf_ref[1]) * _dot_t(xb, wbf_ref[2])
        buf_ref[SUBLANES + r0:SUBLANES + r0 + chunk, :] = u
    c_bs = [_dot_t(xb, wbf_ref[0]) for xb in xbs]
    for r0, c_b in zip(row_starts, c_bs):
        u = buf_ref[SUBLANES + r0:SUBLANES + r0 + chunk, :]
        um2 = buf_ref[SUBLANES - 2 + r0:SUBLANES - 2 + r0 + chunk, :]
        um1 = buf_ref[SUBLANES - 1 + r0:SUBLANES - 1 + r0 + chunk, :]
        y = w[0:1, :] * um2 + w[1:2, :] * um1 + w[2:3, :] * u
        y_ref[r0:r0 + chunk, :] = (c_b * y).astype(y_ref.dtype)
    carry_ref[...] = buf_ref[rows:rows + SUBLANES, :]


def _inproj_conv(x2d, w_in_t, w_sc_conv, w_down, w_out, *, seq, w_conv):
    m, d = x2d.shape
    tm, tc = TM_CONV, TC_CONV
    nj, ni = w_conv // tc, m // tm
    n_steps = nj * ni
    step_of = lambda j, i: j * ni + i
    once = pl.Buffered(1)
    kern = functools.partial(_inproj_conv_kernel, tiles_per_seq=seq // tm, chunk=TR_CONV)
    return pl.pallas_call(
        kern,
        out_shape=(jax.ShapeDtypeStruct((m, w_conv), BF16),
                   jax.ShapeDtypeStruct((m, d), BF16),
                   jax.ShapeDtypeStruct(w_down.shape, BF16),
                   jax.ShapeDtypeStruct(w_out.shape, BF16)),
        grid=(nj, ni),
        in_specs=[
            pl.BlockSpec((tm, d), lambda j, i: (i, 0)),
            pl.BlockSpec((tc, d), lambda j, i: (j, 0), pipeline_mode=once),
            pl.BlockSpec((tc, d), lambda j, i: (nj + j, 0), pipeline_mode=once),
            pl.BlockSpec((tc, d), lambda j, i: (2 * nj + j, 0), pipeline_mode=once),
            pl.BlockSpec((CONV_WIDTH, tc), lambda j, i: (0, j)),
            _cast_rows_spec(*w_down.shape, n_steps, step_of),
            _cast_rows_spec(*w_out.shape, n_steps, step_of),
        ],
        out_specs=(pl.BlockSpec((tm, tc), lambda j, i: (i, j)),
                   pl.BlockSpec((tm, d), lambda j, i: (jnp.where(j == 0, i, ni - 1), 0)),
                   _cast_rows_spec(*w_down.shape, n_steps, step_of),
                   _cast_rows_spec(*w_out.shape, n_steps, step_of)),
        scratch_shapes=[
            pltpu.VMEM((3, tc, d), BF16),
            pltpu.VMEM((SUBLANES, tc), F32),
            pltpu.VMEM((tm + SUBLANES, tc), F32),
        ],
        compiler_params=_params(("arbitrary", "arbitrary")),
        name="inproj_shortconv",
    )(x2d, w_in_t, w_in_t, w_in_t, w_sc_conv, w_down, w_out)


def _inproj_qkvo_kernel(xb_ref, w_ref, wup_ref, o_ref, wup_bf_ref, wbf_ref):
    @pl.when(pl.program_id(1) == 0)
    def _():
        wbf_ref[...] = w_ref[...].astype(BF16)

    wup_bf_ref[...] = wup_ref[...].astype(BF16)
    o_ref[...] = _dot_t(xb_ref[...], wbf_ref[...]).astype(o_ref.dtype)


def _inproj_qkvo(xb2d, w_in_t, w_up, *, col0, width):
    m, d = xb2d.shape
    tm, tn = TM_QKVO, TN_QKVO
    j0 = col0 // tn
    nj, ni = width // tn, m // tm
    n_steps = nj * ni
    step_of = lambda j, i: j * ni + i
    return pl.pallas_call(
        _inproj_qkvo_kernel,
        out_shape=(jax.ShapeDtypeStruct((m, width), BF16),
                   jax.ShapeDtypeStruct(w_up.shape, BF16)),
        grid=(nj, ni),
        in_specs=[
            pl.BlockSpec((tm, d), lambda j, i: (i, 0)),
            pl.BlockSpec((tn, d), lambda j, i: (j0 + j, 0)),
            _cast_rows_spec(*w_up.shape, n_steps, step_of),
        ],
        out_specs=(pl.BlockSpec((tm, tn), lambda j, i: (i, j)),
                   _cast_rows_spec(*w_up.shape, n_steps, step_of)),
        scratch_shapes=[pltpu.VMEM((tn, d), BF16)],
        compiler_params=_params(("arbitrary", "arbitrary")),
        name="inproj_mlstm",
    )(xb2d, w_in_t, w_up)


def _log_sigmoid(x):
    return jnp.minimum(x, 0.0) - jnp.log1p(jnp.exp(-jnp.abs(x)))


def _mlstm_kernel(q_ref, k_ref, v_ref, o_ref, xb0_ref, xbn_ref, wg_ref, bg_ref, gain_ref, y_ref,
                  c_ref, n_ref, m_ref, gates_ref, gates_t_ref, *, nh, dh):
    chunk = q_ref.shape[0]
    step = pl.program_id(0) * pl.num_programs(1) + pl.program_id(1)

    @pl.when(pl.program_id(1) == 0)
    def _():
        c_ref[...] = jnp.zeros(c_ref.shape, F32)
        n_ref[...] = jnp.zeros(n_ref.shape, F32)
        m_ref[...] = jnp.zeros(m_ref.shape, F32)

    row = lax.broadcasted_iota(jnp.int32, (chunk, chunk), 0)
    col = lax.broadcasted_iota(jnp.int32, (chunk, chunk), 1)
    causal = row >= col

    def gate_terms(xb):
        g = _dot_t(xb, wg_ref[...]) + bg_ref[...]
        lf = _log_sigmoid(g)
        bcum = jnp.dot(causal.astype(F32), lf, precision=lax.Precision.HIGHEST,
                       preferred_element_type=F32)
        gates_ref[0] = g
        gates_ref[1] = bcum
        gates_t_ref[0] = g.T
        gates_t_ref[1] = bcum.T

    @pl.when(step == 0)
    def _():
        gate_terms(xb0_ref[...])

    g = gates_ref[0]
    bcum = gates_ref[1]
    g_t = gates_t_ref[0]
    bcum_t = gates_t_ref[1]
    neg_inf = jnp.float32(-jnp.inf)
    heads = range(nh)
    cols = [slice(h * dh, (h + 1) * dh) for h in heads]

    q = [q_ref[:, cols[h]] * jnp.asarray(dh ** -0.5, BF16) for h in heads]
    k = [k_ref[:, cols[h]] for h in heads]
    v = [v_ref[:, cols[h]] for h in heads]
    i_col = [g[:, h:h + 1] for h in heads]
    b_col = [bcum[:, nh + h:nh + h + 1] for h in heads]
    i_row = [g_t[h:h + 1, :] for h in heads]
    b_row = [bcum_t[nh + h:nh + h + 1, :] for h in heads]
    g_tot = [b_col[h][chunk - 1:chunk, :] for h in heads]
    m_prev = [m_ref[h, 0:1, 0:1] for h in heads]
    c_prev = [c_ref[h] for h in heads]
    n_prev = [n_ref[h, 0:1, :] for h in heads]

    qk = [_dot_t(q[h], k[h]) for h in heads]
    inter = [_dot(q[h], c_prev[h].astype(BF16)) for h in heads]
    qn = [jnp.sum(q[h].astype(F32) * n_prev[h], axis=-1, keepdims=True) for h in heads]

    log_d = [jnp.where(causal, b_col[h] - (b_row[h] - i_row[h]), neg_inf) for h in heads]
    inter_log = [b_col[h] + m_prev[h] for h in heads]
    m_t = [jnp.maximum(inter_log[h], jnp.max(log_d[h], axis=-1, keepdims=True)) for h in heads]
    scores = [qk[h] * jnp.exp(log_d[h] - m_t[h]) for h in heads]
    inter_w = [jnp.exp(inter_log[h] - m_t[h]) for h in heads]
    num = [_dot(scores[h].astype(BF16), v[h]) + inter_w[h] * inter[h] for h in heads]
    den = [jnp.sum(scores[h], axis=-1, keepdims=True) + inter_w[h] * qn[h] for h in heads]
    hh = [num[h] / jnp.maximum(jnp.abs(den[h]), jnp.exp(-m_t[h])) for h in heads]

    w_log = [g_tot[h] - b_col[h] + i_col[h] for h in heads]
    m_new = [jnp.maximum(g_tot[h] + m_prev[h], jnp.max(w_log[h], axis=0, keepdims=True))
             for h in heads]
    decay = [jnp.exp(g_tot[h] + m_prev[h] - m_new[h]) for h in heads]
    kw = [k[h].astype(F32) * jnp.exp(w_log[h] - m_new[h]) for h in heads]
    for h in heads:
        c_ref[h] = decay[h] * c_prev[h] + lax.dot_general(
            kw[h].astype(BF16), v[h], (((0,), (0,)), ((), ())), preferred_element_type=F32)
        n_ref[h, 0:1, :] = decay[h] * n_prev[h] + jnp.sum(kw[h], axis=0, keepdims=True)
        m_ref[h] = jnp.broadcast_to(m_new[h], m_ref.shape[1:])

    mu = [jnp.mean(hh[h], axis=-1, keepdims=True) for h in heads]
    hc = [hh[h] - mu[h] for h in heads]
    var = [jnp.mean(hc[h] * hc[h], axis=-1, keepdims=True) for h in heads]
    for h in heads:
        hn = hc[h] * lax.rsqrt(var[h] + HEAD_NORM_EPS) * gain_ref[:, cols[h]]
        o = o_ref[:, cols[h]].astype(F32)
        y_ref[:, cols[h]] = (jax.nn.sigmoid(o) * hn).astype(y_ref.dtype)

    gate_terms(xbn_ref[...])


def _mlstm(qkvo, xb2d, w_gates_t_bf, b_gates_row, gain_row, *, batch, seq, nh, dh):
    m, d = xb2d.shape
    w = nh * dh
    chunk = MLSTM_CHUNK
    nc = seq // chunk

    def blk(kind):
        return pl.BlockSpec((chunk, w), lambda b, c: (b * nc + c, kind))

    kern = functools.partial(_mlstm_kernel, nh=nh, dh=dh)
    return pl.pallas_call(
        kern,
        out_shape=jax.ShapeDtypeStruct((m, w), BF16),
        grid=(batch, nc),
        in_specs=[blk(0), blk(1), blk(2), blk(3),
                  pl.BlockSpec((chunk, d), lambda b, c: (0, 0)),
                  pl.BlockSpec((chunk, d),
                               lambda b, c: (jnp.minimum(b * nc + c + 1, batch * nc - 1), 0)),
                  pl.BlockSpec((LANES, d), lambda b, c: (0, 0)),
                  pl.BlockSpec((1, LANES), lambda b, c: (0, 0)),
                  pl.BlockSpec((1, w), lambda b, c: (0, 0))],
        out_specs=pl.BlockSpec((chunk, w), lambda b, c: (b * nc + c, 0)),
        scratch_shapes=[
            pltpu.VMEM((nh, dh, dh), F32),
            pltpu.VMEM((nh, SUBLANES, dh), F32),
            pltpu.VMEM((nh, SUBLANES, LANES), F32),
            pltpu.VMEM((2, chunk, LANES), F32),
            pltpu.VMEM((2, LANES, chunk), F32),
        ],
        compiler_params=_params(("arbitrary", "arbitrary")),
        name="mlstm_chunkwise",
    )(qkvo, qkvo, qkvo, qkvo, xb2d, xb2d, w_gates_t_bf, b_gates_row, gain_row)


def _layer_norm(z, g, b):
    mu = jnp.mean(z, axis=-1, keepdims=True)
    zc = z - mu
    var = jnp.mean(zc * zc, axis=-1, keepdims=True)
    return zc * lax.rsqrt(var + LN_EPS) * g + b


def _outproj_ln_kernel(x_ref, yc_ref, ym_ref, wc_ref, wm_ref, g_ref, b_ref, o_ref, *,
                       alpha, chunk):
    for r0 in range(0, x_ref.shape[0], chunk):
        rs = slice(r0, r0 + chunk)
        mix = _dot(yc_ref[rs, :], wc_ref[...]) + _dot(ym_ref[rs, :], wm_ref[...])
        z = alpha * x_ref[rs, :] + mix
        o_ref[rs, :] = _layer_norm(z, g_ref[...], b_ref[...])


def _outproj_ln(x2d, y_conv, y_mlstm, w_out_bf, ln_g, ln_b, *, alpha):
    m, d = x2d.shape
    wc = y_conv.shape[1]
    wm = y_mlstm.shape[1]
    assert wc == wm
    tm = TM_OUT
    kern = functools.partial(_outproj_ln_kernel, alpha=alpha, chunk=TR_OUT)
    return pl.pallas_call(
        kern,
        out_shape=jax.ShapeDtypeStruct((m, d), F32),
        grid=(m // tm,),
        in_specs=[
            pl.BlockSpec((tm, d), lambda i: (i, 0)),
            pl.BlockSpec((tm, wc), lambda i: (i, 0)),
            pl.BlockSpec((tm, wm), lambda i: (i, 0)),
            pl.BlockSpec((wc, d), lambda i: (0, 0)),
            pl.BlockSpec((wm, d), lambda i: (1, 0)),
            pl.BlockSpec((1, d), lambda i: (0, 0)),
            pl.BlockSpec((1, d), lambda i: (0, 0)),
        ],
        out_specs=pl.BlockSpec((tm, d), lambda i: (i, 0)),
        compiler_params=_params(("arbitrary",)),
        name="outproj_ln",
    )(x2d, y_conv, y_mlstm, w_out_bf, w_out_bf, ln_g, ln_b)


def _convffn_ln_kernel(x_ref, wv_ref, wg_ref, cwv_ref, cwg_ref, cbv_ref, cbg_ref, wd_ref,
                       g_ref, b_ref, o_ref, xb_ref, acc_ref, carry_v_ref, carry_g_ref,
                       buf_v_ref, buf_g_ref, *, alpha, tiles_per_seq, chunk):
    i = pl.program_id(0)
    f = pl.program_id(1)
    nf = pl.num_programs(1)
    rows = x_ref.shape[0]
    first = (i % tiles_per_seq) == 0

    @pl.when(first)
    def _():
        carry_v_ref[f] = jnp.zeros(carry_v_ref.shape[1:], F32)
        carry_g_ref[f] = jnp.zeros(carry_g_ref.shape[1:], F32)

    def conv(buf_ref, cw_ref, cb_ref, r0):
        h = buf_ref[SUBLANES + r0:SUBLANES + r0 + chunk, :]
        hm1 = buf_ref[SUBLANES - 1 + r0:SUBLANES - 1 + r0 + chunk, :]
        hm2 = buf_ref[SUBLANES - 2 + r0:SUBLANES - 2 + r0 + chunk, :]
        w = cw_ref[...]
        return w[0:1, :] * hm2 + w[1:2, :] * hm1 + w[2:3, :] * h + cb_ref[...]

    def step(is_first, is_last):
        row_starts = list(range(0, rows, chunk))
        buf_v_ref[0:SUBLANES, :] = carry_v_ref[f]
        buf_g_ref[0:SUBLANES, :] = carry_g_ref[f]
        for r0 in row_starts:
            if is_first:
                xb_ref[r0:r0 + chunk, :] = x_ref[r0:r0 + chunk, :].astype(BF16)
            xb = xb_ref[r0:r0 + chunk, :]
            buf_v_ref[SUBLANES + r0:SUBLANES + r0 + chunk, :] = _dot(xb, wv_ref[...])
            buf_g_ref[SUBLANES + r0:SUBLANES + r0 + chunk, :] = _dot(xb, wg_ref[...])
        for r0 in row_starts:
            rs = slice(r0, r0 + chunk)
            val = conv(buf_v_ref, cwv_ref, cbv_ref, r0)
            gate = conv(buf_g_ref, cwg_ref, cbg_ref, r0)
            act = (jax.nn.silu(gate) * val).astype(BF16)
            part = _dot(act, wd_ref[...])
            if is_first:
                acc_ref[rs, :] = part
            elif is_last:
                z = alpha * x_ref[rs, :] + (acc_ref[rs, :] + part)
                o_ref[rs, :] = _layer_norm(z, g_ref[...], b_ref[...])
            else:
                acc_ref[rs, :] += part
        carry_v_ref[f] = buf_v_ref[rows:rows + SUBLANES, :]
        carry_g_ref[f] = buf_g_ref[rows:rows + SUBLANES, :]

    pl.when(f == 0)(functools.partial(step, True, False))
    pl.when(jnp.logical_and(f > 0, f < nf - 1))(functools.partial(step, False, False))
    pl.when(f == nf - 1)(functools.partial(step, False, True))


def _convffn_ln(x2d, w_up_bf, w_ffn_conv, b_ffn_conv_row, w_down_bf, ln_g, ln_b, *,
                alpha, seq, d_ff):
    m, d = x2d.shape
    tm, tf = TM_FFN, TF_FFN
    nf = d_ff // tf
    kern = functools.partial(_convffn_ln_kernel, alpha=alpha, tiles_per_seq=seq // tm,
                             chunk=TR_FFN)
    return pl.pallas_call(
        kern,
        out_shape=jax.ShapeDtypeStruct((m, d), F32),
        grid=(m // tm, nf),
        in_specs=[
            pl.BlockSpec((tm, d), lambda i, f: (i, 0)),
            pl.BlockSpec((d, tf), lambda i, f: (0, f)),
            pl.BlockSpec((d, tf), lambda i, f: (0, nf + f)),
            pl.BlockSpec((CONV_WIDTH, tf), lambda i, f: (0, f)),
            pl.BlockSpec((CONV_WIDTH, tf), lambda i, f: (0, nf + f)),
            pl.BlockSpec((1, tf), lambda i, f: (0, f)),
            pl.BlockSpec((1, tf), lambda i, f: (0, nf + f)),
            pl.BlockSpec((tf, d), lambda i, f: (f, 0)),
            pl.BlockSpec((1, d), lambda i, f: (0, 0)),
            pl.BlockSpec((1, d), lambda i, f: (0, 0)),
        ],
        out_specs=pl.BlockSpec((tm, d), lambda i, f: (i, 0)),
        scratch_shapes=[
            pltpu.VMEM((tm, d), BF16),
            pltpu.VMEM((tm, d), F32),
            pltpu.VMEM((nf, SUBLANES, tf), F32),
            pltpu.VMEM((nf, SUBLANES, tf), F32),
            pltpu.VMEM((tm + SUBLANES, tf), F32),
            pltpu.VMEM((tm + SUBLANES, tf), F32),
        ],
        compiler_params=_params(("arbitrary", "arbitrary")),
        name="convffn_ln",
    )(x2d, w_up_bf, w_up_bf, w_ffn_conv, w_ffn_conv, b_ffn_conv_row, b_ffn_conv_row,
      w_down_bf, ln_g, ln_b)


def kernel(x, w_in, b_gates, w_sc_conv, mh_gain, w_out, ln1_g, ln1_b,
           w_up, w_ffn_conv, b_ffn_conv, w_down, ln2_g, ln2_b):
    batch, seq, d = x.shape
    depth = w_in.shape[0]
    nh = N_HEADS_M
    w_conv = w_sc_conv.shape[-1]
    w_mlstm = mh_gain.shape[-1]
    dh = w_mlstm // nh
    d_ff = w_down.shape[1]
    n_gates = 2 * nh
    gate_col0 = 3 * w_conv + 4 * w_mlstm
    assert w_in.shape[-1] == gate_col0 + n_gates and w_conv + w_mlstm == d
    assert seq % TM_CONV == 0 and seq % TM_FFN == 0 and seq % MLSTM_CHUNK == 0
    assert w_conv % TC_CONV == 0 and d_ff % TF_FFN == 0
    assert (3 * w_conv) % TN_QKVO == 0 and (4 * w_mlstm) % TN_QKVO == 0
    assert (batch * seq) % TM_QKVO == 0 and (batch * seq) % TM_OUT == 0
    alpha = (2 * depth) ** 0.25

    x2d = x.reshape(batch * seq, d)
    for l in range(depth):
        w_in_t = w_in[l].T
        w_gates_t_bf = jnp.pad(w_in_t[gate_col0:], ((0, LANES - n_gates), (0, 0))).astype(BF16)
        b_gates_row = jnp.pad(b_gates[l], (0, LANES - n_gates)).reshape(1, LANES)

        y_conv, xb2d, w_down_bf, w_out_bf = _inproj_conv(
            x2d, w_in_t, w_sc_conv[l], w_down[l], w_out[l], seq=seq, w_conv=w_conv)
        qkvo, w_up_bf = _inproj_qkvo(xb2d, w_in_t, w_up[l], col0=3 * w_conv, width=4 * w_mlstm)
        y_mlstm = _mlstm(qkvo, xb2d, w_gates_t_bf, b_gates_row, mh_gain[l].reshape(1, w_mlstm),
                         batch=batch, seq=seq, nh=nh, dh=dh)
        x2d = _outproj_ln(x2d, y_conv, y_mlstm, w_out_bf,
                          ln1_g[l].reshape(1, d), ln1_b[l].reshape(1, d), alpha=alpha)
        x2d = _convffn_ln(x2d, w_up_bf, w_ffn_conv[l],
                          b_ffn_conv[l].reshape(1, 2 * d_ff), w_down_bf,
                          ln2_g[l].reshape(1, d), ln2_b[l].reshape(1, d),
                          alpha=alpha, seq=seq, d_ff=d_ff)
    return x2d.reshape(batch, seq, d)
```

```python
import functools

import jax
import jax.numpy as jnp
from jax import lax
from jax.experimental import pallas as pl
from jax.experimental.pallas import tpu as pltpu

F32 = jnp.float32
BF16 = jnp.bfloat16

N_HEADS_M = 4
CONV_WIDTH = 3
LN_EPS = 1e-5
HEAD_NORM_EPS = 1e-6

SUBLANES = 8
LANES = 128
BF16_SUBLANES = 16
VMEM_LIMIT_BYTES = 56 * 1024 * 1024

TM_CONV = 512
TC_CONV = 512
TR_CONV = 256
TM_QKVO = 1024
TN_QKVO = 1024
MLSTM_CHUNK = 256
TM_OUT = 512
TR_OUT = 256
TM_FFN = 512
TF_FFN = 512
TR_FFN = 256


def _dot(a, b):
    return jnp.dot(a, b, preferred_element_type=F32)


def _dot_t(a, b_t):
    return lax.dot_general(a, b_t, (((1,), (1,)), ((), ())), preferred_element_type=F32)


def _params(semantics, flags=None):
    return pltpu.CompilerParams(dimension_semantics=semantics,
                                vmem_limit_bytes=VMEM_LIMIT_BYTES, flags=flags)


def _cast_rows_spec(rows, cols, n_steps, step_of):
    assert rows % n_steps == 0 and (rows // n_steps) % BF16_SUBLANES == 0
    return pl.BlockSpec((rows // n_steps, cols), lambda *ids: (step_of(*ids), 0))


def _inproj_conv_kernel(x_ref, wb_ref, wc_ref, wh_ref, cw_ref, wdn_ref, wout_ref,
                        y_ref, xb_ref, wdn_bf_ref, wout_bf_ref,
                        wbf_ref, carry_ref, buf_ref, *, tiles_per_seq, chunk):
    i = pl.program_id(1)
    rows = x_ref.shape[0]

    @pl.when(i == 0)
    def _():
        wbf_ref[0] = wb_ref[...].astype(BF16)
        wbf_ref[1] = wc_ref[...].astype(BF16)
        wbf_ref[2] = wh_ref[...].astype(BF16)

    @pl.when(pl.program_id(0) == 0)
    def _():
        xb_ref[...] = x_ref[...].astype(BF16)

    @pl.when(i % tiles_per_seq == 0)
    def _():
        carry_ref[...] = jnp.zeros(carry_ref.shape, F32)

    wdn_bf_ref[...] = wdn_ref[...].astype(BF16)
    wout_bf_ref[...] = wout_ref[...].astype(BF16)

    buf_ref[0:SUBLANES, :] = carry_ref[...]
    w = cw_ref[...]
    row_starts = list(range(0, rows, chunk))
    xbs = [x_ref[r0:r0 + chunk, :].astype(BF16) for r0 in row_starts]
    for r0, xb in zip(row_starts, xbs):
        u = _dot_t(xb, wbf_ref[1]) * _dot_t(xb, wbf_ref[2])
        buf_ref[SUBLANES + r0:SUBLANES + r0 + chunk, :] = u
    c_bs = [_dot_t(xb, wbf_ref[0]) for xb in xbs]
    for r0, c_b in zip(row_starts, c_bs):
        u = buf_ref[SUBLANES + r0:SUBLANES + r0 + chunk, :]
        um2 = buf_ref[SUBLANES - 2 + r0:SUBLANES - 2 + r0 + chunk, :]
        um1 = buf_ref[SUBLANES - 1 + r0:SUBLANES - 1 + r0 + chunk, :]
        y = w[0:1, :] * um2 + w[1:2, :] * um1 + w[2:3, :] * u
        y_ref[r0:r0 + chunk, :] = (c_b * y).astype(y_ref.dtype)
    carry_ref[...] = buf_ref[rows:rows + SUBLANES, :]


def _inproj_conv(x2d, w_in_t, w_sc_conv, w_down, w_out, *, seq, w_conv):
    m, d = x2d.shape
    tm, tc = TM_CONV, TC_CONV
    nj, ni = w_conv // tc, m // tm
    n_steps = nj * ni
    step_of = lambda j, i: j * ni + i
    once = pl.Buffered(1)
    kern = functools.partial(_inproj_conv_kernel, tiles_per_seq=seq // tm, chunk=TR_CONV)
    return pl.pallas_call(
        kern,
        out_shape=(jax.ShapeDtypeStruct((m, w_conv), BF16),
                   jax.ShapeDtypeStruct((m, d), BF16),
                   jax.ShapeDtypeStruct(w_down.shape, BF16),
                   jax.ShapeDtypeStruct(w_out.shape, BF16)),
        grid=(nj, ni),
        in_specs=[
            pl.BlockSpec((tm, d), lambda j, i: (i, 0)),
            pl.BlockSpec((tc, d), lambda j, i: (j, 0), pipeline_mode=once),
            pl.BlockSpec((tc, d), lambda j, i: (nj + j, 0), pipeline_mode=once),
            pl.BlockSpec((tc, d), lambda j, i: (2 * nj + j, 0), pipeline_mode=once),
            pl.BlockSpec((CONV_WIDTH, tc), lambda j, i: (0, j)),
            _cast_rows_spec(*w_down.shape, n_steps, step_of),
            _cast_rows_spec(*w_out.shape, n_steps, step_of),
        ],
        out_specs=(pl.BlockSpec((tm, tc), lambda j, i: (i, j)),
                   pl.BlockSpec((tm, d), lambda j, i: (jnp.where(j == 0, i, ni - 1), 0)),
                   _cast_rows_spec(*w_down.shape, n_steps, step_of),
                   _cast_rows_spec(*w_out.shape, n_steps, step_of)),
        scratch_shapes=[
            pltpu.VMEM((3, tc, d), BF16),
            pltpu.VMEM((SUBLANES, tc), F32),
            pltpu.VMEM((tm + SUBLANES, tc), F32),
        ],
        compiler_params=_params(("arbitrary", "arbitrary")),
        name="inproj_shortconv",
    )(x2d, w_in_t, w_in_t, w_in_t, w_sc_conv, w_down, w_out)


def _inproj_qkvo_kernel(xb_ref, w_ref, wup_ref, o_ref, wup_bf_ref, wbf_ref):
    @pl.when(pl.program_id(1) == 0)
    def _():
        wbf_ref[...] = w_ref[...].astype(BF16)

    n_blocks, _, tf = wup_bf_ref.shape
    for n in range(n_blocks):
        wup_bf_ref[n] = wup_ref[:, n * tf:(n + 1) * tf].astype(BF16)
    o_ref[...] = _dot_t(xb_ref[...], wbf_ref[...]).astype(o_ref.dtype)


def _inproj_qkvo(xb2d, w_in_t, w_up, *, col0, width, tf):
    m, d = xb2d.shape
    tm, tn = TM_QKVO, TN_QKVO
    j0 = col0 // tn
    nj, ni = width // tn, m // tm
    n_steps = nj * ni
    step_of = lambda j, i: j * ni + i
    up_rows, up_cols = w_up.shape
    n_blocks = up_cols // tf
    rows_per_step = up_rows // n_steps
    assert up_cols % tf == 0 and up_rows % n_steps == 0 and rows_per_step % BF16_SUBLANES == 0
    return pl.pallas_call(
        _inproj_qkvo_kernel,
        out_shape=(jax.ShapeDtypeStruct((m, width), BF16),
                   jax.ShapeDtypeStruct((n_blocks, up_rows, tf), BF16)),
        grid=(nj, ni),
        in_specs=[
            pl.BlockSpec((tm, d), lambda j, i: (i, 0)),
            pl.BlockSpec((tn, d), lambda j, i: (j0 + j, 0)),
            _cast_rows_spec(up_rows, up_cols, n_steps, step_of),
        ],
        out_specs=(pl.BlockSpec((tm, tn), lambda j, i: (i, j)),
                   pl.BlockSpec((n_blocks, rows_per_step, tf),
                                lambda j, i: (0, step_of(j, i), 0))),
        scratch_shapes=[pltpu.VMEM((tn, d), BF16)],
        compiler_params=_params(("arbitrary", "arbitrary")),
        name="inproj_mlstm",
    )(xb2d, w_in_t, w_up)


def _log_sigmoid(x):
    return jnp.minimum(x, 0.0) - jnp.log1p(jnp.exp(-jnp.abs(x)))


def _mlstm_kernel(q_ref, k_ref, v_ref, o_ref, xb0_ref, xbn_ref, wg_ref, bg_ref, gain_ref, y_ref,
                  c_ref, n_ref, m_ref, gates_ref, gates_t_ref, *, nh, dh):
    chunk = q_ref.shape[0]
    step = pl.program_id(0) * pl.num_programs(1) + pl.program_id(1)

    @pl.when(pl.program_id(1) == 0)
    def _():
        c_ref[...] = jnp.zeros(c_ref.shape, F32)
        n_ref[...] = jnp.zeros(n_ref.shape, F32)
        m_ref[...] = jnp.zeros(m_ref.shape, F32)

    row = lax.broadcasted_iota(jnp.int32, (chunk, chunk), 0)
    col = lax.broadcasted_iota(jnp.int32, (chunk, chunk), 1)
    causal = row >= col

    def gate_terms(xb):
        g = _dot_t(xb, wg_ref[...]) + bg_ref[...]
        lf = _log_sigmoid(g)
        bcum = jnp.dot(causal.astype(F32), lf, precision=lax.Precision.HIGHEST,
                       preferred_element_type=F32)
        gates_ref[0] = g
        gates_ref[1] = bcum
        gates_t_ref[0] = g.T
        gates_t_ref[1] = bcum.T

    @pl.when(step == 0)
    def _():
        gate_terms(xb0_ref[...])

    g = gates_ref[0]
    bcum = gates_ref[1]
    g_t = gates_t_ref[0]
    bcum_t = gates_t_ref[1]
    neg_inf = jnp.float32(-jnp.inf)
    heads = range(nh)
    cols = [slice(h * dh, (h + 1) * dh) for h in heads]

    q = [q_ref[:, cols[h]] * jnp.asarray(dh ** -0.5, BF16) for h in heads]
    k = [k_ref[:, cols[h]] for h in heads]
    v = [v_ref[:, cols[h]] for h in heads]
    i_col = [g[:, h:h + 1] for h in heads]
    b_col = [bcum[:, nh + h:nh + h + 1] for h in heads]
    i_row = [g_t[h:h + 1, :] for h in heads]
    b_row = [bcum_t[nh + h:nh + h + 1, :] for h in heads]
    g_tot = [b_col[h][chunk - 1:chunk, :] for h in heads]
    m_prev = [m_ref[h, 0:1, 0:1] for h in heads]
    c_prev = [c_ref[h] for h in heads]
    n_prev = [n_ref[h, 0:1, :] for h in heads]

    qk = [_dot_t(q[h], k[h]) for h in heads]
    inter = [_dot(q[h], c_prev[h].astype(BF16)) for h in heads]
    qn = [jnp.sum(q[h].astype(F32) * n_prev[h], axis=-1, keepdims=True) for h in heads]

    log_d = [jnp.where(causal, b_col[h] - (b_row[h] - i_row[h]), neg_inf) for h in heads]
    inter_log = [b_col[h] + m_prev[h] for h in heads]
    m_t = [jnp.maximum(inter_log[h], jnp.max(log_d[h], axis=-1, keepdims=True)) for h in heads]
    scores = [qk[h] * jnp.exp(log_d[h] - m_t[h]) for h in heads]
    inter_w = [jnp.exp(inter_log[h] - m_t[h]) for h in heads]
    num = [_dot(scores[h].astype(BF16), v[h]) + inter_w[h] * inter[h] for h in heads]
    den = [jnp.sum(scores[h], axis=-1, keepdims=True) + inter_w[h] * qn[h] for h in heads]
    hh = [num[h] / jnp.maximum(jnp.abs(den[h]), jnp.exp(-m_t[h])) for h in heads]

    w_log = [g_tot[h] - b_col[h] + i_col[h] for h in heads]
    m_new = [jnp.maximum(g_tot[h] + m_prev[h], jnp.max(w_log[h], axis=0, keepdims=True))
             for h in heads]
    decay = [jnp.exp(g_tot[h] + m_prev[h] - m_new[h]) for h in heads]
    kw = [k[h].astype(F32) * jnp.exp(w_log[h] - m_new[h]) for h in heads]
    for h in heads:
        c_ref[h] = decay[h] * c_prev[h] + lax.dot_general(
            kw[h].astype(BF16), v[h], (((0,), (0,)), ((), ())), preferred_element_type=F32)
        n_ref[h, 0:1, :] = decay[h] * n_prev[h] + jnp.sum(kw[h], axis=0, keepdims=True)
        m_ref[h] = jnp.broadcast_to(m_new[h], m_ref.shape[1:])

    mu = [jnp.mean(hh[h], axis=-1, keepdims=True) for h in heads]
    hc = [hh[h] - mu[h] for h in heads]
    var = [jnp.mean(hc[h] * hc[h], axis=-1, keepdims=True) for h in heads]
    for h in heads:
        hn = hc[h] * lax.rsqrt(var[h] + HEAD_NORM_EPS) * gain_ref[:, cols[h]]
        o = o_ref[:, cols[h]].astype(F32)
        y_ref[:, cols[h]] = (jax.nn.sigmoid(o) * hn).astype(y_ref.dtype)

    gate_terms(xbn_ref[...])


def _mlstm(qkvo, xb2d, w_gates_t_bf, b_gates_row, gain_row, *, batch, seq, nh, dh):
    m, d = xb2d.shape
    w = nh * dh
    chunk = MLSTM_CHUNK
    nc = seq // chunk

    def blk(kind):
        return pl.BlockSpec((chunk, w), lambda b, c: (b * nc + c, kind))

    kern = functools.partial(_mlstm_kernel, nh=nh, dh=dh)
    return pl.pallas_call(
        kern,
        out_shape=jax.ShapeDtypeStruct((m, w), BF16),
        grid=(batch, nc),
        in_specs=[blk(0), blk(1), blk(2), blk(3),
                  pl.BlockSpec((chunk, d), lambda b, c: (0, 0)),
                  pl.BlockSpec((chunk, d),
                               lambda b, c: (jnp.minimum(b * nc + c + 1, batch * nc - 1), 0)),
                  pl.BlockSpec((LANES, d), lambda b, c: (0, 0)),
                  pl.BlockSpec((1, LANES), lambda b, c: (0, 0)),
                  pl.BlockSpec((1, w), lambda b, c: (0, 0))],
        out_specs=pl.BlockSpec((chunk, w), lambda b, c: (b * nc + c, 0)),
        scratch_shapes=[
            pltpu.VMEM((nh, dh, dh), F32),
            pltpu.VMEM((nh, SUBLANES, dh), F32),
            pltpu.VMEM((nh, SUBLANES, LANES), F32),
            pltpu.VMEM((2, chunk, LANES), F32),
            pltpu.VMEM((2, LANES, chunk), F32),
        ],
        compiler_params=_params(("arbitrary", "arbitrary")),
        name="mlstm_chunkwise",
    )(qkvo, qkvo, qkvo, qkvo, xb2d, xb2d, w_gates_t_bf, b_gates_row, gain_row)


def _layer_norm(z, g, b):
    mu = jnp.mean(z, axis=-1, keepdims=True)
    zc = z - mu
    var = jnp.mean(zc * zc, axis=-1, keepdims=True)
    return zc * lax.rsqrt(var + LN_EPS) * g + b


def _outproj_ln_kernel(x_ref, yc_ref, ym_ref, wc_ref, wm_ref, g_ref, b_ref, o_ref, *,
                       alpha, chunk):
    for r0 in range(0, x_ref.shape[0], chunk):
        rs = slice(r0, r0 + chunk)
        mix = _dot(yc_ref[rs, :], wc_ref[...]) + _dot(ym_ref[rs, :], wm_ref[...])
        z = alpha * x_ref[rs, :] + mix
        o_ref[rs, :] = _layer_norm(z, g_ref[...], b_ref[...])


def _outproj_ln(x2d, y_conv, y_mlstm, w_out_bf, ln_g, ln_b, *, alpha):
    m, d = x2d.shape
    wc = y_conv.shape[1]
    wm = y_mlstm.shape[1]
    assert wc == wm
    tm = TM_OUT
    kern = functools.partial(_outproj_ln_kernel, alpha=alpha, chunk=TR_OUT)
    return pl.pallas_call(
        kern,
        out_shape=jax.ShapeDtypeStruct((m, d), F32),
        grid=(m // tm,),
        in_specs=[
            pl.BlockSpec((tm, d), lambda i: (i, 0)),
            pl.BlockSpec((tm, wc), lambda i: (i, 0)),
            pl.BlockSpec((tm, wm), lambda i: (i, 0)),
            pl.BlockSpec((wc, d), lambda i: (0, 0)),
            pl.BlockSpec((wm, d), lambda i: (1, 0)),
            pl.BlockSpec((1, d), lambda i: (0, 0)),
            pl.BlockSpec((1, d), lambda i: (0, 0)),
        ],
        out_specs=pl.BlockSpec((tm, d), lambda i: (i, 0)),
        compiler_params=_params(("arbitrary",)),
        name="outproj_ln",
    )(x2d, y_conv, y_mlstm, w_out_bf, w_out_bf, ln_g, ln_b)


def _convffn_ln_kernel(x_ref, wv_ref, wg_ref, cwv_ref, cwg_ref, cbv_ref, cbg_ref, wd_ref,
                       g_ref, b_ref, o_ref, xb_ref, acc_ref, carry_v_ref, carry_g_ref,
                       buf_v_ref, buf_g_ref, *, alpha, tiles_per_seq, chunk):
    i = pl.program_id(0)
    f = pl.program_id(1)
    nf = pl.num_programs(1)
    rows = x_ref.shape[0]
    first = (i % tiles_per_seq) == 0

    @pl.when(first)
    def _():
        carry_v_ref[f] = jnp.zeros(carry_v_ref.shape[1:], F32)
        carry_g_ref[f] = jnp.zeros(carry_g_ref.shape[1:], F32)

    def conv(buf_ref, cw_ref, cb_ref, r0):
        h = buf_ref[SUBLANES + r0:SUBLANES + r0 + chunk, :]
        hm1 = buf_ref[SUBLANES - 1 + r0:SUBLANES - 1 + r0 + chunk, :]
        hm2 = buf_ref[SUBLANES - 2 + r0:SUBLANES - 2 + r0 + chunk, :]
        w = cw_ref[...]
        return w[0:1, :] * hm2 + w[1:2, :] * hm1 + w[2:3, :] * h + cb_ref[...]

    def step(is_first, is_last):
        row_starts = list(range(0, rows, chunk))
        buf_v_ref[0:SUBLANES, :] = carry_v_ref[f]
        buf_g_ref[0:SUBLANES, :] = carry_g_ref[f]
        for r0 in row_starts:
            if is_first:
                xb_ref[r0:r0 + chunk, :] = x_ref[r0:r0 + chunk, :].astype(BF16)
            xb = xb_ref[r0:r0 + chunk, :]
            buf_v_ref[SUBLANES + r0:SUBLANES + r0 + chunk, :] = _dot(xb, wv_ref[...])
            buf_g_ref[SUBLANES + r0:SUBLANES + r0 + chunk, :] = _dot(xb, wg_ref[...])
        for r0 in row_starts:
            rs = slice(r0, r0 + chunk)
            val = conv(buf_v_ref, cwv_ref, cbv_ref, r0)
            gate = conv(buf_g_ref, cwg_ref, cbg_ref, r0)
            act = (jax.nn.silu(gate) * val).astype(BF16)
            part = _dot(act, wd_ref[...])
            if is_first:
                acc_ref[rs, :] = part
            elif is_last:
                z = alpha * x_ref[rs, :] + (acc_ref[rs, :] + part)
                o_ref[rs, :] = _layer_norm(z, g_ref[...], b_ref[...])
            else:
                acc_ref[rs, :] += part
        carry_v_ref[f] = buf_v_ref[rows:rows + SUBLANES, :]
        carry_g_ref[f] = buf_g_ref[rows:rows + SUBLANES, :]

    pl.when(f == 0)(functools.partial(step, True, False))
    pl.when(jnp.logical_and(f > 0, f < nf - 1))(functools.partial(step, False, False))
    pl.when(f == nf - 1)(functools.partial(step, False, True))


def _convffn_ln(x2d, w_up_bf, w_ffn_conv, b_ffn_conv_row, w_down_bf, ln_g, ln_b, *,
                alpha, seq, d_ff):
    m, d = x2d.shape
    tm, tf = TM_FFN, TF_FFN
    nf = d_ff // tf
    kern = functools.partial(_convffn_ln_kernel, alpha=alpha, tiles_per_seq=seq // tm,
                             chunk=TR_FFN)
    return pl.pallas_call(
        kern,
        out_shape=jax.ShapeDtypeStruct((m, d), F32),
        grid=(m // tm, nf),
        in_specs=[
            pl.BlockSpec((tm, d), lambda i, f: (i, 0)),
            pl.BlockSpec((None, d, tf), lambda i, f: (f, 0, 0)),
            pl.BlockSpec((None, d, tf), lambda i, f: (nf + f, 0, 0)),
            pl.BlockSpec((CONV_WIDTH, tf), lambda i, f: (0, f)),
            pl.BlockSpec((CONV_WIDTH, tf), lambda i, f: (0, nf + f)),
            pl.BlockSpec((1, tf), lambda i, f: (0, f)),
            pl.BlockSpec((1, tf), lambda i, f: (0, nf + f)),
            pl.BlockSpec((tf, d), lambda i, f: (f, 0)),
            pl.BlockSpec((1, d), lambda i, f: (0, 0)),
            pl.BlockSpec((1, d), lambda i, f: (0, 0)),
        ],
        out_specs=pl.BlockSpec((tm, d), lambda i, f: (i, 0)),
        scratch_shapes=[
            pltpu.VMEM((tm, d), BF16),
            pltpu.VMEM((tm, d), F32),
            pltpu.VMEM((nf, SUBLANES, tf), F32),
            pltpu.VMEM((nf, SUBLANES, tf), F32),
            pltpu.VMEM((tm + SUBLANES, tf), F32),
            pltpu.VMEM((tm + SUBLANES, tf), F32),
        ],
        compiler_params=_params(("arbitrary", "arbitrary")),
        name="convffn_ln",
    )(x2d, w_up_bf, w_up_bf, w_ffn_conv, w_ffn_conv, b_ffn_conv_row, b_ffn_conv_row,
      w_down_bf, ln_g, ln_b)


def kernel(x, w_in, b_gates, w_sc_conv, mh_gain, w_out, ln1_g, ln1_b,
           w_up, w_ffn_conv, b_ffn_conv, w_down, ln2_g, ln2_b):
    batch, seq, d = x.shape
    depth = w_in.shape[0]
    nh = N_HEADS_M
    w_conv = w_sc_conv.shape[-1]
    w_mlstm = mh_gain.shape[-1]
    dh = w_mlstm // nh
    d_ff = w_down.shape[1]
    n_gates = 2 * nh
    gate_col0 = 3 * w_conv + 4 * w_mlstm
    assert w_in.shape[-1] == gate_col0 + n_gates and w_conv + w_mlstm == d
    assert seq % TM_CONV == 0 and seq % TM_FFN == 0 and seq % MLSTM_CHUNK == 0
    assert w_conv % TC_CONV == 0 and d_ff % TF_FFN == 0
    assert (3 * w_conv) % TN_QKVO == 0 and (4 * w_mlstm) % TN_QKVO == 0
    assert (batch * seq) % TM_QKVO == 0 and (batch * seq) % TM_OUT == 0
    alpha = (2 * depth) ** 0.25

    x2d = x.reshape(batch * seq, d)
    for l in range(depth):
        w_in_t = w_in[l].T
        w_gates_t_bf = jnp.pad(w_in_t[gate_col0:], ((0, LANES - n_gates), (0, 0))).astype(BF16)
        b_gates_row = jnp.pad(b_gates[l], (0, LANES - n_gates)).reshape(1, LANES)

        y_conv, xb2d, w_down_bf, w_out_bf = _inproj_conv(
            x2d, w_in_t, w_sc_conv[l], w_down[l], w_out[l], seq=seq, w_conv=w_conv)
        qkvo, w_up_bf = _inproj_qkvo(xb2d, w_in_t, w_up[l], col0=3 * w_conv, width=4 * w_mlstm,
                                     tf=TF_FFN)
        y_mlstm = _mlstm(qkvo, xb2d, w_gates_t_bf, b_gates_row, mh_gain[l].reshape(1, w_mlstm),
                         batch=batch, seq=seq, nh=nh, dh=dh)
        x2d = _outproj_ln(x2d, y_conv, y_mlstm, w_out_bf,
                          ln1_g[l].reshape(1, d), ln1_b[l].reshape(1, d), alpha=alpha)
        x2d = _convffn_ln(x2d, w_up_bf, w_ffn_conv[l],
                          b_ffn_conv[l].reshape(1, 2 * d_ff), w_down_bf,
                          ln2_g[l].reshape(1, d), ln2_b[l].reshape(1, d),
                          alpha=alpha, seq=seq, d_ff=d_ff)
    return x2d.reshape(batch, seq, d)
```

```python
import functools

import jax
import jax.numpy as jnp
from jax import lax
from jax.experimental import pallas as pl
from jax.experimental.pallas import tpu as pltpu

F32 = jnp.float32
BF16 = jnp.bfloat16

N_HEADS_M = 4
CONV_WIDTH = 3
LN_EPS = 1e-5
HEAD_NORM_EPS = 1e-6

SUBLANES = 8
LANES = 128
BF16_SUBLANES = 16
VMEM_LIMIT_BYTES = 56 * 1024 * 1024

TM_CONV = 512
TC_CONV = 512
TR_CONV = 256
TM_QKVO = 1024
TN_QKVO = 1024
MLSTM_CHUNK = 256
OUT_PIECES = 4
TM_FFN = 512
TF_FFN = 512
TR_FFN = 256


def _dot(a, b):
    return jnp.dot(a, b, preferred_element_type=F32)


def _dot_t(a, b_t):
    return lax.dot_general(a, b_t, (((1,), (1,)), ((), ())), preferred_element_type=F32)


def _params(semantics, flags=None):
    return pltpu.CompilerParams(dimension_semantics=semantics,
                                vmem_limit_bytes=VMEM_LIMIT_BYTES, flags=flags)


def _cast_rows_spec(rows, cols, n_steps, step_of):
    assert rows % n_steps == 0 and (rows // n_steps) % BF16_SUBLANES == 0
    return pl.BlockSpec((rows // n_steps, cols), lambda *ids: (step_of(*ids), 0))


def _inproj_conv_kernel(x_ref, wb_ref, wc_ref, wh_ref, cw_ref, wdn_ref, wout_ref,
                        y_ref, xb_ref, wdn_bf_ref, wout_bf_ref,
                        wbf_ref, carry_ref, buf_ref, *, tiles_per_seq, chunk):
    i = pl.program_id(1)
    rows = x_ref.shape[0]

    @pl.when(i == 0)
    def _():
        wbf_ref[0] = wb_ref[...].astype(BF16)
        wbf_ref[1] = wc_ref[...].astype(BF16)
        wbf_ref[2] = wh_ref[...].astype(BF16)

    @pl.when(pl.program_id(0) == 0)
    def _():
        xb_ref[...] = x_ref[...].astype(BF16)

    @pl.when(i % tiles_per_seq == 0)
    def _():
        carry_ref[...] = jnp.zeros(carry_ref.shape, F32)

    wdn_bf_ref[...] = wdn_ref[...].astype(BF16)
    wout_bf_ref[...] = wout_ref[...].astype(BF16)

    buf_ref[0:SUBLANES, :] = carry_ref[...]
    w = cw_ref[...]
    row_starts = list(range(0, rows, chunk))
    xbs = [x_ref[r0:r0 + chunk, :].astype(BF16) for r0 in row_starts]
    for r0, xb in zip(row_starts, xbs):
        u = _dot_t(xb, wbf_ref[1]) * _dot_t(xb, wbf_ref[2])
        buf_ref[SUBLANES + r0:SUBLANES + r0 + chunk, :] = u
    c_bs = [_dot_t(xb, wbf_ref[0]) for xb in xbs]
    for r0, c_b in zip(row_starts, c_bs):
        u = buf_ref[SUBLANES + r0:SUBLANES + r0 + chunk, :]
        um2 = buf_ref[SUBLANES - 2 + r0:SUBLANES - 2 + r0 + chunk, :]
        um1 = buf_ref[SUBLANES - 1 + r0:SUBLANES - 1 + r0 + chunk, :]
        y = w[0:1, :] * um2 + w[1:2, :] * um1 + w[2:3, :] * u
        y_ref[r0:r0 + chunk, :] = (c_b * y).astype(y_ref.dtype)
    carry_ref[...] = buf_ref[rows:rows + SUBLANES, :]


def _inproj_conv(x2d, w_in_t, w_sc_conv, w_down, w_out, *, seq, w_conv):
    m, d = x2d.shape
    tm, tc = TM_CONV, TC_CONV
    nj, ni = w_conv // tc, m // tm
    n_steps = nj * ni
    step_of = lambda j, i: j * ni + i
    once = pl.Buffered(1)
    kern = functools.partial(_inproj_conv_kernel, tiles_per_seq=seq // tm, chunk=TR_CONV)
    return pl.pallas_call(
        kern,
        out_shape=(jax.ShapeDtypeStruct((m, w_conv), BF16),
                   jax.ShapeDtypeStruct((m, d), BF16),
                   jax.ShapeDtypeStruct(w_down.shape, BF16),
                   jax.ShapeDtypeStruct(w_out.shape, BF16)),
        grid=(nj, ni),
        in_specs=[
            pl.BlockSpec((tm, d), lambda j, i: (i, 0)),
            pl.BlockSpec((tc, d), lambda j, i: (j, 0), pipeline_mode=once),
            pl.BlockSpec((tc, d), lambda j, i: (nj + j, 0), pipeline_mode=once),
            pl.BlockSpec((tc, d), lambda j, i: (2 * nj + j, 0), pipeline_mode=once),
            pl.BlockSpec((CONV_WIDTH, tc), lambda j, i: (0, j)),
            _cast_rows_spec(*w_down.shape, n_steps, step_of),
            _cast_rows_spec(*w_out.shape, n_steps, step_of),
        ],
        out_specs=(pl.BlockSpec((tm, tc), lambda j, i: (i, j)),
                   pl.BlockSpec((tm, d), lambda j, i: (jnp.where(j == 0, i, ni - 1), 0)),
                   _cast_rows_spec(*w_down.shape, n_steps, step_of),
                   _cast_rows_spec(*w_out.shape, n_steps, step_of)),
        scratch_shapes=[
            pltpu.VMEM((3, tc, d), BF16),
            pltpu.VMEM((SUBLANES, tc), F32),
            pltpu.VMEM((tm + SUBLANES, tc), F32),
        ],
        compiler_params=_params(("arbitrary", "arbitrary")),
        name="inproj_shortconv",
    )(x2d, w_in_t, w_in_t, w_in_t, w_sc_conv, w_down, w_out)


def _inproj_qkvo_kernel(xb_ref, w_ref, wup_ref, o_ref, wup_bf_ref, wbf_ref):
    @pl.when(pl.program_id(1) == 0)
    def _():
        wbf_ref[...] = w_ref[...].astype(BF16)

    n_blocks, _, tf = wup_bf_ref.shape
    for n in range(n_blocks):
        wup_bf_ref[n] = wup_ref[:, n * tf:(n + 1) * tf].astype(BF16)
    o_ref[...] = _dot_t(xb_ref[...], wbf_ref[...]).astype(o_ref.dtype)


def _inproj_qkvo(xb2d, w_in_t, w_up, *, col0, width, tf):
    m, d = xb2d.shape
    tm, tn = TM_QKVO, TN_QKVO
    j0 = col0 // tn
    nj, ni = width // tn, m // tm
    n_steps = nj * ni
    step_of = lambda j, i: j * ni + i
    up_rows, up_cols = w_up.shape
    n_blocks = up_cols // tf
    rows_per_step = up_rows // n_steps
    assert up_cols % tf == 0 and up_rows % n_steps == 0 and rows_per_step % BF16_SUBLANES == 0
    return pl.pallas_call(
        _inproj_qkvo_kernel,
        out_shape=(jax.ShapeDtypeStruct((m, width), BF16),
                   jax.ShapeDtypeStruct((n_blocks, up_rows, tf), BF16)),
        grid=(nj, ni),
        in_specs=[
            pl.BlockSpec((tm, d), lambda j, i: (i, 0)),
            pl.BlockSpec((tn, d), lambda j, i: (j0 + j, 0)),
            _cast_rows_spec(up_rows, up_cols, n_steps, step_of),
        ],
        out_specs=(pl.BlockSpec((tm, tn), lambda j, i: (i, j)),
                   pl.BlockSpec((n_blocks, rows_per_step, tf),
                                lambda j, i: (0, step_of(j, i), 0))),
        scratch_shapes=[pltpu.VMEM((tn, d), BF16)],
        compiler_params=_params(("arbitrary", "arbitrary")),
        name="inproj_mlstm",
    )(xb2d, w_in_t, w_up)


def _log_sigmoid(x):
    return jnp.minimum(x, 0.0) - jnp.log1p(jnp.exp(-jnp.abs(x)))


def _layer_norm(z, g, b):
    mu = jnp.mean(z, axis=-1, keepdims=True)
    zc = z - mu
    var = jnp.mean(zc * zc, axis=-1, keepdims=True)
    return zc * lax.rsqrt(var + LN_EPS) * g + b


def _mlstm_outproj_kernel(q_ref, k_ref, v_ref, o_ref, xb0_ref, xbn_ref, wg_ref, bg_ref, gain_ref,
                          x_ref, yc_ref, wc_ref, wm_ref, lng_ref, lnb_ref, x1_ref,
                          c_ref, n_ref, m_ref, gates_ref, gates_t_ref, ym_ref, *,
                          nh, dh, chunks_per_seq, alpha, n_pieces):
    chunk = q_ref.shape[0]
    step = pl.program_id(0)

    @pl.when(step % chunks_per_seq == 0)
    def _():
        c_ref[...] = jnp.zeros(c_ref.shape, F32)
        n_ref[...] = jnp.zeros(n_ref.shape, F32)
        m_ref[...] = jnp.zeros(m_ref.shape, F32)

    row = lax.broadcasted_iota(jnp.int32, (chunk, chunk), 0)
    col = lax.broadcasted_iota(jnp.int32, (chunk, chunk), 1)
    causal = row >= col

    def gate_terms(xb):
        g = _dot_t(xb, wg_ref[...]) + bg_ref[...]
        lf = _log_sigmoid(g)
        bcum = jnp.dot(causal.astype(F32), lf, precision=lax.Precision.HIGHEST,
                       preferred_element_type=F32)
        gates_ref[0] = g
        gates_ref[1] = bcum
        gates_t_ref[0] = g.T
        gates_t_ref[1] = bcum.T

    @pl.when(step == 0)
    def _():
        gate_terms(xb0_ref[...])
        ym_ref[...] = jnp.zeros(ym_ref.shape, ym_ref.dtype)

    g = gates_ref[0]
    bcum = gates_ref[1]
    g_t = gates_t_ref[0]
    bcum_t = gates_t_ref[1]
    neg_inf = jnp.float32(-jnp.inf)
    heads = range(nh)
    cols = [slice(h * dh, (h + 1) * dh) for h in heads]

    yc_prev = yc_ref[...]
    ym_prev = ym_ref[...]
    piece = wc_ref.shape[1] // n_pieces

    def mix_piece(p):
        ps = slice(p * piece, (p + 1) * piece)
        return _dot(yc_prev, wc_ref[:, ps]) + _dot(ym_prev, wm_ref[:, ps])

    q = [q_ref[:, cols[h]] * jnp.asarray(dh ** -0.5, BF16) for h in heads]
    k = [k_ref[:, cols[h]] for h in heads]
    v = [v_ref[:, cols[h]] for h in heads]
    i_col = [g[:, h:h + 1] for h in heads]
    b_col = [bcum[:, nh + h:nh + h + 1] for h in heads]
    i_row = [g_t[h:h + 1, :] for h in heads]
    b_row = [bcum_t[nh + h:nh + h + 1, :] for h in heads]
    g_tot = [b_col[h][chunk - 1:chunk, :] for h in heads]
    m_prev = [m_ref[h, 0:1, 0:1] for h in heads]
    c_prev = [c_ref[h] for h in heads]
    n_prev = [n_ref[h, 0:1, :] for h in heads]

    mix = [mix_piece(p) for p in range(n_pieces)]
    qk = [_dot_t(q[h], k[h]) for h in heads]
    inter = [_dot(q[h], c_prev[h].astype(BF16)) for h in heads]
    qn = [jnp.sum(q[h].astype(F32) * n_prev[h], axis=-1, keepdims=True) for h in heads]

    log_d = [jnp.where(causal, b_col[h] - (b_row[h] - i_row[h]), neg_inf) for h in heads]
    inter_log = [b_col[h] + m_prev[h] for h in heads]
    m_t = [jnp.maximum(inter_log[h], jnp.max(log_d[h], axis=-1, keepdims=True)) for h in heads]
    scores = [qk[h] * jnp.exp(log_d[h] - m_t[h]) for h in heads]
    inter_w = [jnp.exp(inter_log[h] - m_t[h]) for h in heads]
    num = [_dot(scores[h].astype(BF16), v[h]) + inter_w[h] * inter[h] for h in heads]
    den = [jnp.sum(scores[h], axis=-1, keepdims=True) + inter_w[h] * qn[h] for h in heads]
    hh = [num[h] / jnp.maximum(jnp.abs(den[h]), jnp.exp(-m_t[h])) for h in heads]

    w_log = [g_tot[h] - b_col[h] + i_col[h] for h in heads]
    m_new = [jnp.maximum(g_tot[h] + m_prev[h], jnp.max(w_log[h], axis=0, keepdims=True))
             for h in heads]
    decay = [jnp.exp(g_tot[h] + m_prev[h] - m_new[h]) for h in heads]
    kw = [k[h].astype(F32) * jnp.exp(w_log[h] - m_new[h]) for h in heads]
    for h in heads:
        c_ref[h] = decay[h] * c_prev[h] + lax.dot_general(
            kw[h].astype(BF16), v[h], (((0,), (0,)), ((), ())), preferred_element_type=F32)
        n_ref[h, 0:1, :] = decay[h] * n_prev[h] + jnp.sum(kw[h], axis=0, keepdims=True)
        m_ref[h] = jnp.broadcast_to(m_new[h], m_ref.shape[1:])

    z = alpha * x_ref[...] + jnp.concatenate(mix, axis=1)
    x1_ref[...] = _layer_norm(z, lng_ref[...], lnb_ref[...])

    mu = [jnp.mean(hh[h], axis=-1, keepdims=True) for h in heads]
    hc = [hh[h] - mu[h] for h in heads]
    var = [jnp.mean(hc[h] * hc[h], axis=-1, keepdims=True) for h in heads]
    for h in heads:
        hn = hc[h] * lax.rsqrt(var[h] + HEAD_NORM_EPS) * gain_ref[:, cols[h]]
        o = o_ref[:, cols[h]].astype(F32)
        ym_ref[:, cols[h]] = (jax.nn.sigmoid(o) * hn).astype(ym_ref.dtype)

    gate_terms(xbn_ref[...])


def _mlstm_outproj(qkvo, xb2d, w_gates_t_bf, b_gates_row, gain_row, x2d, y_conv, w_out_bf,
                   ln_g, ln_b, *, seq, nh, dh, alpha):
    m, d = x2d.shape
    w = nh * dh
    wc = y_conv.shape[1]
    assert wc + w == w_out_bf.shape[0] and wc == w
    chunk = MLSTM_CHUNK
    n_chunks = m // chunk
    last = n_chunks - 1

    def cur(s):
        return jnp.minimum(s, last)

    def prev(s):
        return jnp.maximum(s - 1, 0)

    def blk(kind):
        return pl.BlockSpec((chunk, w), lambda s: (cur(s), kind))

    const = pl.Buffered(1)
    kern = functools.partial(_mlstm_outproj_kernel, nh=nh, dh=dh, chunks_per_seq=seq // chunk,
                             alpha=alpha, n_pieces=OUT_PIECES)
    return pl.pallas_call(
        kern,
        out_shape=jax.ShapeDtypeStruct((m, d), F32),
        grid=(n_chunks + 1,),
        in_specs=[blk(0), blk(1), blk(2), blk(3),
                  pl.BlockSpec((chunk, d), lambda s: (0, 0)),
                  pl.BlockSpec((chunk, d), lambda s: (jnp.minimum(s + 1, last), 0)),
                  pl.BlockSpec((LANES, d), lambda s: (0, 0)),
                  pl.BlockSpec((1, LANES), lambda s: (0, 0)),
                  pl.BlockSpec((1, w), lambda s: (0, 0)),
                  pl.BlockSpec((chunk, d), lambda s: (prev(s), 0)),
                  pl.BlockSpec((chunk, wc), lambda s: (prev(s), 0)),
                  pl.BlockSpec((wc, d), lambda s: (0, 0), pipeline_mode=const),
                  pl.BlockSpec((w, d), lambda s: (1, 0), pipeline_mode=const),
                  pl.BlockSpec((1, d), lambda s: (0, 0)),
                  pl.BlockSpec((1, d), lambda s: (0, 0))],
        out_specs=pl.BlockSpec((chunk, d), lambda s: (prev(s), 0)),
        scratch_shapes=[
            pltpu.VMEM((nh, dh, dh), F32),
            pltpu.VMEM((nh, SUBLANES, dh), F32),
            pltpu.VMEM((nh, SUBLANES, LANES), F32),
            pltpu.VMEM((2, chunk, LANES), F32),
            pltpu.VMEM((2, LANES, chunk), F32),
            pltpu.VMEM((chunk, w), BF16),
        ],
        compiler_params=_params(("arbitrary",)),
        name="mlstm_outproj_ln",
    )(qkvo, qkvo, qkvo, qkvo, xb2d, xb2d, w_gates_t_bf, b_gates_row, gain_row,
      x2d, y_conv, w_out_bf, w_out_bf, ln_g, ln_b)


def _convffn_ln_kernel(x_ref, wv_ref, wg_ref, cwv_ref, cwg_ref, cbv_ref, cbg_ref, wd_ref,
                       g_ref, b_ref, o_ref, xb_ref, acc_ref, carry_v_ref, carry_g_ref,
                       buf_v_ref, buf_g_ref, *, alpha, tiles_per_seq, chunk):
    i = pl.program_id(0)
    f = pl.program_id(1)
    nf = pl.num_programs(1)
    rows = x_ref.shape[0]
    first = (i % tiles_per_seq) == 0

    @pl.when(first)
    def _():
        carry_v_ref[f] = jnp.zeros(carry_v_ref.shape[1:], F32)
        carry_g_ref[f] = jnp.zeros(carry_g_ref.shape[1:], F32)

    def conv(buf_ref, cw_ref, cb_ref, r0):
        h = buf_ref[SUBLANES + r0:SUBLANES + r0 + chunk, :]
        hm1 = buf_ref[SUBLANES - 1 + r0:SUBLANES - 1 + r0 + chunk, :]
        hm2 = buf_ref[SUBLANES - 2 + r0:SUBLANES - 2 + r0 + chunk, :]
        w = cw_ref[...]
        return w[0:1, :] * hm2 + w[1:2, :] * hm1 + w[2:3, :] * h + cb_ref[...]

    def step(is_first, is_last):
        row_starts = list(range(0, rows, chunk))
        buf_v_ref[0:SUBLANES, :] = carry_v_ref[f]
        buf_g_ref[0:SUBLANES, :] = carry_g_ref[f]
        for r0 in row_starts:
            if is_first:
                xb_ref[r0:r0 + chunk, :] = x_ref[r0:r0 + chunk, :].astype(BF16)
            xb = xb_ref[r0:r0 + chunk, :]
            buf_v_ref[SUBLANES + r0:SUBLANES + r0 + chunk, :] = _dot(xb, wv_ref[...])
            buf_g_ref[SUBLANES + r0:SUBLANES + r0 + chunk, :] = _dot(xb, wg_ref[...])
        for r0 in row_starts:
            rs = slice(r0, r0 + chunk)
            val = conv(buf_v_ref, cwv_ref, cbv_ref, r0)
            gate = conv(buf_g_ref, cwg_ref, cbg_ref, r0)
            act = (jax.nn.silu(gate) * val).astype(BF16)
            part = _dot(act, wd_ref[...])
            if is_first:
                acc_ref[rs, :] = part
            elif is_last:
                z = alpha * x_ref[rs, :] + (acc_ref[rs, :] + part)
                o_ref[rs, :] = _layer_norm(z, g_ref[...], b_ref[...])
            else:
                acc_ref[rs, :] += part
        carry_v_ref[f] = buf_v_ref[rows:rows + SUBLANES, :]
        carry_g_ref[f] = buf_g_ref[rows:rows + SUBLANES, :]

    pl.when(f == 0)(functools.partial(step, True, False))
    pl.when(jnp.logical_and(f > 0, f < nf - 1))(functools.partial(step, False, False))
    pl.when(f == nf - 1)(functools.partial(step, False, True))


def _convffn_ln(x2d, w_up_bf, w_ffn_conv, b_ffn_conv_row, w_down_bf, ln_g, ln_b, *,
                alpha, seq, d_ff):
    m, d = x2d.shape
    tm, tf = TM_FFN, TF_FFN
    nf = d_ff // tf
    kern = functools.partial(_convffn_ln_kernel, alpha=alpha, tiles_per_seq=seq // tm,
                             chunk=TR_FFN)
    return pl.pallas_call(
        kern,
        out_shape=jax.ShapeDtypeStruct((m, d), F32),
        grid=(m // tm, nf),
        in_specs=[
            pl.BlockSpec((tm, d), lambda i, f: (i, 0)),
            pl.BlockSpec((None, d, tf), lambda i, f: (f, 0, 0)),
            pl.BlockSpec((None, d, tf), lambda i, f: (nf + f, 0, 0)),
            pl.BlockSpec((CONV_WIDTH, tf), lambda i, f: (0, f)),
            pl.BlockSpec((CONV_WIDTH, tf), lambda i, f: (0, nf + f)),
            pl.BlockSpec((1, tf), lambda i, f: (0, f)),
            pl.BlockSpec((1, tf), lambda i, f: (0, nf + f)),
            pl.BlockSpec((tf, d), lambda i, f: (f, 0)),
            pl.BlockSpec((1, d), lambda i, f: (0, 0)),
            pl.BlockSpec((1, d), lambda i, f: (0, 0)),
        ],
        out_specs=pl.BlockSpec((tm, d), lambda i, f: (i, 0)),
        scratch_shapes=[
            pltpu.VMEM((tm, d), BF16),
            pltpu.VMEM((tm, d), F32),
            pltpu.VMEM((nf, SUBLANES, tf), F32),
            pltpu.VMEM((nf, SUBLANES, tf), F32),
            pltpu.VMEM((tm + SUBLANES, tf), F32),
            pltpu.VMEM((tm + SUBLANES, tf), F32),
        ],
        compiler_params=_params(("arbitrary", "arbitrary")),
        name="convffn_ln",
    )(x2d, w_up_bf, w_up_bf, w_ffn_conv, w_ffn_conv, b_ffn_conv_row, b_ffn_conv_row,
      w_down_bf, ln_g, ln_b)


def kernel(x, w_in, b_gates, w_sc_conv, mh_gain, w_out, ln1_g, ln1_b,
           w_up, w_ffn_conv, b_ffn_conv, w_down, ln2_g, ln2_b):
    batch, seq, d = x.shape
    depth = w_in.shape[0]
    nh = N_HEADS_M
    w_conv = w_sc_conv.shape[-1]
    w_mlstm = mh_gain.shape[-1]
    dh = w_mlstm // nh
    d_ff = w_down.shape[1]
    n_gates = 2 * nh
    gate_col0 = 3 * w_conv + 4 * w_mlstm
    assert w_in.shape[-1] == gate_col0 + n_gates and w_conv + w_mlstm == d
    assert seq % TM_CONV == 0 and seq % TM_FFN == 0 and seq % MLSTM_CHUNK == 0
    assert w_conv % TC_CONV == 0 and d_ff % TF_FFN == 0
    assert (3 * w_conv) % TN_QKVO == 0 and (4 * w_mlstm) % TN_QKVO == 0
    assert (batch * seq) % TM_QKVO == 0
    alpha = (2 * depth) ** 0.25

    x2d = x.reshape(batch * seq, d)
    for l in range(depth):
        w_in_t = w_in[l].T
        w_gates_t_bf = jnp.pad(w_in_t[gate_col0:], ((0, LANES - n_gates), (0, 0))).astype(BF16)
        b_gates_row = jnp.pad(b_gates[l], (0, LANES - n_gates)).reshape(1, LANES)

        y_conv, xb2d, w_down_bf, w_out_bf = _inproj_conv(
            x2d, w_in_t, w_sc_conv[l], w_down[l], w_out[l], seq=seq, w_conv=w_conv)
        qkvo, w_up_bf = _inproj_qkvo(xb2d, w_in_t, w_up[l], col0=3 * w_conv, width=4 * w_mlstm,
                                     tf=TF_FFN)
        x2d = _mlstm_outproj(qkvo, xb2d, w_gates_t_bf, b_gates_row, mh_gain[l].reshape(1, w_mlstm),
                             x2d, y_conv, w_out_bf, ln1_g[l].reshape(1, d), ln1_b[l].reshape(1, d),
                             seq=seq, nh=nh, dh=dh, alpha=alpha)
        x2d = _convffn_ln(x2d, w_up_bf, w_ffn_conv[l],
                          b_ffn_conv[l].reshape(1, 2 * d_ff), w_down_bf,
                          ln2_g[l].reshape(1, d), ln2_b[l].reshape(1, d),
                          alpha=alpha, seq=seq, d_ff=d_ff)
    return x2d.reshape(batch, seq, d)
```

```python
import functools

import jax
import jax.numpy as jnp
from jax import lax
from jax.experimental import pallas as pl
from jax.experimental.pallas import tpu as pltpu

F32 = jnp.float32
BF16 = jnp.bfloat16

N_HEADS_M = 4
CONV_WIDTH = 3
LN_EPS = 1e-5
HEAD_NORM_EPS = 1e-6

SUBLANES = 8
LANES = 128
BF16_SUBLANES = 16
VMEM_LIMIT_BYTES = 56 * 1024 * 1024

TM_CONV = 512
TC_CONV = 512
TR_CONV = 256
TM_QKVO = 1024
TN_QKVO = 1024
MLSTM_CHUNK = 256
HEAD_GROUP = 4
TM_FFN = 512
TF_FFN = 512
TR_FFN = 256


def _dot(a, b):
    return jnp.dot(a, b, preferred_element_type=F32)


def _dot_t(a, b_t):
    return lax.dot_general(a, b_t, (((1,), (1,)), ((), ())), preferred_element_type=F32)


def _params(semantics, flags=None):
    return pltpu.CompilerParams(dimension_semantics=semantics,
                                vmem_limit_bytes=VMEM_LIMIT_BYTES, flags=flags)


def _cast_rows_spec(rows, cols, n_steps, step_of):
    assert rows % n_steps == 0 and (rows // n_steps) % BF16_SUBLANES == 0
    return pl.BlockSpec((rows // n_steps, cols), lambda *ids: (step_of(*ids), 0))


def _inproj_conv_kernel(x_ref, wb_ref, wc_ref, wh_ref, cw_ref, wdn_ref, wout_ref,
                        y_ref, xb_ref, wdn_bf_ref, wout_bf_ref,
                        wbf_ref, carry_ref, buf_ref, *, tiles_per_seq, chunk):
    i = pl.program_id(1)
    rows = x_ref.shape[0]

    @pl.when(i == 0)
    def _():
        wbf_ref[0] = wb_ref[...].astype(BF16)
        wbf_ref[1] = wc_ref[...].astype(BF16)
        wbf_ref[2] = wh_ref[...].astype(BF16)

    @pl.when(pl.program_id(0) == 0)
    def _():
        xb_ref[...] = x_ref[...].astype(BF16)

    @pl.when(i % tiles_per_seq == 0)
    def _():
        carry_ref[...] = jnp.zeros(carry_ref.shape, F32)

    wdn_bf_ref[...] = wdn_ref[...].astype(BF16)
    wout_bf_ref[...] = wout_ref[...].astype(BF16)

    buf_ref[0:SUBLANES, :] = carry_ref[...]
    w = cw_ref[...]
    row_starts = list(range(0, rows, chunk))
    xbs = [x_ref[r0:r0 + chunk, :].astype(BF16) for r0 in row_starts]
    for r0, xb in zip(row_starts, xbs):
        u = _dot_t(xb, wbf_ref[1]) * _dot_t(xb, wbf_ref[2])
        buf_ref[SUBLANES + r0:SUBLANES + r0 + chunk, :] = u
    c_bs = [_dot_t(xb, wbf_ref[0]) for xb in xbs]
    for r0, c_b in zip(row_starts, c_bs):
        u = buf_ref[SUBLANES + r0:SUBLANES + r0 + chunk, :]
        um2 = buf_ref[SUBLANES - 2 + r0:SUBLANES - 2 + r0 + chunk, :]
        um1 = buf_ref[SUBLANES - 1 + r0:SUBLANES - 1 + r0 + chunk, :]
        y = w[0:1, :] * um2 + w[1:2, :] * um1 + w[2:3, :] * u
        y_ref[r0:r0 + chunk, :] = (c_b * y).astype(y_ref.dtype)
    carry_ref[...] = buf_ref[rows:rows + SUBLANES, :]


def _inproj_conv(x2d, w_in_t, w_sc_conv, w_down, w_out, *, seq, w_conv):
    m, d = x2d.shape
    tm, tc = TM_CONV, TC_CONV
    nj, ni = w_conv // tc, m // tm
    n_steps = nj * ni
    step_of = lambda j, i: j * ni + i
    once = pl.Buffered(1)
    kern = functools.partial(_inproj_conv_kernel, tiles_per_seq=seq // tm, chunk=TR_CONV)
    return pl.pallas_call(
        kern,
        out_shape=(jax.ShapeDtypeStruct((m, w_conv), BF16),
                   jax.ShapeDtypeStruct((m, d), BF16),
                   jax.ShapeDtypeStruct(w_down.shape, BF16),
                   jax.ShapeDtypeStruct(w_out.shape, BF16)),
        grid=(nj, ni),
        in_specs=[
            pl.BlockSpec((tm, d), lambda j, i: (i, 0)),
            pl.BlockSpec((tc, d), lambda j, i: (j, 0), pipeline_mode=once),
            pl.BlockSpec((tc, d), lambda j, i: (nj + j, 0), pipeline_mode=once),
            pl.BlockSpec((tc, d), lambda j, i: (2 * nj + j, 0), pipeline_mode=once),
            pl.BlockSpec((CONV_WIDTH, tc), lambda j, i: (0, j)),
            _cast_rows_spec(*w_down.shape, n_steps, step_of),
            _cast_rows_spec(*w_out.shape, n_steps, step_of),
        ],
        out_specs=(pl.BlockSpec((tm, tc), lambda j, i: (i, j)),
                   pl.BlockSpec((tm, d), lambda j, i: (jnp.where(j == 0, i, ni - 1), 0)),
                   _cast_rows_spec(*w_down.shape, n_steps, step_of),
                   _cast_rows_spec(*w_out.shape, n_steps, step_of)),
        scratch_shapes=[
            pltpu.VMEM((3, tc, d), BF16),
            pltpu.VMEM((SUBLANES, tc), F32),
            pltpu.VMEM((tm + SUBLANES, tc), F32),
        ],
        compiler_params=_params(("arbitrary", "arbitrary")),
        name="inproj_shortconv",
    )(x2d, w_in_t, w_in_t, w_in_t, w_sc_conv, w_down, w_out)


def _inproj_qkvo_kernel(xb_ref, w_ref, wup_ref, o_ref, wup_bf_ref, wbf_ref):
    @pl.when(pl.program_id(1) == 0)
    def _():
        wbf_ref[...] = w_ref[...].astype(BF16)

    n_blocks, _, tf = wup_bf_ref.shape
    for n in range(n_blocks):
        wup_bf_ref[n] = wup_ref[:, n * tf:(n + 1) * tf].astype(BF16)
    o_ref[...] = _dot_t(xb_ref[...], wbf_ref[...]).astype(o_ref.dtype)


def _inproj_qkvo(xb2d, w_in_t, w_up, *, col0, width, tf):
    m, d = xb2d.shape
    tm, tn = TM_QKVO, TN_QKVO
    j0 = col0 // tn
    nj, ni = width // tn, m // tm
    n_steps = nj * ni
    step_of = lambda j, i: j * ni + i
    up_rows, up_cols = w_up.shape
    n_blocks = up_cols // tf
    rows_per_step = up_rows // n_steps
    assert up_cols % tf == 0 and up_rows % n_steps == 0 and rows_per_step % BF16_SUBLANES == 0
    return pl.pallas_call(
        _inproj_qkvo_kernel,
        out_shape=(jax.ShapeDtypeStruct((m, width), BF16),
                   jax.ShapeDtypeStruct((n_blocks, up_rows, tf), BF16)),
        grid=(nj, ni),
        in_specs=[
            pl.BlockSpec((tm, d), lambda j, i: (i, 0)),
            pl.BlockSpec((tn, d), lambda j, i: (j0 + j, 0)),
            _cast_rows_spec(up_rows, up_cols, n_steps, step_of),
        ],
        out_specs=(pl.BlockSpec((tm, tn), lambda j, i: (i, j)),
                   pl.BlockSpec((n_blocks, rows_per_step, tf),
                                lambda j, i: (0, step_of(j, i), 0))),
        scratch_shapes=[pltpu.VMEM((tn, d), BF16)],
        compiler_params=_params(("arbitrary", "arbitrary")),
        name="inproj_mlstm",
    )(xb2d, w_in_t, w_up)


def _log_sigmoid(x):
    return jnp.minimum(x, 0.0) - jnp.log1p(jnp.exp(-jnp.abs(x)))


def _layer_norm(z, g, b):
    mu = jnp.mean(z, axis=-1, keepdims=True)
    zc = z - mu
    var = jnp.mean(zc * zc, axis=-1, keepdims=True)
    return zc * lax.rsqrt(var + LN_EPS) * g + b


def _mlstm_outproj_kernel(q_ref, k_ref, v_ref, o_ref, xb0_ref, xbn_ref, wg_ref, bg_ref, gain_ref,
                          x_ref, yc_ref, wc0_ref, wc1_ref, wc2_ref, wc3_ref,
                          wm0_ref, wm1_ref, wm2_ref, wm3_ref, lng_ref, lnb_ref, x1_ref,
                          c_ref, n_ref, m_ref, gates_ref, gates_t_ref, ym_ref, *,
                          nh, dh, chunks_per_seq, alpha):
    chunk = q_ref.shape[0]
    step = pl.program_id(0)

    @pl.when(step % chunks_per_seq == 0)
    def _():
        c_ref[...] = jnp.zeros(c_ref.shape, F32)
        n_ref[...] = jnp.zeros(n_ref.shape, F32)
        m_ref[...] = jnp.zeros(m_ref.shape, F32)

    row = lax.broadcasted_iota(jnp.int32, (chunk, chunk), 0)
    col = lax.broadcasted_iota(jnp.int32, (chunk, chunk), 1)
    causal = row >= col

    def gate_terms(xb):
        g = _dot_t(xb, wg_ref[...]) + bg_ref[...]
        lf = _log_sigmoid(g)
        bcum = jnp.dot(causal.astype(F32), lf, precision=lax.Precision.HIGHEST,
                       preferred_element_type=F32)
        gates_ref[0] = g
        gates_ref[1] = bcum
        gates_t_ref[0] = g.T
        gates_t_ref[1] = bcum.T

    @pl.when(step == 0)
    def _():
        gate_terms(xb0_ref[...])
        ym_ref[...] = jnp.zeros(ym_ref.shape, ym_ref.dtype)

    g = gates_ref[0]
    bcum = gates_ref[1]
    g_t = gates_t_ref[0]
    bcum_t = gates_t_ref[1]
    neg_inf = jnp.float32(-jnp.inf)
    cols = [slice(h * dh, (h + 1) * dh) for h in range(nh)]

    yc_prev = yc_ref[...]
    ym_prev = ym_ref[...]
    mix = jnp.concatenate(
        [_dot(yc_prev, wc[...]) + _dot(ym_prev, wm[...])
         for wc, wm in zip((wc0_ref, wc1_ref, wc2_ref, wc3_ref),
                           (wm0_ref, wm1_ref, wm2_ref, wm3_ref))], axis=1)

    def recurrence(heads):
        q = {h: q_ref[:, cols[h]] * jnp.asarray(dh ** -0.5, BF16) for h in heads}
        k = {h: k_ref[:, cols[h]] for h in heads}
        v = {h: v_ref[:, cols[h]] for h in heads}
        i_col = {h: g[:, h:h + 1] for h in heads}
        b_col = {h: bcum[:, nh + h:nh + h + 1] for h in heads}
        i_row = {h: g_t[h:h + 1, :] for h in heads}
        b_row = {h: bcum_t[nh + h:nh + h + 1, :] for h in heads}
        g_tot = {h: b_col[h][chunk - 1:chunk, :] for h in heads}
        m_prev = {h: m_ref[h, 0:1, 0:1] for h in heads}
        c_prev = {h: c_ref[h] for h in heads}
        n_prev = {h: n_ref[h, 0:1, :] for h in heads}

        qk = {h: _dot_t(q[h], k[h]) for h in heads}
        inter = {h: _dot(q[h], c_prev[h].astype(BF16)) for h in heads}
        qn = {h: jnp.sum(q[h].astype(F32) * n_prev[h], axis=-1, keepdims=True) for h in heads}

        log_d = {h: jnp.where(causal, b_col[h] - (b_row[h] - i_row[h]), neg_inf) for h in heads}
        inter_log = {h: b_col[h] + m_prev[h] for h in heads}
        m_t = {h: jnp.maximum(inter_log[h], jnp.max(log_d[h], axis=-1, keepdims=True))
               for h in heads}
        scores = {h: qk[h] * jnp.exp(log_d[h] - m_t[h]) for h in heads}
        inter_w = {h: jnp.exp(inter_log[h] - m_t[h]) for h in heads}
        num = {h: _dot(scores[h].astype(BF16), v[h]) + inter_w[h] * inter[h] for h in heads}
        den = {h: jnp.sum(scores[h], axis=-1, keepdims=True) + inter_w[h] * qn[h] for h in heads}
        hh = {h: num[h] / jnp.maximum(jnp.abs(den[h]), jnp.exp(-m_t[h])) for h in heads}

        w_log = {h: g_tot[h] - b_col[h] + i_col[h] for h in heads}
        m_new = {h: jnp.maximum(g_tot[h] + m_prev[h], jnp.max(w_log[h], axis=0, keepdims=True))
                 for h in heads}
        decay = {h: jnp.exp(g_tot[h] + m_prev[h] - m_new[h]) for h in heads}
        kw = {h: k[h].astype(F32) * jnp.exp(w_log[h] - m_new[h]) for h in heads}
        for h in heads:
            c_ref[h] = decay[h] * c_prev[h] + lax.dot_general(
                kw[h].astype(BF16), v[h], (((0,), (0,)), ((), ())), preferred_element_type=F32)
            n_ref[h, 0:1, :] = decay[h] * n_prev[h] + jnp.sum(kw[h], axis=0, keepdims=True)
            m_ref[h] = jnp.broadcast_to(m_new[h], m_ref.shape[1:])

        return hh

    def head_norm_gate(heads, hh):
        mu = {h: jnp.mean(hh[h], axis=-1, keepdims=True) for h in heads}
        hc = {h: hh[h] - mu[h] for h in heads}
        var = {h: jnp.mean(hc[h] * hc[h], axis=-1, keepdims=True) for h in heads}
        for h in heads:
            hn = hc[h] * lax.rsqrt(var[h] + HEAD_NORM_EPS) * gain_ref[:, cols[h]]
            o = o_ref[:, cols[h]].astype(F32)
            ym_ref[:, cols[h]] = (jax.nn.sigmoid(o) * hn).astype(ym_ref.dtype)

    groups = [tuple(range(h0, min(h0 + HEAD_GROUP, nh))) for h0 in range(0, nh, HEAD_GROUP)]
    hhs = [recurrence(group) for group in groups]
    z = alpha * x_ref[...] + mix
    x1_ref[...] = _layer_norm(z, lng_ref[...], lnb_ref[...])
    for group, hh in zip(groups, hhs):
        head_norm_gate(group, hh)

    gate_terms(xbn_ref[...])


def _mlstm_outproj(qkvo, xb2d, w_gates_t_bf, b_gates_row, gain_row, x2d, y_conv, w_out_bf,
                   ln_g, ln_b, *, seq, nh, dh, alpha):
    m, d = x2d.shape
    w = nh * dh
    wc = y_conv.shape[1]
    assert wc + w == w_out_bf.shape[0] and wc == w
    chunk = MLSTM_CHUNK
    n_chunks = m // chunk
    last = n_chunks - 1

    def cur(s):
        return jnp.minimum(s, last)

    def prev(s):
        return jnp.maximum(s - 1, 0)

    def blk(kind):
        return pl.BlockSpec((chunk, w), lambda s: (cur(s), kind))

    const = pl.Buffered(1)
    kern = functools.partial(_mlstm_outproj_kernel, nh=nh, dh=dh, chunks_per_seq=seq // chunk,
                             alpha=alpha)
    return pl.pallas_call(
        kern,
        out_shape=jax.ShapeDtypeStruct((m, d), F32),
        grid=(n_chunks + 1,),
        in_specs=[blk(0), blk(1), blk(2), blk(3),
                  pl.BlockSpec((chunk, d), lambda s: (0, 0)),
                  pl.BlockSpec((chunk, d), lambda s: (jnp.minimum(s + 1, last), 0)),
                  pl.BlockSpec((LANES, d), lambda s: (0, 0)),
                  pl.BlockSpec((1, LANES), lambda s: (0, 0)),
                  pl.BlockSpec((1, w), lambda s: (0, 0)),
                  pl.BlockSpec((chunk, d), lambda s: (prev(s), 0)),
                  pl.BlockSpec((chunk, wc), lambda s: (prev(s), 0)),
                  *[pl.BlockSpec((wc, d // 4), lambda s, p=p: (0, p), pipeline_mode=const)
                    for p in range(4)],
                  *[pl.BlockSpec((w, d // 4), lambda s, p=p: (1, p), pipeline_mode=const)
                    for p in range(4)],
                  pl.BlockSpec((1, d), lambda s: (0, 0)),
                  pl.BlockSpec((1, d), lambda s: (0, 0))],
        out_specs=pl.BlockSpec((chunk, d), lambda s: (prev(s), 0)),
        scratch_shapes=[
            pltpu.VMEM((nh, dh, dh), F32),
            pltpu.VMEM((nh, SUBLANES, dh), F32),
            pltpu.VMEM((nh, SUBLANES, LANES), F32),
            pltpu.VMEM((2, chunk, LANES), F32),
            pltpu.VMEM((2, LANES, chunk), F32),
            pltpu.VMEM((chunk, w), BF16),
        ],
        compiler_params=_params(("arbitrary",)),
        name="mlstm_outproj_ln",
    )(qkvo, qkvo, qkvo, qkvo, xb2d, xb2d, w_gates_t_bf, b_gates_row, gain_row,
      x2d, y_conv, *([w_out_bf] * 8), ln_g, ln_b)


def _convffn_ln_kernel(x_ref, wv_ref, wg_ref, cw_ref, cb_ref, wd_ref,
                       g_ref, b_ref, o_ref, xb_ref, acc_ref, carry_v_ref, carry_g_ref,
                       buf_v_ref, buf_g_ref, *, alpha, tiles_per_seq, chunk):
    i = pl.program_id(0)
    f = pl.program_id(1)
    nf = pl.num_programs(1)
    rows = x_ref.shape[0]
    tf = wv_ref.shape[1]
    first = (i % tiles_per_seq) == 0
    cols_v = pl.ds(pl.multiple_of(f * tf, tf), tf)
    cols_g = pl.ds(pl.multiple_of((nf + f) * tf, tf), tf)

    @pl.when(first)
    def _():
        carry_v_ref[f] = jnp.zeros(carry_v_ref.shape[1:], F32)
        carry_g_ref[f] = jnp.zeros(carry_g_ref.shape[1:], F32)

    def conv(buf_ref, cols, r0):
        h = buf_ref[SUBLANES + r0:SUBLANES + r0 + chunk, :]
        hm1 = buf_ref[SUBLANES - 1 + r0:SUBLANES - 1 + r0 + chunk, :]
        hm2 = buf_ref[SUBLANES - 2 + r0:SUBLANES - 2 + r0 + chunk, :]
        w = cw_ref[:, cols]
        return w[0:1, :] * hm2 + w[1:2, :] * hm1 + w[2:3, :] * h + cb_ref[:, cols]

    def step(is_first, is_last):
        row_starts = list(range(0, rows, chunk))
        buf_v_ref[0:SUBLANES, :] = carry_v_ref[f]
        buf_g_ref[0:SUBLANES, :] = carry_g_ref[f]
        for r0 in row_starts:
            if is_first:
                xb_ref[r0:r0 + chunk, :] = x_ref[r0:r0 + chunk, :].astype(BF16)
            xb = xb_ref[r0:r0 + chunk, :]
            buf_v_ref[SUBLANES + r0:SUBLANES + r0 + chunk, :] = _dot(xb, wv_ref[...])
            buf_g_ref[SUBLANES + r0:SUBLANES + r0 + chunk, :] = _dot(xb, wg_ref[...])
        for r0 in row_starts:
            rs = slice(r0, r0 + chunk)
            val = conv(buf_v_ref, cols_v, r0)
            gate = conv(buf_g_ref, cols_g, r0)
            act = (jax.nn.silu(gate) * val).astype(BF16)
            part = _dot(act, wd_ref[...])
            if is_first:
                acc_ref[rs, :] = part
            elif is_last:
                z = alpha * x_ref[rs, :] + (acc_ref[rs, :] + part)
                o_ref[rs, :] = _layer_norm(z, g_ref[...], b_ref[...])
            else:
                acc_ref[rs, :] += part
        carry_v_ref[f] = buf_v_ref[rows:rows + SUBLANES, :]
        carry_g_ref[f] = buf_g_ref[rows:rows + SUBLANES, :]

    pl.when(f == 0)(functools.partial(step, True, False))
    pl.when(jnp.logical_and(f > 0, f < nf - 1))(functools.partial(step, False, False))
    pl.when(f == nf - 1)(functools.partial(step, False, True))


def _convffn_ln(x2d, w_up_bf, w_ffn_conv, b_ffn_conv_row, w_down_bf, ln_g, ln_b, *,
                alpha, seq, d_ff):
    m, d = x2d.shape
    tm, tf = TM_FFN, TF_FFN
    nf = d_ff // tf
    kern = functools.partial(_convffn_ln_kernel, alpha=alpha, tiles_per_seq=seq // tm,
                             chunk=TR_FFN)
    return pl.pallas_call(
        kern,
        out_shape=jax.ShapeDtypeStruct((m, d), F32),
        grid=(m // tm, nf),
        in_specs=[
            pl.BlockSpec((tm, d), lambda i, f: (i, 0)),
            pl.BlockSpec((None, d, tf), lambda i, f: (f, 0, 0)),
            pl.BlockSpec((None, d, tf), lambda i, f: (nf + f, 0, 0)),
            pl.BlockSpec((CONV_WIDTH, 2 * d_ff), lambda i, f: (0, 0)),
            pl.BlockSpec((1, 2 * d_ff), lambda i, f: (0, 0)),
            pl.BlockSpec((tf, d), lambda i, f: (f, 0)),
            pl.BlockSpec((1, d), lambda i, f: (0, 0)),
            pl.BlockSpec((1, d), lambda i, f: (0, 0)),
        ],
        out_specs=pl.BlockSpec((tm, d), lambda i, f: (i, 0)),
        scratch_shapes=[
            pltpu.VMEM((tm, d), BF16),
            pltpu.VMEM((tm, d), F32),
            pltpu.VMEM((nf, SUBLANES, tf), F32),
            pltpu.VMEM((nf, SUBLANES, tf), F32),
            pltpu.VMEM((tm + SUBLANES, tf), F32),
            pltpu.VMEM((tm + SUBLANES, tf), F32),
        ],
        compiler_params=_params(("arbitrary", "arbitrary")),
        name="convffn_ln",
    )(x2d, w_up_bf, w_up_bf, w_ffn_conv, b_ffn_conv_row, w_down_bf, ln_g, ln_b)


def kernel(x, w_in, b_gates, w_sc_conv, mh_gain, w_out, ln1_g, ln1_b,
           w_up, w_ffn_conv, b_ffn_conv, w_down, ln2_g, ln2_b):
    batch, seq, d = x.shape
    depth = w_in.shape[0]
    nh = N_HEADS_M
    w_conv = w_sc_conv.shape[-1]
    w_mlstm = mh_gain.shape[-1]
    dh = w_mlstm // nh
    d_ff = w_down.shape[1]
    n_gates = 2 * nh
    gate_col0 = 3 * w_conv + 4 * w_mlstm
    assert w_in.shape[-1] == gate_col0 + n_gates and w_conv + w_mlstm == d
    assert seq % TM_CONV == 0 and seq % TM_FFN == 0 and seq % MLSTM_CHUNK == 0
    assert w_conv % TC_CONV == 0 and d_ff % TF_FFN == 0
    assert (3 * w_conv) % TN_QKVO == 0 and (4 * w_mlstm) % TN_QKVO == 0
    assert (batch * seq) % TM_QKVO == 0
    alpha = (2 * depth) ** 0.25

    x2d = x.reshape(batch * seq, d)
    for l in range(depth):
        w_in_t = w_in[l].T
        w_gates_t_bf = jnp.pad(w_in_t[gate_col0:], ((0, LANES - n_gates), (0, 0))).astype(BF16)
        b_gates_row = jnp.pad(b_gates[l], (0, LANES - n_gates)).reshape(1, LANES)

        y_conv, xb2d, w_down_bf, w_out_bf = _inproj_conv(
            x2d, w_in_t, w_sc_conv[l], w_down[l], w_out[l], seq=seq, w_conv=w_conv)
        qkvo, w_up_bf = _inproj_qkvo(xb2d, w_in_t, w_up[l], col0=3 * w_conv, width=4 * w_mlstm,
                                     tf=TF_FFN)
        x2d = _mlstm_outproj(qkvo, xb2d, w_gates_t_bf, b_gates_row, mh_gain[l].reshape(1, w_mlstm),
                             x2d, y_conv, w_out_bf, ln1_g[l].reshape(1, d), ln1_b[l].reshape(1, d),
                             seq=seq, nh=nh, dh=dh, alpha=alpha)
        x2d = _convffn_ln(x2d, w_up_bf, w_ffn_conv[l],
                          b_ffn_conv[l].reshape(1, 2 * d_ff), w_down_bf,
                          ln2_g[l].reshape(1, d), ln2_b[l].reshape(1, d),
                          alpha=alpha, seq=seq, d_ff=d_ff)
    return x2d.reshape(batch, seq, d)
```

```python
import functools

import jax
import jax.numpy as jnp
from jax import lax
from jax.experimental import pallas as pl
from jax.experimental.pallas import tpu as pltpu

F32 = jnp.float32
BF16 = jnp.bfloat16

N_HEADS_M = 4
CONV_WIDTH = 3
LN_EPS = 1e-5
HEAD_NORM_EPS = 1e-6

SUBLANES = 8
LANES = 128
BF16_SUBLANES = 16
VMEM_LIMIT_BYTES = 56 * 1024 * 1024

TM_CONV = 512
TC_CONV = 512
TR_CONV = 256
TM_QKVO = 512
TN_QKVO = 1024
MLSTM_CHUNK = 256
HEAD_GROUP = 4
TM_FFN = 512
TF_FFN = 512
TR_FFN = 256


def _dot(a, b):
    return jnp.dot(a, b, preferred_element_type=F32)


def _dot_t(a, b_t):
    return lax.dot_general(a, b_t, (((1,), (1,)), ((), ())), preferred_element_type=F32)


def _params(semantics, flags=None):
    return pltpu.CompilerParams(dimension_semantics=semantics,
                                vmem_limit_bytes=VMEM_LIMIT_BYTES, flags=flags)


def _cast_rows_spec(rows, cols, n_steps, step_of):
    assert rows % n_steps == 0 and (rows // n_steps) % BF16_SUBLANES == 0
    return pl.BlockSpec((rows // n_steps, cols), lambda *ids: (step_of(*ids), 0))


def _inproj_conv_kernel(x_ref, wb_ref, wc_ref, wh_ref, cw_ref, wdn_ref, wout_ref,
                        y_ref, xb_ref, wdn_bf_ref, wout_bf_ref,
                        wbf_ref, carry_ref, buf_ref, *, tiles_per_seq, chunk):
    i = pl.program_id(1)
    rows = x_ref.shape[0]

    @pl.when(i == 0)
    def _():
        wbf_ref[0] = wb_ref[...].astype(BF16)
        wbf_ref[1] = wc_ref[...].astype(BF16)
        wbf_ref[2] = wh_ref[...].astype(BF16)

    @pl.when(pl.program_id(0) == 0)
    def _():
        xb_ref[...] = x_ref[...].astype(BF16)

    @pl.when(i % tiles_per_seq == 0)
    def _():
        carry_ref[...] = jnp.zeros(carry_ref.shape, F32)

    wdn_bf_ref[...] = wdn_ref[...].astype(BF16)
    wout_bf_ref[...] = wout_ref[...].astype(BF16)

    buf_ref[0:SUBLANES, :] = carry_ref[...]
    w = cw_ref[...]
    row_starts = list(range(0, rows, chunk))
    xbs = [x_ref[r0:r0 + chunk, :].astype(BF16) for r0 in row_starts]
    for r0, xb in zip(row_starts, xbs):
        u = _dot_t(xb, wbf_ref[1]) * _dot_t(xb, wbf_ref[2])
        buf_ref[SUBLANES + r0:SUBLANES + r0 + chunk, :] = u
    c_bs = [_dot_t(xb, wbf_ref[0]) for xb in xbs]
    for r0, c_b in zip(row_starts, c_bs):
        u = buf_ref[SUBLANES + r0:SUBLANES + r0 + chunk, :]
        um2 = buf_ref[SUBLANES - 2 + r0:SUBLANES - 2 + r0 + chunk, :]
        um1 = buf_ref[SUBLANES - 1 + r0:SUBLANES - 1 + r0 + chunk, :]
        y = w[0:1, :] * um2 + w[1:2, :] * um1 + w[2:3, :] * u
        y_ref[r0:r0 + chunk, :] = (c_b * y).astype(y_ref.dtype)
    carry_ref[...] = buf_ref[rows:rows + SUBLANES, :]


def _inproj_conv(x2d, w_in_t, w_sc_conv, w_down, w_out, *, seq, w_conv):
    m, d = x2d.shape
    tm, tc = TM_CONV, TC_CONV
    nj, ni = w_conv // tc, m // tm
    n_steps = nj * ni
    step_of = lambda j, i: j * ni + i
    once = pl.Buffered(1)
    kern = functools.partial(_inproj_conv_kernel, tiles_per_seq=seq // tm, chunk=TR_CONV)
    return pl.pallas_call(
        kern,
        out_shape=(jax.ShapeDtypeStruct((m, w_conv), BF16),
                   jax.ShapeDtypeStruct((m, d), BF16),
                   jax.ShapeDtypeStruct(w_down.shape, BF16),
                   jax.ShapeDtypeStruct(w_out.shape, BF16)),
        grid=(nj, ni),
        in_specs=[
            pl.BlockSpec((tm, d), lambda j, i: (i, 0)),
            pl.BlockSpec((tc, d), lambda j, i: (j, 0), pipeline_mode=once),
            pl.BlockSpec((tc, d), lambda j, i: (nj + j, 0), pipeline_mode=once),
            pl.BlockSpec((tc, d), lambda j, i: (2 * nj + j, 0), pipeline_mode=once),
            pl.BlockSpec((CONV_WIDTH, tc), lambda j, i: (0, j)),
            _cast_rows_spec(*w_down.shape, n_steps, step_of),
            _cast_rows_spec(*w_out.shape, n_steps, step_of),
        ],
        out_specs=(pl.BlockSpec((tm, tc), lambda j, i: (i, j)),
                   pl.BlockSpec((tm, d), lambda j, i: (jnp.where(j == 0, i, ni - 1), 0)),
                   _cast_rows_spec(*w_down.shape, n_steps, step_of),
                   _cast_rows_spec(*w_out.shape, n_steps, step_of)),
        scratch_shapes=[
            pltpu.VMEM((3, tc, d), BF16),
            pltpu.VMEM((SUBLANES, tc), F32),
            pltpu.VMEM((tm + SUBLANES, tc), F32),
        ],
        compiler_params=_params(("arbitrary", "arbitrary")),
        name="inproj_shortconv",
    )(x2d, w_in_t, w_in_t, w_in_t, w_sc_conv, w_down, w_out)


def _inproj_qkvo_kernel(xb_ref, w_ref, wup_ref, o_ref, wup_bf_ref, wbf_ref):
    @pl.when(pl.program_id(1) == 0)
    def _():
        wbf_ref[...] = w_ref[...].astype(BF16)

    n_blocks, _, tf = wup_bf_ref.shape
    for n in range(n_blocks):
        wup_bf_ref[n] = wup_ref[:, n * tf:(n + 1) * tf].astype(BF16)
    o_ref[...] = _dot_t(xb_ref[...], wbf_ref[...]).astype(o_ref.dtype)


def _inproj_qkvo(xb2d, w_in_t, w_up, *, col0, width, tf):
    m, d = xb2d.shape
    tm, tn = TM_QKVO, TN_QKVO
    j0 = col0 // tn
    nj, ni = width // tn, m // tm
    n_steps = nj * ni
    step_of = lambda j, i: j * ni + i
    up_rows, up_cols = w_up.shape
    n_blocks = up_cols // tf
    rows_per_step = up_rows // n_steps
    assert up_cols % tf == 0 and up_rows % n_steps == 0 and rows_per_step % BF16_SUBLANES == 0
    return pl.pallas_call(
        _inproj_qkvo_kernel,
        out_shape=(jax.ShapeDtypeStruct((m, width), BF16),
                   jax.ShapeDtypeStruct((n_blocks, up_rows, tf), BF16)),
        grid=(nj, ni),
        in_specs=[
            pl.BlockSpec((tm, d), lambda j, i: (i, 0)),
            pl.BlockSpec((tn, d), lambda j, i: (j0 + j, 0)),
            _cast_rows_spec(up_rows, up_cols, n_steps, step_of),
        ],
        out_specs=(pl.BlockSpec((tm, tn), lambda j, i: (i, j)),
                   pl.BlockSpec((n_blocks, rows_per_step, tf),
                                lambda j, i: (0, step_of(j, i), 0))),
        scratch_shapes=[pltpu.VMEM((tn, d), BF16)],
        compiler_params=_params(("arbitrary", "arbitrary")),
        name="inproj_mlstm",
    )(xb2d, w_in_t, w_up)


def _log_sigmoid(x):
    return jnp.minimum(x, 0.0) - jnp.log1p(jnp.exp(-jnp.abs(x)))


def _layer_norm(z, g, b):
    mu = jnp.mean(z, axis=-1, keepdims=True)
    zc = z - mu
    var = jnp.mean(zc * zc, axis=-1, keepdims=True)
    return zc * lax.rsqrt(var + LN_EPS) * g + b


def _mlstm_outproj_kernel(q_ref, k_ref, v_ref, o_ref, xb0_ref, xbn_ref, wg_ref, bg_ref, gain_ref,
                          x_ref, yc_ref, wc0_ref, wc1_ref, wc2_ref, wc3_ref,
                          wm0_ref, wm1_ref, wm2_ref, wm3_ref, lng_ref, lnb_ref, x1_ref,
                          c_ref, n_ref, m_ref, gates_ref, gates_t_ref, ym_ref, *,
                          nh, dh, chunks_per_seq, alpha):
    chunk = q_ref.shape[0]
    step = pl.program_id(0)

    @pl.when(step % chunks_per_seq == 0)
    def _():
        c_ref[...] = jnp.zeros(c_ref.shape, F32)
        n_ref[...] = jnp.zeros(n_ref.shape, F32)
        m_ref[...] = jnp.zeros(m_ref.shape, F32)

    row = lax.broadcasted_iota(jnp.int32, (chunk, chunk), 0)
    col = lax.broadcasted_iota(jnp.int32, (chunk, chunk), 1)
    causal = row >= col

    def gate_terms(xb):
        g = _dot_t(xb, wg_ref[...]) + bg_ref[...]
        lf = _log_sigmoid(g)
        bcum = jnp.dot(causal.astype(F32), lf, precision=lax.Precision.HIGHEST,
                       preferred_element_type=F32)
        gates_ref[0] = g
        gates_ref[1] = bcum
        gates_t_ref[0] = g.T
        gates_t_ref[1] = bcum.T

    @pl.when(step == 0)
    def _():
        gate_terms(xb0_ref[...])
        ym_ref[...] = jnp.zeros(ym_ref.shape, ym_ref.dtype)

    g = gates_ref[0]
    bcum = gates_ref[1]
    g_t = gates_t_ref[0]
    bcum_t = gates_t_ref[1]
    neg_inf = jnp.float32(-jnp.inf)
    cols = [slice(h * dh, (h + 1) * dh) for h in range(nh)]

    yc_prev = yc_ref[...]
    ym_prev = ym_ref[...]
    mix = jnp.concatenate(
        [_dot(yc_prev, wc[...]) + _dot(ym_prev, wm[...])
         for wc, wm in zip((wc0_ref, wc1_ref, wc2_ref, wc3_ref),
                           (wm0_ref, wm1_ref, wm2_ref, wm3_ref))], axis=1)

    def recurrence(heads):
        q = {h: q_ref[:, cols[h]] * jnp.asarray(dh ** -0.5, BF16) for h in heads}
        k = {h: k_ref[:, cols[h]] for h in heads}
        v = {h: v_ref[:, cols[h]] for h in heads}
        i_col = {h: g[:, h:h + 1] for h in heads}
        b_col = {h: bcum[:, nh + h:nh + h + 1] for h in heads}
        i_row = {h: g_t[h:h + 1, :] for h in heads}
        b_row = {h: bcum_t[nh + h:nh + h + 1, :] for h in heads}
        g_tot = {h: b_col[h][chunk - 1:chunk, :] for h in heads}
        m_prev = {h: m_ref[h, 0:1, 0:1] for h in heads}
        c_prev = {h: c_ref[h] for h in heads}
        n_prev = {h: n_ref[h, 0:1, :] for h in heads}

        qk = {h: _dot_t(q[h], k[h]) for h in heads}
        inter = {h: _dot(q[h], c_prev[h].astype(BF16)) for h in heads}
        qn = {h: jnp.sum(q[h].astype(F32) * n_prev[h], axis=-1, keepdims=True) for h in heads}

        log_d = {h: jnp.where(causal, b_col[h] - (b_row[h] - i_row[h]), neg_inf) for h in heads}
        inter_log = {h: b_col[h] + m_prev[h] for h in heads}
        m_t = {h: jnp.maximum(inter_log[h], jnp.max(log_d[h], axis=-1, keepdims=True))
               for h in heads}
        scores = {h: qk[h] * jnp.exp(log_d[h] - m_t[h]) for h in heads}
        inter_w = {h: jnp.exp(inter_log[h] - m_t[h]) for h in heads}
        num = {h: _dot(scores[h].astype(BF16), v[h]) + inter_w[h] * inter[h] for h in heads}
        den = {h: jnp.sum(scores[h], axis=-1, keepdims=True) + inter_w[h] * qn[h] for h in heads}
        hh = {h: num[h] / jnp.maximum(jnp.abs(den[h]), jnp.exp(-m_t[h])) for h in heads}

        w_log = {h: g_tot[h] - b_col[h] + i_col[h] for h in heads}
        m_new = {h: jnp.maximum(g_tot[h] + m_prev[h], jnp.max(w_log[h], axis=0, keepdims=True))
                 for h in heads}
        decay = {h: jnp.exp(g_tot[h] + m_prev[h] - m_new[h]) for h in heads}
        kw = {h: k[h].astype(F32) * jnp.exp(w_log[h] - m_new[h]) for h in heads}
        for h in heads:
            c_ref[h] = decay[h] * c_prev[h] + lax.dot_general(
                kw[h].astype(BF16), v[h], (((0,), (0,)), ((), ())), preferred_element_type=F32)
            n_ref[h, 0:1, :] = decay[h] * n_prev[h] + jnp.sum(kw[h], axis=0, keepdims=True)
            m_ref[h] = jnp.broadcast_to(m_new[h], m_ref.shape[1:])

        return hh

    def head_norm_gate(heads, hh):
        mu = {h: jnp.mean(hh[h], axis=-1, keepdims=True) for h in heads}
        hc = {h: hh[h] - mu[h] for h in heads}
        var = {h: jnp.mean(hc[h] * hc[h], axis=-1, keepdims=True) for h in heads}
        for h in heads:
            hn = hc[h] * lax.rsqrt(var[h] + HEAD_NORM_EPS) * gain_ref[:, cols[h]]
            o = o_ref[:, cols[h]].astype(F32)
            ym_ref[:, cols[h]] = (jax.nn.sigmoid(o) * hn).astype(ym_ref.dtype)

    groups = [tuple(range(h0, min(h0 + HEAD_GROUP, nh))) for h0 in range(0, nh, HEAD_GROUP)]
    hhs = [recurrence(group) for group in groups]
    z = alpha * x_ref[...] + mix
    x1_ref[...] = _layer_norm(z, lng_ref[...], lnb_ref[...])
    for group, hh in zip(groups, hhs):
        head_norm_gate(group, hh)

    gate_terms(xbn_ref[...])


def _mlstm_outproj(qkvo, xb2d, w_gates_t_bf, b_gates_row, gain_row, x2d, y_conv, w_out_bf,
                   ln_g, ln_b, *, seq, nh, dh, alpha):
    m, d = x2d.shape
    w = nh * dh
    wc = y_conv.shape[1]
    assert wc + w == w_out_bf.shape[0] and wc == w
    chunk = MLSTM_CHUNK
    n_chunks = m // chunk
    last = n_chunks - 1

    def cur(s):
        return jnp.minimum(s, last)

    def prev(s):
        return jnp.maximum(s - 1, 0)

    def blk(kind):
        return pl.BlockSpec((chunk, w), lambda s: (cur(s), kind))

    const = pl.Buffered(1)
    kern = functools.partial(_mlstm_outproj_kernel, nh=nh, dh=dh, chunks_per_seq=seq // chunk,
                             alpha=alpha)
    return pl.pallas_call(
        kern,
        out_shape=jax.ShapeDtypeStruct((m, d), F32),
        grid=(n_chunks + 1,),
        in_specs=[blk(0), blk(1), blk(2), blk(3),
                  pl.BlockSpec((chunk, d), lambda s: (0, 0)),
                  pl.BlockSpec((chunk, d), lambda s: (jnp.minimum(s + 1, last), 0)),
                  pl.BlockSpec((LANES, d), lambda s: (0, 0)),
                  pl.BlockSpec((1, LANES), lambda s: (0, 0)),
                  pl.BlockSpec((1, w), lambda s: (0, 0)),
                  pl.BlockSpec((chunk, d), lambda s: (prev(s), 0)),
                  pl.BlockSpec((chunk, wc), lambda s: (prev(s), 0)),
                  *[pl.BlockSpec((wc, d // 4), lambda s, p=p: (0, p), pipeline_mode=const)
                    for p in range(4)],
                  *[pl.BlockSpec((w, d // 4), lambda s, p=p: (1, p), pipeline_mode=const)
                    for p in range(4)],
                  pl.BlockSpec((1, d), lambda s: (0, 0)),
                  pl.BlockSpec((1, d), lambda s: (0, 0))],
        out_specs=pl.BlockSpec((chunk, d), lambda s: (prev(s), 0)),
        scratch_shapes=[
            pltpu.VMEM((nh, dh, dh), F32),
            pltpu.VMEM((nh, SUBLANES, dh), F32),
            pltpu.VMEM((nh, SUBLANES, LANES), F32),
            pltpu.VMEM((2, chunk, LANES), F32),
            pltpu.VMEM((2, LANES, chunk), F32),
            pltpu.VMEM((chunk, w), BF16),
        ],
        compiler_params=_params(("arbitrary",)),
        name="mlstm_outproj_ln",
    )(qkvo, qkvo, qkvo, qkvo, xb2d, xb2d, w_gates_t_bf, b_gates_row, gain_row,
      x2d, y_conv, *([w_out_bf] * 8), ln_g, ln_b)


def _convffn_ln_kernel(x_ref, wv_ref, wg_ref, cw_ref, cb_ref, wd_ref,
                       g_ref, b_ref, o_ref, xb_ref, acc_ref, carry_v_ref, carry_g_ref,
                       buf_v_ref, buf_g_ref, *, alpha, tiles_per_seq, chunk):
    i = pl.program_id(0)
    f = pl.program_id(1)
    nf = pl.num_programs(1)
    rows = x_ref.shape[0]
    tf = wv_ref.shape[1]
    first = (i % tiles_per_seq) == 0
    cols_v = pl.ds(pl.multiple_of(f * tf, tf), tf)
    cols_g = pl.ds(pl.multiple_of((nf + f) * tf, tf), tf)

    @pl.when(first)
    def _():
        carry_v_ref[f] = jnp.zeros(carry_v_ref.shape[1:], F32)
        carry_g_ref[f] = jnp.zeros(carry_g_ref.shape[1:], F32)

    def conv(buf_ref, cols, r0):
        h = buf_ref[SUBLANES + r0:SUBLANES + r0 + chunk, :]
        hm1 = buf_ref[SUBLANES - 1 + r0:SUBLANES - 1 + r0 + chunk, :]
        hm2 = buf_ref[SUBLANES - 2 + r0:SUBLANES - 2 + r0 + chunk, :]
        w = cw_ref[:, cols]
        return w[0:1, :] * hm2 + w[1:2, :] * hm1 + w[2:3, :] * h + cb_ref[:, cols]

    def step(is_first, is_last):
        row_starts = list(range(0, rows, chunk))
        buf_v_ref[0:SUBLANES, :] = carry_v_ref[f]
        buf_g_ref[0:SUBLANES, :] = carry_g_ref[f]
        for r0 in row_starts:
            if is_first:
                xb_ref[r0:r0 + chunk, :] = x_ref[r0:r0 + chunk, :].astype(BF16)
            xb = xb_ref[r0:r0 + chunk, :]
            buf_v_ref[SUBLANES + r0:SUBLANES + r0 + chunk, :] = _dot(xb, wv_ref[...])
            buf_g_ref[SUBLANES + r0:SUBLANES + r0 + chunk, :] = _dot(xb, wg_ref[...])
        for r0 in row_starts:
            rs = slice(r0, r0 + chunk)
            val = conv(buf_v_ref, cols_v, r0)
            gate = conv(buf_g_ref, cols_g, r0)
            act = (jax.nn.silu(gate) * val).astype(BF16)
            part = _dot(act, wd_ref[...])
            if is_first:
                acc_ref[rs, :] = part
            elif is_last:
                z = alpha * x_ref[rs, :] + (acc_ref[rs, :] + part)
                o_ref[rs, :] = _layer_norm(z, g_ref[...], b_ref[...])
            else:
                acc_ref[rs, :] += part
        carry_v_ref[f] = buf_v_ref[rows:rows + SUBLANES, :]
        carry_g_ref[f] = buf_g_ref[rows:rows + SUBLANES, :]

    pl.when(f == 0)(functools.partial(step, True, False))
    pl.when(jnp.logical_and(f > 0, f < nf - 1))(functools.partial(step, False, False))
    pl.when(f == nf - 1)(functools.partial(step, False, True))


def _convffn_ln(x2d, w_up_bf, w_ffn_conv, b_ffn_conv_row, w_down_bf, ln_g, ln_b, *,
                alpha, seq, d_ff):
    m, d = x2d.shape
    tm, tf = TM_FFN, TF_FFN
    nf = d_ff // tf
    kern = functools.partial(_convffn_ln_kernel, alpha=alpha, tiles_per_seq=seq // tm,
                             chunk=TR_FFN)
    return pl.pallas_call(
        kern,
        out_shape=jax.ShapeDtypeStruct((m, d), F32),
        grid=(m // tm, nf),
        in_specs=[
            pl.BlockSpec((tm, d), lambda i, f: (i, 0)),
            pl.BlockSpec((None, d, tf), lambda i, f: (f, 0, 0)),
            pl.BlockSpec((None, d, tf), lambda i, f: (nf + f, 0, 0)),
            pl.BlockSpec((CONV_WIDTH, 2 * d_ff), lambda i, f: (0, 0)),
            pl.BlockSpec((1, 2 * d_ff), lambda i, f: (0, 0)),
            pl.BlockSpec((tf, d), lambda i, f: (f, 0)),
            pl.BlockSpec((1, d), lambda i, f: (0, 0)),
            pl.BlockSpec((1, d), lambda i, f: (0, 0)),
        ],
        out_specs=pl.BlockSpec((tm, d), lambda i, f: (i, 0)),
        scratch_shapes=[
            pltpu.VMEM((tm, d), BF16),
            pltpu.VMEM((tm, d), F32),
            pltpu.VMEM((nf, SUBLANES, tf), F32),
            pltpu.VMEM((nf, SUBLANES, tf), F32),
            pltpu.VMEM((tm + SUBLANES, tf), F32),
            pltpu.VMEM((tm + SUBLANES, tf), F32),
        ],
        compiler_params=_params(("arbitrary", "arbitrary")),
        name="convffn_ln",
    )(x2d, w_up_bf, w_up_bf, w_ffn_conv, b_ffn_conv_row, w_down_bf, ln_g, ln_b)


def kernel(x, w_in, b_gates, w_sc_conv, mh_gain, w_out, ln1_g, ln1_b,
           w_up, w_ffn_conv, b_ffn_conv, w_down, ln2_g, ln2_b):
    batch, seq, d = x.shape
    depth = w_in.shape[0]
    nh = N_HEADS_M
    w_conv = w_sc_conv.shape[-1]
    w_mlstm = mh_gain.shape[-1]
    dh = w_mlstm // nh
    d_ff = w_down.shape[1]
    n_gates = 2 * nh
    gate_col0 = 3 * w_conv + 4 * w_mlstm
    assert w_in.shape[-1] == gate_col0 + n_gates and w_conv + w_mlstm == d
    assert seq % TM_CONV == 0 and seq % TM_FFN == 0 and seq % MLSTM_CHUNK == 0
    assert w_conv % TC_CONV == 0 and d_ff % TF_FFN == 0
    assert (3 * w_conv) % TN_QKVO == 0 and (4 * w_mlstm) % TN_QKVO == 0
    assert (batch * seq) % TM_QKVO == 0
    alpha = (2 * depth) ** 0.25

    x2d = x.reshape(batch * seq, d)
    for l in range(depth):
        w_in_t = w_in[l].T
        w_gates_t_bf = jnp.pad(w_in_t[gate_col0:], ((0, LANES - n_gates), (0, 0))).astype(BF16)
        b_gates_row = jnp.pad(b_gates[l], (0, LANES - n_gates)).reshape(1, LANES)

        y_conv, xb2d, w_down_bf, w_out_bf = _inproj_conv(
            x2d, w_in_t, w_sc_conv[l], w_down[l], w_out[l], seq=seq, w_conv=w_conv)
        qkvo, w_up_bf = _inproj_qkvo(xb2d, w_in_t, w_up[l], col0=3 * w_conv, width=4 * w_mlstm,
                                     tf=TF_FFN)
        x2d = _mlstm_outproj(qkvo, xb2d, w_gates_t_bf, b_gates_row, mh_gain[l].reshape(1, w_mlstm),
                             x2d, y_conv, w_out_bf, ln1_g[l].reshape(1, d), ln1_b[l].reshape(1, d),
                             seq=seq, nh=nh, dh=dh, alpha=alpha)
        x2d = _convffn_ln(x2d, w_up_bf, w_ffn_conv[l],
                          b_ffn_conv[l].reshape(1, 2 * d_ff), w_down_bf,
                          ln2_g[l].reshape(1, d), ln2_b[l].reshape(1, d),
                          alpha=alpha, seq=seq, d_ff=d_ff)
    return x2d.reshape(batch, seq, d)
```

```python
import functools

import jax
import jax.numpy as jnp
from jax import lax
from jax.experimental import pallas as pl
from jax.experimental.pallas import tpu as pltpu

F32 = jnp.float32
BF16 = jnp.bfloat16

N_HEADS_M = 4
CONV_WIDTH = 3
LN_EPS = 1e-5
HEAD_NORM_EPS = 1e-6

SUBLANES = 8
LANES = 128
BF16_SUBLANES = 16
VMEM_LIMIT_BYTES = 56 * 1024 * 1024

TM_CONV = 512
TC_CONV = 512
TR_CONV = 256
TM_QKVO = 1024
TN_QKVO = 1024
MLSTM_CHUNK = 256
HEAD_GROUP = 4
TM_FFN = 512
TF_FFN = 512
TR_FFN = 256
WEIGHT_SLOTS = 3


def _dot(a, b):
    return jnp.dot(a, b, preferred_element_type=F32)


def _dot_t(a, b_t):
    return lax.dot_general(a, b_t, (((1,), (1,)), ((), ())), preferred_element_type=F32)


def _params(semantics, flags=None):
    return pltpu.CompilerParams(dimension_semantics=semantics,
                                vmem_limit_bytes=VMEM_LIMIT_BYTES, flags=flags)


def _cast_rows_spec(rows, cols, n_steps, step_of):
    assert rows % n_steps == 0 and (rows // n_steps) % BF16_SUBLANES == 0
    return pl.BlockSpec((rows // n_steps, cols), lambda *ids: (step_of(*ids), 0))


def _inproj_conv_kernel(x_ref, wb_ref, wc_ref, wh_ref, cw_ref, wdn_ref, wout_ref,
                        y_ref, xb_ref, wdn_bf_ref, wout_bf_ref,
                        wbf_ref, carry_ref, buf_ref, *, tiles_per_seq, chunk):
    i = pl.program_id(1)
    rows = x_ref.shape[0]

    @pl.when(i == 0)
    def _():
        wbf_ref[0] = wb_ref[...].astype(BF16)
        wbf_ref[1] = wc_ref[...].astype(BF16)
        wbf_ref[2] = wh_ref[...].astype(BF16)

    @pl.when(pl.program_id(0) == 0)
    def _():
        xb_ref[...] = x_ref[...].astype(BF16)

    @pl.when(i % tiles_per_seq == 0)
    def _():
        carry_ref[...] = jnp.zeros(carry_ref.shape, F32)

    wdn_bf_ref[...] = wdn_ref[...].astype(BF16)
    wout_bf_ref[...] = wout_ref[...].astype(BF16)

    buf_ref[0:SUBLANES, :] = carry_ref[...]
    w = cw_ref[...]
    row_starts = list(range(0, rows, chunk))
    xbs = [x_ref[r0:r0 + chunk, :].astype(BF16) for r0 in row_starts]
    for r0, xb in zip(row_starts, xbs):
        u = _dot_t(xb, wbf_ref[1]) * _dot_t(xb, wbf_ref[2])
        buf_ref[SUBLANES + r0:SUBLANES + r0 + chunk, :] = u
    c_bs = [_dot_t(xb, wbf_ref[0]) for xb in xbs]
    for r0, c_b in zip(row_starts, c_bs):
        u = buf_ref[SUBLANES + r0:SUBLANES + r0 + chunk, :]
        um2 = buf_ref[SUBLANES - 2 + r0:SUBLANES - 2 + r0 + chunk, :]
        um1 = buf_ref[SUBLANES - 1 + r0:SUBLANES - 1 + r0 + chunk, :]
        y = w[0:1, :] * um2 + w[1:2, :] * um1 + w[2:3, :] * u
        y_ref[r0:r0 + chunk, :] = (c_b * y).astype(y_ref.dtype)
    carry_ref[...] = buf_ref[rows:rows + SUBLANES, :]


def _inproj_conv(x2d, w_in_t, w_sc_conv, w_down, w_out, *, seq, w_conv):
    m, d = x2d.shape
    tm, tc = TM_CONV, TC_CONV
    nj, ni = w_conv // tc, m // tm
    n_steps = nj * ni
    step_of = lambda j, i: j * ni + i
    once = pl.Buffered(1)
    kern = functools.partial(_inproj_conv_kernel, tiles_per_seq=seq // tm, chunk=TR_CONV)
    return pl.pallas_call(
        kern,
        out_shape=(jax.ShapeDtypeStruct((m, w_conv), BF16),
                   jax.ShapeDtypeStruct((m, d), BF16),
                   jax.ShapeDtypeStruct(w_down.shape, BF16),
                   jax.ShapeDtypeStruct(w_out.shape, BF16)),
        grid=(nj, ni),
        in_specs=[
            pl.BlockSpec((tm, d), lambda j, i: (i, 0)),
            pl.BlockSpec((tc, d), lambda j, i: (j, 0), pipeline_mode=once),
            pl.BlockSpec((tc, d), lambda j, i: (nj + j, 0), pipeline_mode=once),
            pl.BlockSpec((tc, d), lambda j, i: (2 * nj + j, 0), pipeline_mode=once),
            pl.BlockSpec((CONV_WIDTH, tc), lambda j, i: (0, j)),
            _cast_rows_spec(*w_down.shape, n_steps, step_of),
            _cast_rows_spec(*w_out.shape, n_steps, step_of),
        ],
        out_specs=(pl.BlockSpec((tm, tc), lambda j, i: (i, j)),
                   pl.BlockSpec((tm, d), lambda j, i: (jnp.where(j == 0, i, ni - 1), 0)),
                   _cast_rows_spec(*w_down.shape, n_steps, step_of),
                   _cast_rows_spec(*w_out.shape, n_steps, step_of)),
        scratch_shapes=[
            pltpu.VMEM((3, tc, d), BF16),
            pltpu.VMEM((SUBLANES, tc), F32),
            pltpu.VMEM((tm + SUBLANES, tc), F32),
        ],
        compiler_params=_params(("arbitrary", "arbitrary")),
        name="inproj_shortconv",
    )(x2d, w_in_t, w_in_t, w_in_t, w_sc_conv, w_down, w_out)


def _inproj_qkvo_kernel(xb_ref, w_ref, wup_ref, o_ref, wup_bf_ref, wbf_ref):
    @pl.when(pl.program_id(1) == 0)
    def _():
        wbf_ref[...] = w_ref[...].astype(BF16)

    n_blocks, _, tf = wup_bf_ref.shape
    for n in range(n_blocks):
        wup_bf_ref[n] = wup_ref[:, n * tf:(n + 1) * tf].astype(BF16)
    o_ref[...] = _dot_t(xb_ref[...], wbf_ref[...]).astype(o_ref.dtype)


def _inproj_qkvo(xb2d, w_in_t, w_up, *, col0, width, tf):
    m, d = xb2d.shape
    tm, tn = TM_QKVO, TN_QKVO
    j0 = col0 // tn
    nj, ni = width // tn, m // tm
    n_steps = nj * ni
    step_of = lambda j, i: j * ni + i
    up_rows, up_cols = w_up.shape
    n_blocks = up_cols // tf
    rows_per_step = up_rows // n_steps
    assert up_cols % tf == 0 and up_rows % n_steps == 0 and rows_per_step % BF16_SUBLANES == 0
    return pl.pallas_call(
        _inproj_qkvo_kernel,
        out_shape=(jax.ShapeDtypeStruct((m, width), BF16),
                   jax.ShapeDtypeStruct((n_blocks, up_rows, tf), BF16)),
        grid=(nj, ni),
        in_specs=[
            pl.BlockSpec((tm, d), lambda j, i: (i, 0)),
            pl.BlockSpec((tn, d), lambda j, i: (j0 + j, 0)),
            _cast_rows_spec(up_rows, up_cols, n_steps, step_of),
        ],
        out_specs=(pl.BlockSpec((tm, tn), lambda j, i: (i, j)),
                   pl.BlockSpec((n_blocks, rows_per_step, tf),
                                lambda j, i: (0, step_of(j, i), 0))),
        scratch_shapes=[pltpu.VMEM((tn, d), BF16)],
        compiler_params=_params(("arbitrary", "arbitrary")),
        name="inproj_mlstm",
    )(xb2d, w_in_t, w_up)


def _log_sigmoid(x):
    return jnp.minimum(x, 0.0) - jnp.log1p(jnp.exp(-jnp.abs(x)))


def _layer_norm(z, g, b):
    mu = jnp.mean(z, axis=-1, keepdims=True)
    zc = z - mu
    var = jnp.mean(zc * zc, axis=-1, keepdims=True)
    return zc * lax.rsqrt(var + LN_EPS) * g + b


def _mlstm_outproj_kernel(q_ref, k_ref, v_ref, o_ref, xb0_ref, xbn_ref, wg_ref, bg_ref, gain_ref,
                          x_ref, yc_ref, wc0_ref, wc1_ref, wc2_ref, wc3_ref,
                          wm0_ref, wm1_ref, wm2_ref, wm3_ref, lng_ref, lnb_ref, x1_ref,
                          c_ref, n_ref, m_ref, gates_ref, gates_t_ref, ym_ref, *,
                          nh, dh, chunks_per_seq, alpha):
    chunk = q_ref.shape[0]
    step = pl.program_id(0)

    @pl.when(step % chunks_per_seq == 0)
    def _():
        c_ref[...] = jnp.zeros(c_ref.shape, F32)
        n_ref[...] = jnp.zeros(n_ref.shape, F32)
        m_ref[...] = jnp.zeros(m_ref.shape, F32)

    row = lax.broadcasted_iota(jnp.int32, (chunk, chunk), 0)
    col = lax.broadcasted_iota(jnp.int32, (chunk, chunk), 1)
    causal = row >= col

    def gate_terms(xb):
        g = _dot_t(xb, wg_ref[...]) + bg_ref[...]
        lf = _log_sigmoid(g)
        bcum = jnp.dot(causal.astype(F32), lf, precision=lax.Precision.HIGHEST,
                       preferred_element_type=F32)
        gates_ref[0] = g
        gates_ref[1] = bcum
        gates_t_ref[0] = g.T
        gates_t_ref[1] = bcum.T

    @pl.when(step == 0)
    def _():
        gate_terms(xb0_ref[...])
        ym_ref[...] = jnp.zeros(ym_ref.shape, ym_ref.dtype)

    g = gates_ref[0]
    bcum = gates_ref[1]
    g_t = gates_t_ref[0]
    bcum_t = gates_t_ref[1]
    neg_inf = jnp.float32(-jnp.inf)
    cols = [slice(h * dh, (h + 1) * dh) for h in range(nh)]

    yc_prev = yc_ref[...]
    ym_prev = ym_ref[...]
    mix = jnp.concatenate(
        [_dot(yc_prev, wc[...]) + _dot(ym_prev, wm[...])
         for wc, wm in zip((wc0_ref, wc1_ref, wc2_ref, wc3_ref),
                           (wm0_ref, wm1_ref, wm2_ref, wm3_ref))], axis=1)

    def recurrence(heads):
        q = {h: q_ref[:, cols[h]] * jnp.asarray(dh ** -0.5, BF16) for h in heads}
        k = {h: k_ref[:, cols[h]] for h in heads}
        v = {h: v_ref[:, cols[h]] for h in heads}
        i_col = {h: g[:, h:h + 1] for h in heads}
        b_col = {h: bcum[:, nh + h:nh + h + 1] for h in heads}
        i_row = {h: g_t[h:h + 1, :] for h in heads}
        b_row = {h: bcum_t[nh + h:nh + h + 1, :] for h in heads}
        g_tot = {h: b_col[h][chunk - 1:chunk, :] for h in heads}
        m_prev = {h: m_ref[h, 0:1, 0:1] for h in heads}
        c_prev = {h: c_ref[h] for h in heads}
        n_prev = {h: n_ref[h, 0:1, :] for h in heads}

        qk = {h: _dot_t(q[h], k[h]) for h in heads}
        inter = {h: _dot(q[h], c_prev[h].astype(BF16)) for h in heads}
        qn = {h: jnp.sum(q[h].astype(F32) * n_prev[h], axis=-1, keepdims=True) for h in heads}

        log_d = {h: jnp.where(causal, b_col[h] - (b_row[h] - i_row[h]), neg_inf) for h in heads}
        inter_log = {h: b_col[h] + m_prev[h] for h in heads}
        m_t = {h: jnp.maximum(inter_log[h], jnp.max(log_d[h], axis=-1, keepdims=True))
               for h in heads}
        scores = {h: qk[h] * jnp.exp(log_d[h] - m_t[h]) for h in heads}
        inter_w = {h: jnp.exp(inter_log[h] - m_t[h]) for h in heads}
        num = {h: _dot(scores[h].astype(BF16), v[h]) + inter_w[h] * inter[h] for h in heads}
        den = {h: jnp.sum(scores[h], axis=-1, keepdims=True) + inter_w[h] * qn[h] for h in heads}
        hh = {h: num[h] / jnp.maximum(jnp.abs(den[h]), jnp.exp(-m_t[h])) for h in heads}

        w_log = {h: g_tot[h] - b_col[h] + i_col[h] for h in heads}
        m_new = {h: jnp.maximum(g_tot[h] + m_prev[h], jnp.max(w_log[h], axis=0, keepdims=True))
                 for h in heads}
        decay = {h: jnp.exp(g_tot[h] + m_prev[h] - m_new[h]) for h in heads}
        kw = {h: k[h].astype(F32) * jnp.exp(w_log[h] - m_new[h]) for h in heads}
        for h in heads:
            c_ref[h] = decay[h] * c_prev[h] + lax.dot_general(
                kw[h].astype(BF16), v[h], (((0,), (0,)), ((), ())), preferred_element_type=F32)
            n_ref[h, 0:1, :] = decay[h] * n_prev[h] + jnp.sum(kw[h], axis=0, keepdims=True)
            m_ref[h] = jnp.broadcast_to(m_new[h], m_ref.shape[1:])

        return hh

    def head_norm_gate(heads, hh):
        mu = {h: jnp.mean(hh[h], axis=-1, keepdims=True) for h in heads}
        hc = {h: hh[h] - mu[h] for h in heads}
        var = {h: jnp.mean(hc[h] * hc[h], axis=-1, keepdims=True) for h in heads}
        for h in heads:
            hn = hc[h] * lax.rsqrt(var[h] + HEAD_NORM_EPS) * gain_ref[:, cols[h]]
            o = o_ref[:, cols[h]].astype(F32)
            ym_ref[:, cols[h]] = (jax.nn.sigmoid(o) * hn).astype(ym_ref.dtype)

    groups = [tuple(range(h0, min(h0 + HEAD_GROUP, nh))) for h0 in range(0, nh, HEAD_GROUP)]
    hhs = [recurrence(group) for group in groups]
    z = alpha * x_ref[...] + mix
    x1_ref[...] = _layer_norm(z, lng_ref[...], lnb_ref[...])
    for group, hh in zip(groups, hhs):
        head_norm_gate(group, hh)

    gate_terms(xbn_ref[...])


def _mlstm_outproj(qkvo, xb2d, w_gates_t_bf, b_gates_row, gain_row, x2d, y_conv, w_out_bf,
                   ln_g, ln_b, *, seq, nh, dh, alpha):
    m, d = x2d.shape
    w = nh * dh
    wc = y_conv.shape[1]
    assert wc + w == w_out_bf.shape[0] and wc == w
    chunk = MLSTM_CHUNK
    n_chunks = m // chunk
    last = n_chunks - 1

    def cur(s):
        return jnp.minimum(s, last)

    def prev(s):
        return jnp.maximum(s - 1, 0)

    def blk(kind):
        return pl.BlockSpec((chunk, w), lambda s: (cur(s), kind))

    const = pl.Buffered(1)
    kern = functools.partial(_mlstm_outproj_kernel, nh=nh, dh=dh, chunks_per_seq=seq // chunk,
                             alpha=alpha)
    return pl.pallas_call(
        kern,
        out_shape=jax.ShapeDtypeStruct((m, d), F32),
        grid=(n_chunks + 1,),
        in_specs=[blk(0), blk(1), blk(2), blk(3),
                  pl.BlockSpec((chunk, d), lambda s: (0, 0)),
                  pl.BlockSpec((chunk, d), lambda s: (jnp.minimum(s + 1, last), 0)),
                  pl.BlockSpec((LANES, d), lambda s: (0, 0)),
                  pl.BlockSpec((1, LANES), lambda s: (0, 0)),
                  pl.BlockSpec((1, w), lambda s: (0, 0)),
                  pl.BlockSpec((chunk, d), lambda s: (prev(s), 0)),
                  pl.BlockSpec((chunk, wc), lambda s: (prev(s), 0)),
                  *[pl.BlockSpec((wc, d // 4), lambda s, p=p: (0, p), pipeline_mode=const)
                    for p in range(4)],
                  *[pl.BlockSpec((w, d // 4), lambda s, p=p: (1, p), pipeline_mode=const)
                    for p in range(4)],
                  pl.BlockSpec((1, d), lambda s: (0, 0)),
                  pl.BlockSpec((1, d), lambda s: (0, 0))],
        out_specs=pl.BlockSpec((chunk, d), lambda s: (prev(s), 0)),
        scratch_shapes=[
            pltpu.VMEM((nh, dh, dh), F32),
            pltpu.VMEM((nh, SUBLANES, dh), F32),
            pltpu.VMEM((nh, SUBLANES, LANES), F32),
            pltpu.VMEM((2, chunk, LANES), F32),
            pltpu.VMEM((2, LANES, chunk), F32),
            pltpu.VMEM((chunk, w), BF16),
        ],
        compiler_params=_params(("arbitrary",)),
        name="mlstm_outproj_ln",
    )(qkvo, qkvo, qkvo, qkvo, xb2d, xb2d, w_gates_t_bf, b_gates_row, gain_row,
      x2d, y_conv, *([w_out_bf] * 8), ln_g, ln_b)


def _convffn_ln_kernel(x_ref, wup_hbm, cw_ref, cb_ref, wdn_hbm,
                       g_ref, b_ref, o_ref, xb_ref, acc_ref, carry_v_ref, carry_g_ref,
                       buf_v_ref, buf_g_ref, wv_ring, wg_ring, wd_ring, sems, *,
                       alpha, tiles_per_seq, chunk):
    i = pl.program_id(0)
    f = pl.program_id(1)
    nf = pl.num_programs(1)
    n_steps = pl.num_programs(0) * nf
    rows = x_ref.shape[0]
    tf = wv_ring.shape[2]
    first = (i % tiles_per_seq) == 0
    step_id = i * nf + f
    slot = step_id % WEIGHT_SLOTS

    def weight_copies(to_slot, blk):
        return (
            pltpu.make_async_copy(wup_hbm.at[blk], wv_ring.at[to_slot], sems.at[0, to_slot]),
            pltpu.make_async_copy(wup_hbm.at[nf + blk], wg_ring.at[to_slot], sems.at[1, to_slot]),
            pltpu.make_async_copy(wdn_hbm.at[pl.ds(pl.multiple_of(blk * tf, tf), tf), :],
                                  wd_ring.at[to_slot], sems.at[2, to_slot]),
        )

    @pl.when(step_id == 0)
    def _():
        for ahead in range(WEIGHT_SLOTS - 1):
            for copy in weight_copies(ahead, ahead % nf):
                copy.start()

    for copy in weight_copies(slot, f):
        copy.wait()

    @pl.when(step_id + (WEIGHT_SLOTS - 1) < n_steps)
    def _():
        ahead = step_id + (WEIGHT_SLOTS - 1)
        for copy in weight_copies(ahead % WEIGHT_SLOTS, ahead % nf):
            copy.start()

    wv_ref = wv_ring.at[slot]
    wg_ref = wg_ring.at[slot]
    wd_ref = wd_ring.at[slot]
    cols_v = pl.ds(pl.multiple_of(f * tf, tf), tf)
    cols_g = pl.ds(pl.multiple_of((nf + f) * tf, tf), tf)

    @pl.when(first)
    def _():
        carry_v_ref[f] = jnp.zeros(carry_v_ref.shape[1:], F32)
        carry_g_ref[f] = jnp.zeros(carry_g_ref.shape[1:], F32)

    def conv(buf_ref, cols, r0):
        h = buf_ref[SUBLANES + r0:SUBLANES + r0 + chunk, :]
        hm1 = buf_ref[SUBLANES - 1 + r0:SUBLANES - 1 + r0 + chunk, :]
        hm2 = buf_ref[SUBLANES - 2 + r0:SUBLANES - 2 + r0 + chunk, :]
        w = cw_ref[:, cols]
        return w[0:1, :] * hm2 + w[1:2, :] * hm1 + w[2:3, :] * h + cb_ref[:, cols]

    def step(is_first, is_last):
        row_starts = list(range(0, rows, chunk))
        buf_v_ref[0:SUBLANES, :] = carry_v_ref[f]
        buf_g_ref[0:SUBLANES, :] = carry_g_ref[f]
        for r0 in row_starts:
            if is_first:
                xb_ref[r0:r0 + chunk, :] = x_ref[r0:r0 + chunk, :].astype(BF16)
            xb = xb_ref[r0:r0 + chunk, :]
            buf_v_ref[SUBLANES + r0:SUBLANES + r0 + chunk, :] = _dot(xb, wv_ref[...])
            buf_g_ref[SUBLANES + r0:SUBLANES + r0 + chunk, :] = _dot(xb, wg_ref[...])
        for r0 in row_starts:
            rs = slice(r0, r0 + chunk)
            val = conv(buf_v_ref, cols_v, r0)
            gate = conv(buf_g_ref, cols_g, r0)
            act = (jax.nn.silu(gate) * val).astype(BF16)
            part = _dot(act, wd_ref[...])
            if is_first:
                acc_ref[rs, :] = part
            elif is_last:
                z = alpha * x_ref[rs, :] + (acc_ref[rs, :] + part)
                o_ref[rs, :] = _layer_norm(z, g_ref[...], b_ref[...])
            else:
                acc_ref[rs, :] += part
        carry_v_ref[f] = buf_v_ref[rows:rows + SUBLANES, :]
        carry_g_ref[f] = buf_g_ref[rows:rows + SUBLANES, :]

    pl.when(f == 0)(functools.partial(step, True, False))
    pl.when(jnp.logical_and(f > 0, f < nf - 1))(functools.partial(step, False, False))
    pl.when(f == nf - 1)(functools.partial(step, False, True))


def _convffn_ln(x2d, w_up_bf, w_ffn_conv, b_ffn_conv_row, w_down_bf, ln_g, ln_b, *,
                alpha, seq, d_ff):
    m, d = x2d.shape
    tm, tf = TM_FFN, TF_FFN
    nf = d_ff // tf
    kern = functools.partial(_convffn_ln_kernel, alpha=alpha, tiles_per_seq=seq // tm,
                             chunk=TR_FFN)
    return pl.pallas_call(
        kern,
        out_shape=jax.ShapeDtypeStruct((m, d), F32),
        grid=(m // tm, nf),
        in_specs=[
            pl.BlockSpec((tm, d), lambda i, f: (i, 0)),
            pl.BlockSpec(memory_space=pl.ANY),
            pl.BlockSpec((CONV_WIDTH, 2 * d_ff), lambda i, f: (0, 0)),
            pl.BlockSpec((1, 2 * d_ff), lambda i, f: (0, 0)),
            pl.BlockSpec(memory_space=pl.ANY),
            pl.BlockSpec((1, d), lambda i, f: (0, 0)),
            pl.BlockSpec((1, d), lambda i, f: (0, 0)),
        ],
        out_specs=pl.BlockSpec((tm, d), lambda i, f: (i, 0)),
        scratch_shapes=[
            pltpu.VMEM((tm, d), BF16),
            pltpu.VMEM((tm, d), F32),
            pltpu.VMEM((nf, SUBLANES, tf), F32),
            pltpu.VMEM((nf, SUBLANES, tf), F32),
            pltpu.VMEM((tm + SUBLANES, tf), F32),
            pltpu.VMEM((tm + SUBLANES, tf), F32),
            pltpu.VMEM((WEIGHT_SLOTS, d, tf), BF16),
            pltpu.VMEM((WEIGHT_SLOTS, d, tf), BF16),
            pltpu.VMEM((WEIGHT_SLOTS, tf, d), BF16),
            pltpu.SemaphoreType.DMA((3, WEIGHT_SLOTS)),
        ],
        compiler_params=_params(("arbitrary", "arbitrary")),
        name="convffn_ln",
    )(x2d, w_up_bf, w_ffn_conv, b_ffn_conv_row, w_down_bf, ln_g, ln_b)


def kernel(x, w_in, b_gates, w_sc_conv, mh_gain, w_out, ln1_g, ln1_b,
           w_up, w_ffn_conv, b_ffn_conv, w_down, ln2_g, ln2_b):
    batch, seq, d = x.shape
    depth = w_in.shape[0]
    nh = N_HEADS_M
    w_conv = w_sc_conv.shape[-1]
    w_mlstm = mh_gain.shape[-1]
    dh = w_mlstm // nh
    d_ff = w_down.shape[1]
    n_gates = 2 * nh
    gate_col0 = 3 * w_conv + 4 * w_mlstm
    assert w_in.shape[-1] == gate_col0 + n_gates and w_conv + w_mlstm == d
    assert seq % TM_CONV == 0 and seq % TM_FFN == 0 and seq % MLSTM_CHUNK == 0
    assert w_conv % TC_CONV == 0 and d_ff % TF_FFN == 0
    assert (3 * w_conv) % TN_QKVO == 0 and (4 * w_mlstm) % TN_QKVO == 0
    assert (batch * seq) % TM_QKVO == 0
    alpha = (2 * depth) ** 0.25

    x2d = x.reshape(batch * seq, d)
    for l in range(depth):
        w_in_t = w_in[l].T
        w_gates_t_bf = jnp.pad(w_in_t[gate_col0:], ((0, LANES - n_gates), (0, 0))).astype(BF16)
        b_gates_row = jnp.pad(b_gates[l], (0, LANES - n_gates)).reshape(1, LANES)

        y_conv, xb2d, w_down_bf, w_out_bf = _inproj_conv(
            x2d, w_in_t, w_sc_conv[l], w_down[l], w_out[l], seq=seq, w_conv=w_conv)
        qkvo, w_up_bf = _inproj_qkvo(xb2d, w_in_t, w_up[l], col0=3 * w_conv, width=4 * w_mlstm,
                                     tf=TF_FFN)
        x2d = _mlstm_outproj(qkvo, xb2d, w_gates_t_bf, b_gates_row, mh_gain[l].reshape(1, w_mlstm),
                             x2d, y_conv, w_out_bf, ln1_g[l].reshape(1, d), ln1_b[l].reshape(1, d),
                             seq=seq, nh=nh, dh=dh, alpha=alpha)
        x2d = _convffn_ln(x2d, w_up_bf, w_ffn_conv[l],
                          b_ffn_conv[l].reshape(1, 2 * d_ff), w_down_bf,
                          ln2_g[l].reshape(1, d), ln2_b[l].reshape(1, d),
                          alpha=alpha, seq=seq, d_ff=d_ff)
    return x2d.reshape(batch, seq, d)
```

```python
import functools

import jax
import jax.numpy as jnp
from jax import lax
from jax.experimental import pallas as pl
from jax.experimental.pallas import tpu as pltpu

F32 = jnp.float32
BF16 = jnp.bfloat16

N_HEADS_M = 4
CONV_WIDTH = 3
LN_EPS = 1e-5
HEAD_NORM_EPS = 1e-6

SUBLANES = 8
LANES = 128
BF16_SUBLANES = 16
VMEM_LIMIT_BYTES = 56 * 1024 * 1024

TM_CONV = 512
TC_CONV = 512
TR_CONV = 256
TM_QKVO = 1024
TN_QKVO = 1024
MLSTM_CHUNK = 256
HEAD_GROUP = 4
TM_FFN = 512
TF_FFN = 512
TR_FFN = 256
WEIGHT_SLOTS = 3


def _dot(a, b):
    return jnp.dot(a, b, preferred_element_type=F32)


def _dot_t(a, b_t):
    return lax.dot_general(a, b_t, (((1,), (1,)), ((), ())), preferred_element_type=F32)


def _params(semantics, flags=None):
    return pltpu.CompilerParams(dimension_semantics=semantics,
                                vmem_limit_bytes=VMEM_LIMIT_BYTES, flags=flags)


def _cast_rows_spec(rows, cols, n_steps, step_of):
    assert rows % n_steps == 0 and (rows // n_steps) % BF16_SUBLANES == 0
    return pl.BlockSpec((rows // n_steps, cols), lambda *ids: (step_of(*ids), 0))


def _inproj_conv_kernel(x_ref, wb_ref, wc_ref, wh_ref, cw_ref, wdn_ref, wout_ref,
                        y_ref, xb_ref, wdn_bf_ref, wout_bf_ref,
                        wbf_ref, carry_ref, buf_ref, *, tiles_per_seq, chunk):
    i = pl.program_id(1)
    rows = x_ref.shape[0]

    @pl.when(i == 0)
    def _():
        wbf_ref[0] = wb_ref[...].astype(BF16)
        wbf_ref[1] = wc_ref[...].astype(BF16)
        wbf_ref[2] = wh_ref[...].astype(BF16)

    @pl.when(pl.program_id(0) == 0)
    def _():
        xb_ref[...] = x_ref[...].astype(BF16)

    @pl.when(i % tiles_per_seq == 0)
    def _():
        carry_ref[...] = jnp.zeros(carry_ref.shape, F32)

    wdn_bf_ref[...] = wdn_ref[...].astype(BF16)
    wout_bf_ref[...] = wout_ref[...].astype(BF16)

    buf_ref[0:SUBLANES, :] = carry_ref[...]
    w = cw_ref[...]
    row_starts = list(range(0, rows, chunk))
    xbs = [x_ref[r0:r0 + chunk, :].astype(BF16) for r0 in row_starts]
    for r0, xb in zip(row_starts, xbs):
        u = _dot_t(xb, wbf_ref[1]) * _dot_t(xb, wbf_ref[2])
        buf_ref[SUBLANES + r0:SUBLANES + r0 + chunk, :] = u
    c_bs = [_dot_t(xb, wbf_ref[0]) for xb in xbs]
    for r0, c_b in zip(row_starts, c_bs):
        u = buf_ref[SUBLANES + r0:SUBLANES + r0 + chunk, :]
        um2 = buf_ref[SUBLANES - 2 + r0:SUBLANES - 2 + r0 + chunk, :]
        um1 = buf_ref[SUBLANES - 1 + r0:SUBLANES - 1 + r0 + chunk, :]
        y = w[0:1, :] * um2 + w[1:2, :] * um1 + w[2:3, :] * u
        y_ref[r0:r0 + chunk, :] = (c_b * y).astype(y_ref.dtype)
    carry_ref[...] = buf_ref[rows:rows + SUBLANES, :]


def _inproj_conv(x2d, w_in_t, w_sc_conv, w_down, w_out, *, seq, w_conv):
    m, d = x2d.shape
    tm, tc = TM_CONV, TC_CONV
    nj, ni = w_conv // tc, m // tm
    n_steps = nj * ni
    step_of = lambda j, i: j * ni + i
    once = pl.Buffered(1)
    kern = functools.partial(_inproj_conv_kernel, tiles_per_seq=seq // tm, chunk=TR_CONV)
    return pl.pallas_call(
        kern,
        out_shape=(jax.ShapeDtypeStruct((m, w_conv), BF16),
                   jax.ShapeDtypeStruct((m, d), BF16),
                   jax.ShapeDtypeStruct(w_down.shape, BF16),
                   jax.ShapeDtypeStruct(w_out.shape, BF16)),
        grid=(nj, ni),
        in_specs=[
            pl.BlockSpec((tm, d), lambda j, i: (i, 0)),
            pl.BlockSpec((tc, d), lambda j, i: (j, 0), pipeline_mode=once),
            pl.BlockSpec((tc, d), lambda j, i: (nj + j, 0), pipeline_mode=once),
            pl.BlockSpec((tc, d), lambda j, i: (2 * nj + j, 0), pipeline_mode=once),
            pl.BlockSpec((CONV_WIDTH, tc), lambda j, i: (0, j)),
            _cast_rows_spec(*w_down.shape, n_steps, step_of),
            _cast_rows_spec(*w_out.shape, n_steps, step_of),
        ],
        out_specs=(pl.BlockSpec((tm, tc), lambda j, i: (i, j)),
                   pl.BlockSpec((tm, d), lambda j, i: (jnp.where(j == 0, i, ni - 1), 0)),
                   _cast_rows_spec(*w_down.shape, n_steps, step_of),
                   _cast_rows_spec(*w_out.shape, n_steps, step_of)),
        scratch_shapes=[
            pltpu.VMEM((3, tc, d), BF16),
            pltpu.VMEM((SUBLANES, tc), F32),
            pltpu.VMEM((tm + SUBLANES, tc), F32),
        ],
        compiler_params=_params(("arbitrary", "arbitrary")),
        name="inproj_shortconv",
    )(x2d, w_in_t, w_in_t, w_in_t, w_sc_conv, w_down, w_out)


def _inproj_qkvo_kernel(xb_ref, w_ref, wup_ref, o_ref, wup_bf_ref, wbf_ref):
    @pl.when(pl.program_id(1) == 0)
    def _():
        wbf_ref[...] = w_ref[...].astype(BF16)

    n_blocks, _, tf = wup_bf_ref.shape
    for n in range(n_blocks):
        wup_bf_ref[n] = wup_ref[:, n * tf:(n + 1) * tf].astype(BF16)
    o_ref[...] = _dot_t(xb_ref[...], wbf_ref[...]).astype(o_ref.dtype)


def _inproj_qkvo(xb2d, w_in_t, w_up, *, col0, width, tf):
    m, d = xb2d.shape
    tm, tn = TM_QKVO, TN_QKVO
    j0 = col0 // tn
    nj, ni = width // tn, m // tm
    n_steps = nj * ni
    step_of = lambda j, i: j * ni + i
    up_rows, up_cols = w_up.shape
    n_blocks = up_cols // tf
    rows_per_step = up_rows // n_steps
    assert up_cols % tf == 0 and up_rows % n_steps == 0 and rows_per_step % BF16_SUBLANES == 0
    return pl.pallas_call(
        _inproj_qkvo_kernel,
        out_shape=(jax.ShapeDtypeStruct((m, width), BF16),
                   jax.ShapeDtypeStruct((n_blocks, up_rows, tf), BF16)),
        grid=(nj, ni),
        in_specs=[
            pl.BlockSpec((tm, d), lambda j, i: (i, 0)),
            pl.BlockSpec((tn, d), lambda j, i: (j0 + j, 0)),
            _cast_rows_spec(up_rows, up_cols, n_steps, step_of),
        ],
        out_specs=(pl.BlockSpec((tm, tn), lambda j, i: (i, j)),
                   pl.BlockSpec((n_blocks, rows_per_step, tf),
                                lambda j, i: (0, step_of(j, i), 0))),
        scratch_shapes=[pltpu.VMEM((tn, d), BF16)],
        compiler_params=_params(("arbitrary", "arbitrary")),
        name="inproj_mlstm",
    )(xb2d, w_in_t, w_up)


def _log_sigmoid(x):
    return jnp.minimum(x, 0.0) - jnp.log1p(jnp.exp(-jnp.abs(x)))


def _layer_norm(z, g, b):
    mu = jnp.mean(z, axis=-1, keepdims=True)
    zc = z - mu
    var = jnp.mean(zc * zc, axis=-1, keepdims=True)
    return zc * lax.rsqrt(var + LN_EPS) * g + b


def _mlstm_outproj_kernel(q_ref, k_ref, v_ref, o_ref, xb0_ref, xbn_ref, wg_ref, bg_ref, gain_ref,
                          x_ref, yc_ref, wc0_ref, wc1_ref, wc2_ref, wc3_ref,
                          wm0_ref, wm1_ref, wm2_ref, wm3_ref, lng_ref, lnb_ref, x1_ref,
                          c_ref, n_ref, m_ref, gates_ref, gates_t_ref, ym_ref, *,
                          nh, dh, chunks_per_seq, alpha):
    chunk = q_ref.shape[0]
    step = pl.program_id(0)

    @pl.when(step % chunks_per_seq == 0)
    def _():
        c_ref[...] = jnp.zeros(c_ref.shape, F32)
        n_ref[...] = jnp.zeros(n_ref.shape, F32)
        m_ref[...] = jnp.zeros(m_ref.shape, F32)

    row = lax.broadcasted_iota(jnp.int32, (chunk, chunk), 0)
    col = lax.broadcasted_iota(jnp.int32, (chunk, chunk), 1)
    causal = row >= col

    def gate_terms(xb):
        g = _dot_t(xb, wg_ref[...]) + bg_ref[...]
        lf = _log_sigmoid(g)
        bcum = jnp.dot(causal.astype(F32), lf, precision=lax.Precision.HIGHEST,
                       preferred_element_type=F32)
        gates_ref[0] = g
        gates_ref[1] = bcum
        gates_t_ref[0] = g.T
        gates_t_ref[1] = bcum.T

    @pl.when(step == 0)
    def _():
        gate_terms(xb0_ref[...])
        ym_ref[...] = jnp.zeros(ym_ref.shape, ym_ref.dtype)

    g = gates_ref[0]
    bcum = gates_ref[1]
    g_t = gates_t_ref[0]
    bcum_t = gates_t_ref[1]
    neg_inf = jnp.float32(-jnp.inf)
    cols = [slice(h * dh, (h + 1) * dh) for h in range(nh)]

    yc_prev = yc_ref[...]
    ym_prev = ym_ref[...]
    mix = jnp.concatenate(
        [_dot(yc_prev, wc[...]) + _dot(ym_prev, wm[...])
         for wc, wm in zip((wc0_ref, wc1_ref, wc2_ref, wc3_ref),
                           (wm0_ref, wm1_ref, wm2_ref, wm3_ref))], axis=1)

    def recurrence(heads):
        q = {h: q_ref[:, cols[h]] * jnp.asarray(dh ** -0.5, BF16) for h in heads}
        k = {h: k_ref[:, cols[h]] for h in heads}
        v = {h: v_ref[:, cols[h]] for h in heads}
        i_col = {h: g[:, h:h + 1] for h in heads}
        b_col = {h: bcum[:, nh + h:nh + h + 1] for h in heads}
        i_row = {h: g_t[h:h + 1, :] for h in heads}
        b_row = {h: bcum_t[nh + h:nh + h + 1, :] for h in heads}
        g_tot = {h: b_col[h][chunk - 1:chunk, :] for h in heads}
        m_prev = {h: m_ref[h, 0:1, 0:1] for h in heads}
        c_prev = {h: c_ref[h] for h in heads}
        n_prev = {h: n_ref[h, 0:1, :] for h in heads}

        qk = {h: _dot_t(q[h], k[h]) for h in heads}
        inter = {h: _dot(q[h], c_prev[h].astype(BF16)) for h in heads}
        qn = {h: jnp.sum(q[h].astype(F32) * n_prev[h], axis=-1, keepdims=True) for h in heads}

        log_d = {h: jnp.where(causal, b_col[h] - (b_row[h] - i_row[h]), neg_inf) for h in heads}
        inter_log = {h: b_col[h] + m_prev[h] for h in heads}
        m_t = {h: jnp.maximum(inter_log[h], jnp.max(log_d[h], axis=-1, keepdims=True))
               for h in heads}
        scores = {h: qk[h] * jnp.exp(log_d[h] - m_t[h]) for h in heads}
        inter_w = {h: jnp.exp(inter_log[h] - m_t[h]) for h in heads}
        num = {h: _dot(scores[h].astype(BF16), v[h]) + inter_w[h] * inter[h] for h in heads}
        den = {h: jnp.sum(scores[h], axis=-1, keepdims=True) + inter_w[h] * qn[h] for h in heads}
        hh = {h: num[h] / jnp.maximum(jnp.abs(den[h]), jnp.exp(-m_t[h])) for h in heads}

        w_log = {h: g_tot[h] - b_col[h] + i_col[h] for h in heads}
        m_new = {h: jnp.maximum(g_tot[h] + m_prev[h], jnp.max(w_log[h], axis=0, keepdims=True))
                 for h in heads}
        decay = {h: jnp.exp(g_tot[h] + m_prev[h] - m_new[h]) for h in heads}
        kw = {h: k[h].astype(F32) * jnp.exp(w_log[h] - m_new[h]) for h in heads}
        for h in heads:
            c_ref[h] = decay[h] * c_prev[h] + lax.dot_general(
                kw[h].astype(BF16), v[h], (((0,), (0,)), ((), ())), preferred_element_type=F32)
            n_ref[h, 0:1, :] = decay[h] * n_prev[h] + jnp.sum(kw[h], axis=0, keepdims=True)
            m_ref[h] = jnp.broadcast_to(m_new[h], m_ref.shape[1:])

        return hh

    def head_norm_gate(heads, hh):
        mu = {h: jnp.mean(hh[h], axis=-1, keepdims=True) for h in heads}
        hc = {h: hh[h] - mu[h] for h in heads}
        var = {h: jnp.mean(hc[h] * hc[h], axis=-1, keepdims=True) for h in heads}
        for h in heads:
            hn = hc[h] * lax.rsqrt(var[h] + HEAD_NORM_EPS) * gain_ref[:, cols[h]]
            o = o_ref[:, cols[h]].astype(F32)
            ym_ref[:, cols[h]] = (jax.nn.sigmoid(o) * hn).astype(ym_ref.dtype)

    groups = [tuple(range(h0, min(h0 + HEAD_GROUP, nh))) for h0 in range(0, nh, HEAD_GROUP)]
    hhs = [recurrence(group) for group in groups]
    z = alpha * x_ref[...] + mix
    x1_ref[...] = _layer_norm(z, lng_ref[...], lnb_ref[...])
    for group, hh in zip(groups, hhs):
        head_norm_gate(group, hh)

    gate_terms(xbn_ref[...])


def _mlstm_outproj(qkvo, xb2d, w_gates_t_bf, b_gates_row, gain_row, x2d, y_conv, w_out_bf,
                   ln_g, ln_b, *, seq, nh, dh, alpha):
    m, d = x2d.shape
    w = nh * dh
    wc = y_conv.shape[1]
    assert wc + w == w_out_bf.shape[0] and wc == w
    chunk = MLSTM_CHUNK
    n_chunks = m // chunk
    last = n_chunks - 1

    def cur(s):
        return jnp.minimum(s, last)

    def prev(s):
        return jnp.maximum(s - 1, 0)

    def blk(kind):
        return pl.BlockSpec((chunk, w), lambda s: (cur(s), kind))

    const = pl.Buffered(1)
    kern = functools.partial(_mlstm_outproj_kernel, nh=nh, dh=dh, chunks_per_seq=seq // chunk,
                             alpha=alpha)
    return pl.pallas_call(
        kern,
        out_shape=jax.ShapeDtypeStruct((m, d), F32),
        grid=(n_chunks + 1,),
        in_specs=[blk(0), blk(1), blk(2), blk(3),
                  pl.BlockSpec((chunk, d), lambda s: (0, 0)),
                  pl.BlockSpec((chunk, d), lambda s: (jnp.minimum(s + 1, last), 0)),
                  pl.BlockSpec((LANES, d), lambda s: (0, 0)),
                  pl.BlockSpec((1, LANES), lambda s: (0, 0)),
                  pl.BlockSpec((1, w), lambda s: (0, 0)),
                  pl.BlockSpec((chunk, d), lambda s: (prev(s), 0)),
                  pl.BlockSpec((chunk, wc), lambda s: (prev(s), 0)),
                  *[pl.BlockSpec((wc, d // 4), lambda s, p=p: (0, p), pipeline_mode=const)
                    for p in range(4)],
                  *[pl.BlockSpec((w, d // 4), lambda s, p=p: (1, p), pipeline_mode=const)
                    for p in range(4)],
                  pl.BlockSpec((1, d), lambda s: (0, 0)),
                  pl.BlockSpec((1, d), lambda s: (0, 0))],
        out_specs=pl.BlockSpec((chunk, d), lambda s: (prev(s), 0)),
        scratch_shapes=[
            pltpu.VMEM((nh, dh, dh), F32),
            pltpu.VMEM((nh, SUBLANES, dh), F32),
            pltpu.VMEM((nh, SUBLANES, LANES), F32),
            pltpu.VMEM((2, chunk, LANES), F32),
            pltpu.VMEM((2, LANES, chunk), F32),
            pltpu.VMEM((chunk, w), BF16),
        ],
        compiler_params=_params(("arbitrary",)),
        name="mlstm_outproj_ln",
    )(qkvo, qkvo, qkvo, qkvo, xb2d, xb2d, w_gates_t_bf, b_gates_row, gain_row,
      x2d, y_conv, *([w_out_bf] * 8), ln_g, ln_b)


def _convffn_ln_kernel(x_ref, wup_hbm, cw_ref, cb_ref, wdn_hbm,
                       g_ref, b_ref, o_ref, xb_ref, acc_ref, carry_v_ref, carry_g_ref,
                       buf_v_ref, buf_g_ref, wv_ring, wg_ring, wd_ring, sems, *,
                       alpha, tiles_per_seq, chunk, nf):
    i = pl.program_id(0)
    n_steps = pl.num_programs(0) * nf
    rows = x_ref.shape[0]
    tf = wv_ring.shape[2]
    first = (i % tiles_per_seq) == 0

    def weight_copies(to_slot, blk):
        return (
            pltpu.make_async_copy(wup_hbm.at[blk], wv_ring.at[to_slot], sems.at[0, to_slot]),
            pltpu.make_async_copy(wup_hbm.at[nf + blk], wg_ring.at[to_slot], sems.at[1, to_slot]),
            pltpu.make_async_copy(wdn_hbm.at[pl.ds(pl.multiple_of(blk * tf, tf), tf), :],
                                  wd_ring.at[to_slot], sems.at[2, to_slot]),
        )

    @pl.when(i == 0)
    def _():
        for ahead in range(WEIGHT_SLOTS - 1):
            for copy in weight_copies(ahead, ahead % nf):
                copy.start()

    def conv(buf_ref, cols, r0):
        h = buf_ref[SUBLANES + r0:SUBLANES + r0 + chunk, :]
        hm1 = buf_ref[SUBLANES - 1 + r0:SUBLANES - 1 + r0 + chunk, :]
        hm2 = buf_ref[SUBLANES - 2 + r0:SUBLANES - 2 + r0 + chunk, :]
        w = cw_ref[:, cols]
        return w[0:1, :] * hm2 + w[1:2, :] * hm1 + w[2:3, :] * h + cb_ref[:, cols]

    def block_step(f, is_first, is_last):
        step_id = i * nf + f
        slot = step_id % WEIGHT_SLOTS
        for copy in weight_copies(slot, f):
            copy.wait()

        @pl.when(step_id + (WEIGHT_SLOTS - 1) < n_steps)
        def _():
            ahead = step_id + (WEIGHT_SLOTS - 1)
            for copy in weight_copies(ahead % WEIGHT_SLOTS, ahead % nf):
                copy.start()

        wv_ref = wv_ring.at[slot]
        wg_ref = wg_ring.at[slot]
        wd_ref = wd_ring.at[slot]
        cols_v = pl.ds(pl.multiple_of(f * tf, tf), tf)
        cols_g = pl.ds(pl.multiple_of((nf + f) * tf, tf), tf)

        @pl.when(first)
        def _():
            carry_v_ref[f] = jnp.zeros(carry_v_ref.shape[1:], F32)
            carry_g_ref[f] = jnp.zeros(carry_g_ref.shape[1:], F32)

        row_starts = list(range(0, rows, chunk))
        buf_v_ref[0:SUBLANES, :] = carry_v_ref[f]
        buf_g_ref[0:SUBLANES, :] = carry_g_ref[f]
        for r0 in row_starts:
            if is_first:
                xb_ref[r0:r0 + chunk, :] = x_ref[r0:r0 + chunk, :].astype(BF16)
            xb = xb_ref[r0:r0 + chunk, :]
            buf_v_ref[SUBLANES + r0:SUBLANES + r0 + chunk, :] = _dot(xb, wv_ref[...])
            buf_g_ref[SUBLANES + r0:SUBLANES + r0 + chunk, :] = _dot(xb, wg_ref[...])
        for r0 in row_starts:
            rs = slice(r0, r0 + chunk)
            val = conv(buf_v_ref, cols_v, r0)
            gate = conv(buf_g_ref, cols_g, r0)
            act = (jax.nn.silu(gate) * val).astype(BF16)
            part = _dot(act, wd_ref[...])
            if is_first:
                acc_ref[rs, :] = part
            elif is_last:
                z = alpha * x_ref[rs, :] + (acc_ref[rs, :] + part)
                o_ref[rs, :] = _layer_norm(z, g_ref[...], b_ref[...])
            else:
                acc_ref[rs, :] += part
        carry_v_ref[f] = buf_v_ref[rows:rows + SUBLANES, :]
        carry_g_ref[f] = buf_g_ref[rows:rows + SUBLANES, :]

    block_step(0, True, False)

    @pl.loop(1, nf - 1)
    def _(f):
        block_step(f, False, False)

    block_step(nf - 1, False, True)


def _convffn_ln(x2d, w_up_bf, w_ffn_conv, b_ffn_conv_row, w_down_bf, ln_g, ln_b, *,
                alpha, seq, d_ff):
    m, d = x2d.shape
    tm, tf = TM_FFN, TF_FFN
    nf = d_ff // tf
    kern = functools.partial(_convffn_ln_kernel, alpha=alpha, tiles_per_seq=seq // tm,
                             chunk=TR_FFN, nf=nf)
    return pl.pallas_call(
        kern,
        out_shape=jax.ShapeDtypeStruct((m, d), F32),
        grid=(m // tm,),
        in_specs=[
            pl.BlockSpec((tm, d), lambda i: (i, 0)),
            pl.BlockSpec(memory_space=pl.ANY),
            pl.BlockSpec((CONV_WIDTH, 2 * d_ff), lambda i: (0, 0)),
            pl.BlockSpec((1, 2 * d_ff), lambda i: (0, 0)),
            pl.BlockSpec(memory_space=pl.ANY),
            pl.BlockSpec((1, d), lambda i: (0, 0)),
            pl.BlockSpec((1, d), lambda i: (0, 0)),
        ],
        out_specs=pl.BlockSpec((tm, d), lambda i: (i, 0)),
        scratch_shapes=[
            pltpu.VMEM((tm, d), BF16),
            pltpu.VMEM((tm, d), F32),
            pltpu.VMEM((nf, SUBLANES, tf), F32),
            pltpu.VMEM((nf, SUBLANES, tf), F32),
            pltpu.VMEM((tm + SUBLANES, tf), F32),
            pltpu.VMEM((tm + SUBLANES, tf), F32),
            pltpu.VMEM((WEIGHT_SLOTS, d, tf), BF16),
            pltpu.VMEM((WEIGHT_SLOTS, d, tf), BF16),
            pltpu.VMEM((WEIGHT_SLOTS, tf, d), BF16),
            pltpu.SemaphoreType.DMA((3, WEIGHT_SLOTS)),
        ],
        compiler_params=_params(("arbitrary",)),
        name="convffn_ln",
    )(x2d, w_up_bf, w_ffn_conv, b_ffn_conv_row, w_down_bf, ln_g, ln_b)


def kernel(x, w_in, b_gates, w_sc_conv, mh_gain, w_out, ln1_g, ln1_b,
           w_up, w_ffn_conv, b_ffn_conv, w_down, ln2_g, ln2_b):
    batch, seq, d = x.shape
    depth = w_in.shape[0]
    nh = N_HEADS_M
    w_conv = w_sc_conv.shape[-1]
    w_mlstm = mh_gain.shape[-1]
    dh = w_mlstm // nh
    d_ff = w_down.shape[1]
    n_gates = 2 * nh
    gate_col0 = 3 * w_conv + 4 * w_mlstm
    assert w_in.shape[-1] == gate_col0 + n_gates and w_conv + w_mlstm == d
    assert seq % TM_CONV == 0 and seq % TM_FFN == 0 and seq % MLSTM_CHUNK == 0
    assert w_conv % TC_CONV == 0 and d_ff % TF_FFN == 0
    assert (3 * w_conv) % TN_QKVO == 0 and (4 * w_mlstm) % TN_QKVO == 0
    assert (batch * seq) % TM_QKVO == 0
    alpha = (2 * depth) ** 0.25

    x2d = x.reshape(batch * seq, d)
    for l in range(depth):
        w_in_t = w_in[l].T
        w_gates_t_bf = jnp.pad(w_in_t[gate_col0:], ((0, LANES - n_gates), (0, 0))).astype(BF16)
        b_gates_row = jnp.pad(b_gates[l], (0, LANES - n_gates)).reshape(1, LANES)

        y_conv, xb2d, w_down_bf, w_out_bf = _inproj_conv(
            x2d, w_in_t, w_sc_conv[l], w_down[l], w_out[l], seq=seq, w_conv=w_conv)
        qkvo, w_up_bf = _inproj_qkvo(xb2d, w_in_t, w_up[l], col0=3 * w_conv, width=4 * w_mlstm,
                                     tf=TF_FFN)
        x2d = _mlstm_outproj(qkvo, xb2d, w_gates_t_bf, b_gates_row, mh_gain[l].reshape(1, w_mlstm),
                             x2d, y_conv, w_out_bf, ln1_g[l].reshape(1, d), ln1_b[l].reshape(1, d),
                             seq=seq, nh=nh, dh=dh, alpha=alpha)
        x2d = _convffn_ln(x2d, w_up_bf, w_ffn_conv[l],
                          b_ffn_conv[l].reshape(1, 2 * d_ff), w_down_bf,
                          ln2_g[l].reshape(1, d), ln2_b[l].reshape(1, d),
                          alpha=alpha, seq=seq, d_ff=d_ff)
    return x2d.reshape(batch, seq, d)
```

```python
import functools

import jax
import jax.numpy as jnp
from jax import lax
from jax.experimental import pallas as pl
from jax.experimental.pallas import tpu as pltpu

F32 = jnp.float32
BF16 = jnp.bfloat16

N_HEADS_M = 4
CONV_WIDTH = 3
LN_EPS = 1e-5
HEAD_NORM_EPS = 1e-6

SUBLANES = 8
LANES = 128
BF16_SUBLANES = 16
VMEM_LIMIT_BYTES = 56 * 1024 * 1024

TM_CONV = 512
TC_CONV = 512
TR_CONV = 256
TM_QKVO = 1024
TN_QKVO = 1024
MLSTM_CHUNK = 256
HEAD_GROUP = 4
OUT_COL_BLOCK = 512
TM_FFN = 512
TF_FFN = 512
TR_FFN = 256
WEIGHT_SLOTS = 3
N_WEIGHT_STREAMS = 3


def _dot(a, b):
    return jnp.dot(a, b, preferred_element_type=F32)


def _dot_t(a, b_t):
    return lax.dot_general(a, b_t, (((1,), (1,)), ((), ())), preferred_element_type=F32)


def _params(semantics):
    return pltpu.CompilerParams(dimension_semantics=semantics,
                                vmem_limit_bytes=VMEM_LIMIT_BYTES)


def _cast_rows_spec(rows, cols, n_steps, step_of):
    assert rows % n_steps == 0 and (rows // n_steps) % BF16_SUBLANES == 0
    return pl.BlockSpec((rows // n_steps, cols), lambda *ids: (step_of(*ids), 0))


def _inproj_conv_kernel(x_ref, wb_ref, wc_ref, wh_ref, cw_ref, wdn_ref, wout_ref,
                        y_ref, xb_ref, wdn_bf_ref, wout_bf_ref,
                        wbf_ref, carry_ref, buf_ref, *, tiles_per_seq, chunk):
    i = pl.program_id(1)
    rows = x_ref.shape[0]

    @pl.when(i == 0)
    def _():
        wbf_ref[0] = wb_ref[...].astype(BF16)
        wbf_ref[1] = wc_ref[...].astype(BF16)
        wbf_ref[2] = wh_ref[...].astype(BF16)

    @pl.when(pl.program_id(0) == 0)
    def _():
        xb_ref[...] = x_ref[...].astype(BF16)

    @pl.when(i % tiles_per_seq == 0)
    def _():
        carry_ref[...] = jnp.zeros(carry_ref.shape, F32)

    wdn_bf_ref[...] = wdn_ref[...].astype(BF16)
    wout_bf_ref[...] = wout_ref[...].astype(BF16)

    buf_ref[0:SUBLANES, :] = carry_ref[...]
    w = cw_ref[...]
    row_starts = list(range(0, rows, chunk))
    xbs = [x_ref[r0:r0 + chunk, :].astype(BF16) for r0 in row_starts]
    for r0, xb in zip(row_starts, xbs):
        u = _dot_t(xb, wbf_ref[1]) * _dot_t(xb, wbf_ref[2])
        buf_ref[SUBLANES + r0:SUBLANES + r0 + chunk, :] = u
    c_bs = [_dot_t(xb, wbf_ref[0]) for xb in xbs]
    for r0, c_b in zip(row_starts, c_bs):
        u = buf_ref[SUBLANES + r0:SUBLANES + r0 + chunk, :]
        um2 = buf_ref[SUBLANES - 2 + r0:SUBLANES - 2 + r0 + chunk, :]
        um1 = buf_ref[SUBLANES - 1 + r0:SUBLANES - 1 + r0 + chunk, :]
        y = w[0:1, :] * um2 + w[1:2, :] * um1 + w[2:3, :] * u
        y_ref[r0:r0 + chunk, :] = (c_b * y).astype(y_ref.dtype)
    carry_ref[...] = buf_ref[rows:rows + SUBLANES, :]


def _inproj_conv(x2d, w_in_t, w_sc_conv, w_down, w_out, *, seq, w_conv):
    m, d = x2d.shape
    tm, tc = TM_CONV, TC_CONV
    nj, ni = w_conv // tc, m // tm
    n_steps = nj * ni
    step_of = lambda j, i: j * ni + i
    once = pl.Buffered(1)
    kern = functools.partial(_inproj_conv_kernel, tiles_per_seq=seq // tm, chunk=TR_CONV)
    return pl.pallas_call(
        kern,
        out_shape=(jax.ShapeDtypeStruct((m, w_conv), BF16),
                   jax.ShapeDtypeStruct((m, d), BF16),
                   jax.ShapeDtypeStruct(w_down.shape, BF16),
                   jax.ShapeDtypeStruct(w_out.shape, BF16)),
        grid=(nj, ni),
        in_specs=[
            pl.BlockSpec((tm, d), lambda j, i: (i, 0)),
            pl.BlockSpec((tc, d), lambda j, i: (j, 0), pipeline_mode=once),
            pl.BlockSpec((tc, d), lambda j, i: (nj + j, 0), pipeline_mode=once),
            pl.BlockSpec((tc, d), lambda j, i: (2 * nj + j, 0), pipeline_mode=once),
            pl.BlockSpec((CONV_WIDTH, tc), lambda j, i: (0, j)),
            _cast_rows_spec(*w_down.shape, n_steps, step_of),
            _cast_rows_spec(*w_out.shape, n_steps, step_of),
        ],
        out_specs=(pl.BlockSpec((tm, tc), lambda j, i: (i, j)),
                   pl.BlockSpec((tm, d), lambda j, i: (jnp.where(j == 0, i, ni - 1), 0)),
                   _cast_rows_spec(*w_down.shape, n_steps, step_of),
                   _cast_rows_spec(*w_out.shape, n_steps, step_of)),
        scratch_shapes=[
            pltpu.VMEM((3, tc, d), BF16),
            pltpu.VMEM((SUBLANES, tc), F32),
            pltpu.VMEM((tm + SUBLANES, tc), F32),
        ],
        compiler_params=_params(("arbitrary", "arbitrary")),
        name="inproj_shortconv",
    )(x2d, w_in_t, w_in_t, w_in_t, w_sc_conv, w_down, w_out)


def _inproj_qkvo_kernel(xb_ref, w_ref, wup_ref, o_ref, wup_bf_ref, wbf_ref):
    @pl.when(pl.program_id(1) == 0)
    def _():
        wbf_ref[...] = w_ref[...].astype(BF16)

    n_blocks, _, tf = wup_bf_ref.shape
    for n in range(n_blocks):
        wup_bf_ref[n] = wup_ref[:, n * tf:(n + 1) * tf].astype(BF16)
    o_ref[...] = _dot_t(xb_ref[...], wbf_ref[...]).astype(o_ref.dtype)


def _inproj_qkvo(xb2d, w_in_t, w_up, *, col0, width, tf):
    m, d = xb2d.shape
    tm, tn = TM_QKVO, TN_QKVO
    j0 = col0 // tn
    nj, ni = width // tn, m // tm
    n_steps = nj * ni
    step_of = lambda j, i: j * ni + i
    up_rows, up_cols = w_up.shape
    n_blocks = up_cols // tf
    rows_per_step = up_rows // n_steps
    assert up_cols % tf == 0 and up_rows % n_steps == 0 and rows_per_step % BF16_SUBLANES == 0
    return pl.pallas_call(
        _inproj_qkvo_kernel,
        out_shape=(jax.ShapeDtypeStruct((m, width), BF16),
                   jax.ShapeDtypeStruct((n_blocks, up_rows, tf), BF16)),
        grid=(nj, ni),
        in_specs=[
            pl.BlockSpec((tm, d), lambda j, i: (i, 0)),
            pl.BlockSpec((tn, d), lambda j, i: (j0 + j, 0)),
            _cast_rows_spec(up_rows, up_cols, n_steps, step_of),
        ],
        out_specs=(pl.BlockSpec((tm, tn), lambda j, i: (i, j)),
                   pl.BlockSpec((n_blocks, rows_per_step, tf),
                                lambda j, i: (0, step_of(j, i), 0))),
        scratch_shapes=[pltpu.VMEM((tn, d), BF16)],
        compiler_params=_params(("arbitrary", "arbitrary")),
        name="inproj_mlstm",
    )(xb2d, w_in_t, w_up)


def _log_sigmoid(x):
    return jnp.minimum(x, 0.0) - jnp.log1p(jnp.exp(-jnp.abs(x)))


def _layer_norm(z, g, b):
    mu = jnp.mean(z, axis=-1, keepdims=True)
    zc = z - mu
    var = jnp.mean(zc * zc, axis=-1, keepdims=True)
    return zc * lax.rsqrt(var + LN_EPS) * g + b


def _mlstm_outproj_kernel(*refs, nh, dh, chunks_per_seq, alpha, n_split):
    (q_ref, k_ref, v_ref, o_ref, xb0_ref, xbn_ref, wg_ref, bg_ref, gain_ref,
     x_ref, yc_ref) = refs[:11]
    wc_refs = refs[11:11 + n_split]
    wm_refs = refs[11 + n_split:11 + 2 * n_split]
    (lng_ref, lnb_ref, x1_ref,
     c_ref, n_ref, m_ref, gates_ref, gates_t_ref, ym_ref) = refs[11 + 2 * n_split:]
    chunk = q_ref.shape[0]
    step = pl.program_id(0)

    @pl.when(step % chunks_per_seq == 0)
    def _():
        c_ref[...] = jnp.zeros(c_ref.shape, F32)
        n_ref[...] = jnp.zeros(n_ref.shape, F32)
        m_ref[...] = jnp.zeros(m_ref.shape, F32)

    row = lax.broadcasted_iota(jnp.int32, (chunk, chunk), 0)
    col = lax.broadcasted_iota(jnp.int32, (chunk, chunk), 1)
    causal = row >= col

    def gate_terms(xb):
        g = _dot_t(xb, wg_ref[...]) + bg_ref[...]
        lf = _log_sigmoid(g)
        bcum = jnp.dot(causal.astype(F32), lf, precision=lax.Precision.HIGHEST,
                       preferred_element_type=F32)
        gates_ref[0] = g
        gates_ref[1] = bcum
        gates_t_ref[0] = g.T
        gates_t_ref[1] = bcum.T

    @pl.when(step == 0)
    def _():
        gate_terms(xb0_ref[...])
        ym_ref[...] = jnp.zeros(ym_ref.shape, ym_ref.dtype)

    g = gates_ref[0]
    bcum = gates_ref[1]
    g_t = gates_t_ref[0]
    bcum_t = gates_t_ref[1]
    neg_inf = jnp.float32(-jnp.inf)
    cols = [slice(h * dh, (h + 1) * dh) for h in range(nh)]

    yc_prev = yc_ref[...]
    ym_prev = ym_ref[...]
    mix = jnp.concatenate(
        [_dot(yc_prev, wc[...]) + _dot(ym_prev, wm[...])
         for wc, wm in zip(wc_refs, wm_refs)], axis=1)

    def recurrence(heads):
        q = {h: q_ref[:, cols[h]] * jnp.asarray(dh ** -0.5, BF16) for h in heads}
        k = {h: k_ref[:, cols[h]] for h in heads}
        v = {h: v_ref[:, cols[h]] for h in heads}
        i_col = {h: g[:, h:h + 1] for h in heads}
        b_col = {h: bcum[:, nh + h:nh + h + 1] for h in heads}
        i_row = {h: g_t[h:h + 1, :] for h in heads}
        b_row = {h: bcum_t[nh + h:nh + h + 1, :] for h in heads}
        g_tot = {h: b_col[h][chunk - 1:chunk, :] for h in heads}
        m_prev = {h: m_ref[h, 0:1, 0:1] for h in heads}
        c_prev = {h: c_ref[h] for h in heads}
        n_prev = {h: n_ref[h, 0:1, :] for h in heads}

        qk = {h: _dot_t(q[h], k[h]) for h in heads}
        inter = {h: _dot(q[h], c_prev[h].astype(BF16)) for h in heads}
        qn = {h: jnp.sum(q[h].astype(F32) * n_prev[h], axis=-1, keepdims=True) for h in heads}

        log_d = {h: jnp.where(causal, b_col[h] - (b_row[h] - i_row[h]), neg_inf) for h in heads}
        inter_log = {h: b_col[h] + m_prev[h] for h in heads}
        m_t = {h: jnp.maximum(inter_log[h], jnp.max(log_d[h], axis=-1, keepdims=True))
               for h in heads}
        scores = {h: qk[h] * jnp.exp(log_d[h] - m_t[h]) for h in heads}
        inter_w = {h: jnp.exp(inter_log[h] - m_t[h]) for h in heads}
        num = {h: _dot(scores[h].astype(BF16), v[h]) + inter_w[h] * inter[h] for h in heads}
        den = {h: jnp.sum(scores[h], axis=-1, keepdims=True) + inter_w[h] * qn[h] for h in heads}
        hh = {h: num[h] / jnp.maximum(jnp.abs(den[h]), jnp.exp(-m_t[h])) for h in heads}

        w_log = {h: g_tot[h] - b_col[h] + i_col[h] for h in heads}
        m_new = {h: jnp.maximum(g_tot[h] + m_prev[h], jnp.max(w_log[h], axis=0, keepdims=True))
                 for h in heads}
        decay = {h: jnp.exp(g_tot[h] + m_prev[h] - m_new[h]) for h in heads}
        kw = {h: k[h].astype(F32) * jnp.exp(w_log[h] - m_new[h]) for h in heads}
        for h in heads:
            c_ref[h] = decay[h] * c_prev[h] + lax.dot_general(
                kw[h].astype(BF16), v[h], (((0,), (0,)), ((), ())), preferred_element_type=F32)
            n_ref[h, 0:1, :] = decay[h] * n_prev[h] + jnp.sum(kw[h], axis=0, keepdims=True)
            m_ref[h] = jnp.broadcast_to(m_new[h], m_ref.shape[1:])

        return hh

    def head_norm_gate(heads, hh):
        mu = {h: jnp.mean(hh[h], axis=-1, keepdims=True) for h in heads}
        hc = {h: hh[h] - mu[h] for h in heads}
        var = {h: jnp.mean(hc[h] * hc[h], axis=-1, keepdims=True) for h in heads}
        for h in heads:
            hn = hc[h] * lax.rsqrt(var[h] + HEAD_NORM_EPS) * gain_ref[:, cols[h]]
            o = o_ref[:, cols[h]].astype(F32)
            ym_ref[:, cols[h]] = (jax.nn.sigmoid(o) * hn).astype(ym_ref.dtype)

    groups = [tuple(range(h0, min(h0 + HEAD_GROUP, nh))) for h0 in range(0, nh, HEAD_GROUP)]
    hhs = [recurrence(group) for group in groups]
    z = alpha * x_ref[...] + mix
    x1_ref[...] = _layer_norm(z, lng_ref[...], lnb_ref[...])
    for group, hh in zip(groups, hhs):
        head_norm_gate(group, hh)

    gate_terms(xbn_ref[...])


def _mlstm_outproj(qkvo, xb2d, w_gates_t_bf, b_gates_row, gain_row, x2d, y_conv, w_out_bf,
                   ln_g, ln_b, *, seq, nh, dh, alpha):
    m, d = x2d.shape
    w = nh * dh
    wc = y_conv.shape[1]
    assert wc + w == w_out_bf.shape[0] and wc == w
    chunk = MLSTM_CHUNK
    n_chunks = m // chunk
    last = n_chunks - 1

    def cur(s):
        return jnp.minimum(s, last)

    def prev(s):
        return jnp.maximum(s - 1, 0)

    def blk(kind):
        return pl.BlockSpec((chunk, w), lambda s: (cur(s), kind))

    const = pl.Buffered(1)
    piece = OUT_COL_BLOCK
    assert d % piece == 0
    n_split = d // piece
    kern = functools.partial(_mlstm_outproj_kernel, nh=nh, dh=dh, chunks_per_seq=seq // chunk,
                             alpha=alpha, n_split=n_split)
    return pl.pallas_call(
        kern,
        out_shape=jax.ShapeDtypeStruct((m, d), F32),
        grid=(n_chunks + 1,),
        in_specs=[blk(0), blk(1), blk(2), blk(3),
                  pl.BlockSpec((chunk, d), lambda s: (0, 0)),
                  pl.BlockSpec((chunk, d), lambda s: (jnp.minimum(s + 1, last), 0)),
                  pl.BlockSpec((LANES, d), lambda s: (0, 0)),
                  pl.BlockSpec((1, LANES), lambda s: (0, 0)),
                  pl.BlockSpec((1, w), lambda s: (0, 0)),
                  pl.BlockSpec((chunk, d), lambda s: (prev(s), 0)),
                  pl.BlockSpec((chunk, wc), lambda s: (prev(s), 0)),
                  *[pl.BlockSpec((wc, piece), lambda s, p=p: (0, p), pipeline_mode=const)
                    for p in range(n_split)],
                  *[pl.BlockSpec((w, piece), lambda s, p=p: (1, p), pipeline_mode=const)
                    for p in range(n_split)],
                  pl.BlockSpec((1, d), lambda s: (0, 0)),
                  pl.BlockSpec((1, d), lambda s: (0, 0))],
        out_specs=pl.BlockSpec((chunk, d), lambda s: (prev(s), 0)),
        scratch_shapes=[
            pltpu.VMEM((nh, dh, dh), F32),
            pltpu.VMEM((nh, SUBLANES, dh), F32),
            pltpu.VMEM((nh, SUBLANES, LANES), F32),
            pltpu.VMEM((2, chunk, LANES), F32),
            pltpu.VMEM((2, LANES, chunk), F32),
            pltpu.VMEM((chunk, w), BF16),
        ],
        compiler_params=_params(("arbitrary",)),
        name="mlstm_outproj_ln",
    )(qkvo, qkvo, qkvo, qkvo, xb2d, xb2d, w_gates_t_bf, b_gates_row, gain_row,
      x2d, y_conv, *([w_out_bf] * (2 * n_split)), ln_g, ln_b)


def _convffn_ln_kernel(x_ref, wup_hbm, cw_ref, cb_ref, wdn_hbm,
                       g_ref, b_ref, o_ref, xb_ref, acc_ref, carry_v_ref, carry_g_ref,
                       buf_v_ref, buf_g_ref, wv_ring, wg_ring, wd_ring, sems, *,
                       alpha, tiles_per_seq, chunk, nf):
    i = pl.program_id(0)
    n_steps = pl.num_programs(0) * nf
    rows = x_ref.shape[0]
    tf = wv_ring.shape[2]
    first = (i % tiles_per_seq) == 0

    def weight_copies(to_slot, blk):
        return (
            pltpu.make_async_copy(wup_hbm.at[blk], wv_ring.at[to_slot], sems.at[0, to_slot]),
            pltpu.make_async_copy(wup_hbm.at[nf + blk], wg_ring.at[to_slot], sems.at[1, to_slot]),
            pltpu.make_async_copy(wdn_hbm.at[pl.ds(pl.multiple_of(blk * tf, tf), tf), :],
                                  wd_ring.at[to_slot], sems.at[2, to_slot]),
        )

    @pl.when(i == 0)
    def _():
        for ahead in range(WEIGHT_SLOTS - 1):
            for copy in weight_copies(ahead, ahead % nf):
                copy.start()

    def conv(buf_ref, cols, r0):
        h = buf_ref[SUBLANES + r0:SUBLANES + r0 + chunk, :]
        hm1 = buf_ref[SUBLANES - 1 + r0:SUBLANES - 1 + r0 + chunk, :]
        hm2 = buf_ref[SUBLANES - 2 + r0:SUBLANES - 2 + r0 + chunk, :]
        w = cw_ref[:, cols]
        return w[0:1, :] * hm2 + w[1:2, :] * hm1 + w[2:3, :] * h + cb_ref[:, cols]

    def block_step(f, is_first, is_last):
        step_id = i * nf + f
        slot = step_id % WEIGHT_SLOTS
        for copy in weight_copies(slot, f):
            copy.wait()

        @pl.when(step_id + (WEIGHT_SLOTS - 1) < n_steps)
        def _():
            ahead = step_id + (WEIGHT_SLOTS - 1)
            for copy in weight_copies(ahead % WEIGHT_SLOTS, ahead % nf):
                copy.start()

        wv_ref = wv_ring.at[slot]
        wg_ref = wg_ring.at[slot]
        wd_ref = wd_ring.at[slot]
        cols_v = pl.ds(pl.multiple_of(f * tf, tf), tf)
        cols_g = pl.ds(pl.multiple_of((nf + f) * tf, tf), tf)

        @pl.when(first)
        def _():
            carry_v_ref[f] = jnp.zeros(carry_v_ref.shape[1:], F32)
            carry_g_ref[f] = jnp.zeros(carry_g_ref.shape[1:], F32)

        row_starts = list(range(0, rows, chunk))
        buf_v_ref[0:SUBLANES, :] = carry_v_ref[f]
        buf_g_ref[0:SUBLANES, :] = carry_g_ref[f]
        for r0 in row_starts:
            if is_first:
                xb_ref[r0:r0 + chunk, :] = x_ref[r0:r0 + chunk, :].astype(BF16)
            xb = xb_ref[r0:r0 + chunk, :]
            buf_v_ref[SUBLANES + r0:SUBLANES + r0 + chunk, :] = _dot(xb, wv_ref[...])
            buf_g_ref[SUBLANES + r0:SUBLANES + r0 + chunk, :] = _dot(xb, wg_ref[...])
        for r0 in row_starts:
            rs = slice(r0, r0 + chunk)
            val = conv(buf_v_ref, cols_v, r0)
            gate = conv(buf_g_ref, cols_g, r0)
            act = (jax.nn.silu(gate) * val).astype(BF16)
            part = _dot(act, wd_ref[...])
            if is_first:
                acc_ref[rs, :] = part
            elif is_last:
                z = alpha * x_ref[rs, :] + (acc_ref[rs, :] + part)
                o_ref[rs, :] = _layer_norm(z, g_ref[...], b_ref[...])
            else:
                acc_ref[rs, :] += part
        carry_v_ref[f] = buf_v_ref[rows:rows + SUBLANES, :]
        carry_g_ref[f] = buf_g_ref[rows:rows + SUBLANES, :]

    block_step(0, True, False)

    @pl.loop(1, nf - 1)
    def _(f):
        block_step(f, False, False)

    block_step(nf - 1, False, True)


def _convffn_ln(x2d, w_up_bf, w_ffn_conv, b_ffn_conv_row, w_down_bf, ln_g, ln_b, *,
                alpha, seq, d_ff):
    m, d = x2d.shape
    tm, tf = TM_FFN, TF_FFN
    nf = d_ff // tf
    kern = functools.partial(_convffn_ln_kernel, alpha=alpha, tiles_per_seq=seq // tm,
                             chunk=TR_FFN, nf=nf)
    return pl.pallas_call(
        kern,
        out_shape=jax.ShapeDtypeStruct((m, d), F32),
        grid=(m // tm,),
        in_specs=[
            pl.BlockSpec((tm, d), lambda i: (i, 0)),
            pl.BlockSpec(memory_space=pl.ANY),
            pl.BlockSpec((CONV_WIDTH, 2 * d_ff), lambda i: (0, 0)),
            pl.BlockSpec((1, 2 * d_ff), lambda i: (0, 0)),
            pl.BlockSpec(memory_space=pl.ANY),
            pl.BlockSpec((1, d), lambda i: (0, 0)),
            pl.BlockSpec((1, d), lambda i: (0, 0)),
        ],
        out_specs=pl.BlockSpec((tm, d), lambda i: (i, 0)),
        scratch_shapes=[
            pltpu.VMEM((tm, d), BF16),
            pltpu.VMEM((tm, d), F32),
            pltpu.VMEM((nf, SUBLANES, tf), F32),
            pltpu.VMEM((nf, SUBLANES, tf), F32),
            pltpu.VMEM((tm + SUBLANES, tf), F32),
            pltpu.VMEM((tm + SUBLANES, tf), F32),
            pltpu.VMEM((WEIGHT_SLOTS, d, tf), BF16),
            pltpu.VMEM((WEIGHT_SLOTS, d, tf), BF16),
            pltpu.VMEM((WEIGHT_SLOTS, tf, d), BF16),
            pltpu.SemaphoreType.DMA((N_WEIGHT_STREAMS, WEIGHT_SLOTS)),
        ],
        compiler_params=_params(("arbitrary",)),
        name="convffn_ln",
    )(x2d, w_up_bf, w_ffn_conv, b_ffn_conv_row, w_down_bf, ln_g, ln_b)


def kernel(x, w_in, b_gates, w_sc_conv, mh_gain, w_out, ln1_g, ln1_b,
           w_up, w_ffn_conv, b_ffn_conv, w_down, ln2_g, ln2_b):
    batch, seq, d = x.shape
    depth = w_in.shape[0]
    nh = N_HEADS_M
    w_conv = w_sc_conv.shape[-1]
    w_mlstm = mh_gain.shape[-1]
    dh = w_mlstm // nh
    d_ff = w_down.shape[1]
    n_gates = 2 * nh
    gate_col0 = 3 * w_conv + 4 * w_mlstm
    assert w_in.shape[-1] == gate_col0 + n_gates and w_conv + w_mlstm == d
    assert seq % TM_CONV == 0 and seq % TM_FFN == 0 and seq % MLSTM_CHUNK == 0
    assert w_conv % TC_CONV == 0 and d_ff % TF_FFN == 0
    assert (3 * w_conv) % TN_QKVO == 0 and (4 * w_mlstm) % TN_QKVO == 0
    assert (batch * seq) % TM_QKVO == 0
    alpha = (2 * depth) ** 0.25

    x2d = x.reshape(batch * seq, d)
    for l in range(depth):
        w_in_t = w_in[l].T
        w_gates_t_bf = jnp.pad(w_in_t[gate_col0:], ((0, LANES - n_gates), (0, 0))).astype(BF16)
        b_gates_row = jnp.pad(b_gates[l], (0, LANES - n_gates)).reshape(1, LANES)

        y_conv, xb2d, w_down_bf, w_out_bf = _inproj_conv(
            x2d, w_in_t, w_sc_conv[l], w_down[l], w_out[l], seq=seq, w_conv=w_conv)
        qkvo, w_up_bf = _inproj_qkvo(xb2d, w_in_t, w_up[l], col0=3 * w_conv, width=4 * w_mlstm,
                                     tf=TF_FFN)
        x2d = _mlstm_outproj(qkvo, xb2d, w_gates_t_bf, b_gates_row, mh_gain[l].reshape(1, w_mlstm),
                             x2d, y_conv, w_out_bf, ln1_g[l].reshape(1, d), ln1_b[l].reshape(1, d),
                             seq=seq, nh=nh, dh=dh, alpha=alpha)
        x2d = _convffn_ln(x2d, w_up_bf, w_ffn_conv[l],
                          b_ffn_conv[l].reshape(1, 2 * d_ff), w_down_bf,
                          ln2_g[l].reshape(1, d), ln2_b[l].reshape(1, d),
                          alpha=alpha, seq=seq, d_ff=d_ff)
    return x2d.reshape(batch, seq, d)
```

```python
import functools

import jax
import jax.numpy as jnp
from jax import lax
from jax.experimental import pallas as pl
from jax.experimental.pallas import tpu as pltpu

F32 = jnp.float32
BF16 = jnp.bfloat16

N_HEADS_M = 4
CONV_WIDTH = 3
LN_EPS = 1e-5
HEAD_NORM_EPS = 1e-6

SUBLANES = 8
LANES = 128
BF16_SUBLANES = 16
VMEM_LIMIT_BYTES = 56 * 1024 * 1024

TM_CONV = 512
TC_CONV = 512
TR_CONV = 256
TM_QKVO = 1024
TN_QKVO = 1024
MLSTM_CHUNK = 256
HEAD_GROUP = 4
OUT_COL_BLOCK = 512
TM_FFN = 512
TF_FFN = 512
TR_FFN = 512
WEIGHT_SLOTS = 3
N_WEIGHT_STREAMS = 3


def _dot(a, b):
    return jnp.dot(a, b, preferred_element_type=F32)


def _dot_t(a, b_t):
    return lax.dot_general(a, b_t, (((1,), (1,)), ((), ())), preferred_element_type=F32)


def _params(semantics):
    return pltpu.CompilerParams(dimension_semantics=semantics,
                                vmem_limit_bytes=VMEM_LIMIT_BYTES)


def _cast_rows_spec(rows, cols, n_steps, step_of):
    assert rows % n_steps == 0 and (rows // n_steps) % BF16_SUBLANES == 0
    return pl.BlockSpec((rows // n_steps, cols), lambda *ids: (step_of(*ids), 0))


def _inproj_conv_kernel(x_ref, wb_ref, wc_ref, wh_ref, cw_ref, wdn_ref, wout_ref,
                        y_ref, xb_ref, wdn_bf_ref, wout_bf_ref,
                        wbf_ref, carry_ref, buf_ref, *, tiles_per_seq, chunk):
    i = pl.program_id(1)
    rows = x_ref.shape[0]

    @pl.when(i == 0)
    def _():
        wbf_ref[0] = wb_ref[...].astype(BF16)
        wbf_ref[1] = wc_ref[...].astype(BF16)
        wbf_ref[2] = wh_ref[...].astype(BF16)

    @pl.when(pl.program_id(0) == 0)
    def _():
        xb_ref[...] = x_ref[...].astype(BF16)

    @pl.when(i % tiles_per_seq == 0)
    def _():
        carry_ref[...] = jnp.zeros(carry_ref.shape, F32)

    wdn_bf_ref[...] = wdn_ref[...].astype(BF16)
    wout_bf_ref[...] = wout_ref[...].astype(BF16)

    buf_ref[0:SUBLANES, :] = carry_ref[...]
    w = cw_ref[...]
    row_starts = list(range(0, rows, chunk))
    xbs = [x_ref[r0:r0 + chunk, :].astype(BF16) for r0 in row_starts]
    for r0, xb in zip(row_starts, xbs):
        u = _dot_t(xb, wbf_ref[1]) * _dot_t(xb, wbf_ref[2])
        buf_ref[SUBLANES + r0:SUBLANES + r0 + chunk, :] = u
    c_bs = [_dot_t(xb, wbf_ref[0]) for xb in xbs]
    for r0, c_b in zip(row_starts, c_bs):
        u = buf_ref[SUBLANES + r0:SUBLANES + r0 + chunk, :]
        um2 = buf_ref[SUBLANES - 2 + r0:SUBLANES - 2 + r0 + chunk, :]
        um1 = buf_ref[SUBLANES - 1 + r0:SUBLANES - 1 + r0 + chunk, :]
        y = w[0:1, :] * um2 + w[1:2, :] * um1 + w[2:3, :] * u
        y_ref[r0:r0 + chunk, :] = (c_b * y).astype(y_ref.dtype)
    carry_ref[...] = buf_ref[rows:rows + SUBLANES, :]


def _inproj_conv(x2d, w_in_t, w_sc_conv, w_down, w_out, *, seq, w_conv):
    m, d = x2d.shape
    tm, tc = TM_CONV, TC_CONV
    nj, ni = w_conv // tc, m // tm
    n_steps = nj * ni
    step_of = lambda j, i: j * ni + i
    once = pl.Buffered(1)
    kern = functools.partial(_inproj_conv_kernel, tiles_per_seq=seq // tm, chunk=TR_CONV)
    return pl.pallas_call(
        kern,
        out_shape=(jax.ShapeDtypeStruct((m, w_conv), BF16),
                   jax.ShapeDtypeStruct((m, d), BF16),
                   jax.ShapeDtypeStruct(w_down.shape, BF16),
                   jax.ShapeDtypeStruct(w_out.shape, BF16)),
        grid=(nj, ni),
        in_specs=[
            pl.BlockSpec((tm, d), lambda j, i: (i, 0)),
            pl.BlockSpec((tc, d), lambda j, i: (j, 0), pipeline_mode=once),
            pl.BlockSpec((tc, d), lambda j, i: (nj + j, 0), pipeline_mode=once),
            pl.BlockSpec((tc, d), lambda j, i: (2 * nj + j, 0), pipeline_mode=once),
            pl.BlockSpec((CONV_WIDTH, tc), lambda j, i: (0, j)),
            _cast_rows_spec(*w_down.shape, n_steps, step_of),
            _cast_rows_spec(*w_out.shape, n_steps, step_of),
        ],
        out_specs=(pl.BlockSpec((tm, tc), lambda j, i: (i, j)),
                   pl.BlockSpec((tm, d), lambda j, i: (jnp.where(j == 0, i, ni - 1), 0)),
                   _cast_rows_spec(*w_down.shape, n_steps, step_of),
                   _cast_rows_spec(*w_out.shape, n_steps, step_of)),
        scratch_shapes=[
            pltpu.VMEM((3, tc, d), BF16),
            pltpu.VMEM((SUBLANES, tc), F32),
            pltpu.VMEM((tm + SUBLANES, tc), F32),
        ],
        compiler_params=_params(("arbitrary", "arbitrary")),
        name="inproj_shortconv",
    )(x2d, w_in_t, w_in_t, w_in_t, w_sc_conv, w_down, w_out)


def _inproj_qkvo_kernel(xb_ref, w_ref, wup_ref, o_ref, wup_bf_ref, wbf_ref):
    @pl.when(pl.program_id(1) == 0)
    def _():
        wbf_ref[...] = w_ref[...].astype(BF16)

    n_blocks, _, tf = wup_bf_ref.shape
    for n in range(n_blocks):
        wup_bf_ref[n] = wup_ref[:, n * tf:(n + 1) * tf].astype(BF16)
    o_ref[...] = _dot_t(xb_ref[...], wbf_ref[...]).astype(o_ref.dtype)


def _inproj_qkvo(xb2d, w_in_t, w_up, *, col0, width, tf):
    m, d = xb2d.shape
    tm, tn = TM_QKVO, TN_QKVO
    j0 = col0 // tn
    nj, ni = width // tn, m // tm
    n_steps = nj * ni
    step_of = lambda j, i: j * ni + i
    up_rows, up_cols = w_up.shape
    n_blocks = up_cols // tf
    rows_per_step = up_rows // n_steps
    assert up_cols % tf == 0 and up_rows % n_steps == 0 and rows_per_step % BF16_SUBLANES == 0
    return pl.pallas_call(
        _inproj_qkvo_kernel,
        out_shape=(jax.ShapeDtypeStruct((m, width), BF16),
                   jax.ShapeDtypeStruct((n_blocks, up_rows, tf), BF16)),
        grid=(nj, ni),
        in_specs=[
            pl.BlockSpec((tm, d), lambda j, i: (i, 0)),
            pl.BlockSpec((tn, d), lambda j, i: (j0 + j, 0)),
            _cast_rows_spec(up_rows, up_cols, n_steps, step_of),
        ],
        out_specs=(pl.BlockSpec((tm, tn), lambda j, i: (i, j)),
                   pl.BlockSpec((n_blocks, rows_per_step, tf),
                                lambda j, i: (0, step_of(j, i), 0))),
        scratch_shapes=[pltpu.VMEM((tn, d), BF16)],
        compiler_params=_params(("arbitrary", "arbitrary")),
        name="inproj_mlstm",
    )(xb2d, w_in_t, w_up)


def _log_sigmoid(x):
    return jnp.minimum(x, 0.0) - jnp.log1p(jnp.exp(-jnp.abs(x)))


def _layer_norm(z, g, b):
    mu = jnp.mean(z, axis=-1, keepdims=True)
    zc = z - mu
    var = jnp.mean(zc * zc, axis=-1, keepdims=True)
    return zc * lax.rsqrt(var + LN_EPS) * g + b


def _mlstm_outproj_kernel(*refs, nh, dh, chunks_per_seq, alpha, n_split):
    (q_ref, k_ref, v_ref, o_ref, xb0_ref, xbn_ref, wg_ref, bg_ref, gain_ref,
     x_ref, yc_ref) = refs[:11]
    wc_refs = refs[11:11 + n_split]
    wm_refs = refs[11 + n_split:11 + 2 * n_split]
    (lng_ref, lnb_ref, x1_ref,
     c_ref, n_ref, m_ref, gates_ref, gates_t_ref, ym_ref) = refs[11 + 2 * n_split:]
    chunk = q_ref.shape[0]
    step = pl.program_id(0)

    @pl.when(step % chunks_per_seq == 0)
    def _():
        c_ref[...] = jnp.zeros(c_ref.shape, F32)
        n_ref[...] = jnp.zeros(n_ref.shape, F32)
        m_ref[...] = jnp.zeros(m_ref.shape, F32)

    row = lax.broadcasted_iota(jnp.int32, (chunk, chunk), 0)
    col = lax.broadcasted_iota(jnp.int32, (chunk, chunk), 1)
    causal = row >= col

    def gate_terms(xb):
        g = _dot_t(xb, wg_ref[...]) + bg_ref[...]
        lf = _log_sigmoid(g)
        bcum = jnp.dot(causal.astype(F32), lf, precision=lax.Precision.HIGHEST,
                       preferred_element_type=F32)
        gates_ref[0] = g
        gates_ref[1] = bcum
        gates_t_ref[0] = g.T
        gates_t_ref[1] = bcum.T

    @pl.when(step == 0)
    def _():
        gate_terms(xb0_ref[...])
        ym_ref[...] = jnp.zeros(ym_ref.shape, ym_ref.dtype)

    g = gates_ref[0]
    bcum = gates_ref[1]
    g_t = gates_t_ref[0]
    bcum_t = gates_t_ref[1]
    neg_inf = jnp.float32(-jnp.inf)
    cols = [slice(h * dh, (h + 1) * dh) for h in range(nh)]

    yc_prev = yc_ref[...]
    ym_prev = ym_ref[...]
    mix = jnp.concatenate(
        [_dot(yc_prev, wc[...]) + _dot(ym_prev, wm[...])
         for wc, wm in zip(wc_refs, wm_refs)], axis=1)

    def recurrence(heads):
        q = {h: q_ref[:, cols[h]] * jnp.asarray(dh ** -0.5, BF16) for h in heads}
        k = {h: k_ref[:, cols[h]] for h in heads}
        v = {h: v_ref[:, cols[h]] for h in heads}
        i_col = {h: g[:, h:h + 1] for h in heads}
        b_col = {h: bcum[:, nh + h:nh + h + 1] for h in heads}
        i_row = {h: g_t[h:h + 1, :] for h in heads}
        b_row = {h: bcum_t[nh + h:nh + h + 1, :] for h in heads}
        g_tot = {h: b_col[h][chunk - 1:chunk, :] for h in heads}
        m_prev = {h: m_ref[h, 0:1, 0:1] for h in heads}
        c_prev = {h: c_ref[h] for h in heads}
        n_prev = {h: n_ref[h, 0:1, :] for h in heads}

        qk = {h: _dot_t(q[h], k[h]) for h in heads}
        inter = {h: _dot(q[h], c_prev[h].astype(BF16)) for h in heads}
        qn = {h: jnp.sum(q[h].astype(F32) * n_prev[h], axis=-1, keepdims=True) for h in heads}

        log_d = {h: jnp.where(causal, b_col[h] - (b_row[h] - i_row[h]), neg_inf) for h in heads}
        inter_log = {h: b_col[h] + m_prev[h] for h in heads}
        m_t = {h: jnp.maximum(inter_log[h], jnp.max(log_d[h], axis=-1, keepdims=True))
               for h in heads}
        scores = {h: qk[h] * jnp.exp(log_d[h] - m_t[h]) for h in heads}
        inter_w = {h: jnp.exp(inter_log[h] - m_t[h]) for h in heads}
        num = {h: _dot(scores[h].astype(BF16), v[h]) + inter_w[h] * inter[h] for h in heads}
        den = {h: jnp.sum(scores[h], axis=-1, keepdims=True) + inter_w[h] * qn[h] for h in heads}
        hh = {h: num[h] / jnp.maximum(jnp.abs(den[h]), jnp.exp(-m_t[h])) for h in heads}

        w_log = {h: g_tot[h] - b_col[h] + i_col[h] for h in heads}
        m_new = {h: jnp.maximum(g_tot[h] + m_prev[h], jnp.max(w_log[h], axis=0, keepdims=True))
                 for h in heads}
        decay = {h: jnp.exp(g_tot[h] + m_prev[h] - m_new[h]) for h in heads}
        kw = {h: k[h].astype(F32) * jnp.exp(w_log[h] - m_new[h]) for h in heads}
        for h in heads:
            c_ref[h] = decay[h] * c_prev[h] + lax.dot_general(
                kw[h].astype(BF16), v[h], (((0,), (0,)), ((), ())), preferred_element_type=F32)
            n_ref[h, 0:1, :] = decay[h] * n_prev[h] + jnp.sum(kw[h], axis=0, keepdims=True)
            m_ref[h] = jnp.broadcast_to(m_new[h], m_ref.shape[1:])

        return hh

    def head_norm_gate(heads, hh):
        mu = {h: jnp.mean(hh[h], axis=-1, keepdims=True) for h in heads}
        hc = {h: hh[h] - mu[h] for h in heads}
        var = {h: jnp.mean(hc[h] * hc[h], axis=-1, keepdims=True) for h in heads}
        for h in heads:
            hn = hc[h] * lax.rsqrt(var[h] + HEAD_NORM_EPS) * gain_ref[:, cols[h]]
            o = o_ref[:, cols[h]].astype(F32)
            ym_ref[:, cols[h]] = (jax.nn.sigmoid(o) * hn).astype(ym_ref.dtype)

    groups = [tuple(range(h0, min(h0 + HEAD_GROUP, nh))) for h0 in range(0, nh, HEAD_GROUP)]
    hhs = [recurrence(group) for group in groups]
    z = alpha * x_ref[...] + mix
    x1_ref[...] = _layer_norm(z, lng_ref[...], lnb_ref[...])
    for group, hh in zip(groups, hhs):
        head_norm_gate(group, hh)

    gate_terms(xbn_ref[...])


def _mlstm_outproj(qkvo, xb2d, w_gates_t_bf, b_gates_row, gain_row, x2d, y_conv, w_out_bf,
                   ln_g, ln_b, *, seq, nh, dh, alpha):
    m, d = x2d.shape
    w = nh * dh
    wc = y_conv.shape[1]
    assert wc + w == w_out_bf.shape[0] and wc == w
    chunk = MLSTM_CHUNK
    n_chunks = m // chunk
    last = n_chunks - 1

    def cur(s):
        return jnp.minimum(s, last)

    def prev(s):
        return jnp.maximum(s - 1, 0)

    def blk(kind):
        return pl.BlockSpec((chunk, w), lambda s: (cur(s), kind))

    const = pl.Buffered(1)
    piece = OUT_COL_BLOCK
    assert d % piece == 0
    n_split = d // piece
    kern = functools.partial(_mlstm_outproj_kernel, nh=nh, dh=dh, chunks_per_seq=seq // chunk,
                             alpha=alpha, n_split=n_split)
    return pl.pallas_call(
        kern,
        out_shape=jax.ShapeDtypeStruct((m, d), F32),
        grid=(n_chunks + 1,),
        in_specs=[blk(0), blk(1), blk(2), blk(3),
                  pl.BlockSpec((chunk, d), lambda s: (0, 0)),
                  pl.BlockSpec((chunk, d), lambda s: (jnp.minimum(s + 1, last), 0)),
                  pl.BlockSpec((LANES, d), lambda s: (0, 0)),
                  pl.BlockSpec((1, LANES), lambda s: (0, 0)),
                  pl.BlockSpec((1, w), lambda s: (0, 0)),
                  pl.BlockSpec((chunk, d), lambda s: (prev(s), 0)),
                  pl.BlockSpec((chunk, wc), lambda s: (prev(s), 0)),
                  *[pl.BlockSpec((wc, piece), lambda s, p=p: (0, p), pipeline_mode=const)
                    for p in range(n_split)],
                  *[pl.BlockSpec((w, piece), lambda s, p=p: (1, p), pipeline_mode=const)
                    for p in range(n_split)],
                  pl.BlockSpec((1, d), lambda s: (0, 0)),
                  pl.BlockSpec((1, d), lambda s: (0, 0))],
        out_specs=pl.BlockSpec((chunk, d), lambda s: (prev(s), 0)),
        scratch_shapes=[
            pltpu.VMEM((nh, dh, dh), F32),
            pltpu.VMEM((nh, SUBLANES, dh), F32),
            pltpu.VMEM((nh, SUBLANES, LANES), F32),
            pltpu.VMEM((2, chunk, LANES), F32),
            pltpu.VMEM((2, LANES, chunk), F32),
            pltpu.VMEM((chunk, w), BF16),
        ],
        compiler_params=_params(("arbitrary",)),
        name="mlstm_outproj_ln",
    )(qkvo, qkvo, qkvo, qkvo, xb2d, xb2d, w_gates_t_bf, b_gates_row, gain_row,
      x2d, y_conv, *([w_out_bf] * (2 * n_split)), ln_g, ln_b)


def _convffn_ln_kernel(x_ref, wup_hbm, cw_ref, cb_ref, wdn_hbm,
                       g_ref, b_ref, o_ref, xb_ref, acc_ref, carry_v_ref, carry_g_ref,
                       buf_v_ref, buf_g_ref, wv_ring, wg_ring, wd_ring, sems, *,
                       alpha, tiles_per_seq, chunk, nf):
    i = pl.program_id(0)
    n_steps = pl.num_programs(0) * nf
    rows = x_ref.shape[0]
    tf = wv_ring.shape[2]
    first = (i % tiles_per_seq) == 0

    def weight_copies(to_slot, blk):
        return (
            pltpu.make_async_copy(wup_hbm.at[blk], wv_ring.at[to_slot], sems.at[0, to_slot]),
            pltpu.make_async_copy(wup_hbm.at[nf + blk], wg_ring.at[to_slot], sems.at[1, to_slot]),
            pltpu.make_async_copy(wdn_hbm.at[pl.ds(pl.multiple_of(blk * tf, tf), tf), :],
                                  wd_ring.at[to_slot], sems.at[2, to_slot]),
        )

    @pl.when(i == 0)
    def _():
        for ahead in range(WEIGHT_SLOTS - 1):
            for copy in weight_copies(ahead, ahead % nf):
                copy.start()

    def conv(buf_ref, cols, r0):
        h = buf_ref[SUBLANES + r0:SUBLANES + r0 + chunk, :]
        hm1 = buf_ref[SUBLANES - 1 + r0:SUBLANES - 1 + r0 + chunk, :]
        hm2 = buf_ref[SUBLANES - 2 + r0:SUBLANES - 2 + r0 + chunk, :]
        w = cw_ref[:, cols]
        return w[0:1, :] * hm2 + w[1:2, :] * hm1 + w[2:3, :] * h + cb_ref[:, cols]

    def block_step(f, is_first, is_last):
        step_id = i * nf + f
        slot = step_id % WEIGHT_SLOTS
        for copy in weight_copies(slot, f):
            copy.wait()

        @pl.when(step_id + (WEIGHT_SLOTS - 1) < n_steps)
        def _():
            ahead = step_id + (WEIGHT_SLOTS - 1)
            for copy in weight_copies(ahead % WEIGHT_SLOTS, ahead % nf):
                copy.start()

        wv_ref = wv_ring.at[slot]
        wg_ref = wg_ring.at[slot]
        wd_ref = wd_ring.at[slot]
        cols_v = pl.ds(pl.multiple_of(f * tf, tf), tf)
        cols_g = pl.ds(pl.multiple_of((nf + f) * tf, tf), tf)

        @pl.when(first)
        def _():
            carry_v_ref[f] = jnp.zeros(carry_v_ref.shape[1:], F32)
            carry_g_ref[f] = jnp.zeros(carry_g_ref.shape[1:], F32)

        row_starts = list(range(0, rows, chunk))
        buf_v_ref[0:SUBLANES, :] = carry_v_ref[f]
        buf_g_ref[0:SUBLANES, :] = carry_g_ref[f]
        for r0 in row_starts:
            if is_first:
                xb_ref[r0:r0 + chunk, :] = x_ref[r0:r0 + chunk, :].astype(BF16)
            xb = xb_ref[r0:r0 + chunk, :]
            buf_v_ref[SUBLANES + r0:SUBLANES + r0 + chunk, :] = _dot(xb, wv_ref[...])
            buf_g_ref[SUBLANES + r0:SUBLANES + r0 + chunk, :] = _dot(xb, wg_ref[...])
        for r0 in row_starts:
            rs = slice(r0, r0 + chunk)
            val = conv(buf_v_ref, cols_v, r0)
            gate = conv(buf_g_ref, cols_g, r0)
            act = (jax.nn.silu(gate) * val).astype(BF16)
            part = _dot(act, wd_ref[...])
            if is_first:
                acc_ref[rs, :] = part
            elif is_last:
                z = alpha * x_ref[rs, :] + (acc_ref[rs, :] + part)
                o_ref[rs, :] = _layer_norm(z, g_ref[...], b_ref[...])
            else:
                acc_ref[rs, :] += part
        carry_v_ref[f] = buf_v_ref[rows:rows + SUBLANES, :]
        carry_g_ref[f] = buf_g_ref[rows:rows + SUBLANES, :]

    block_step(0, True, False)

    @pl.loop(1, nf - 1)
    def _(f):
        block_step(f, False, False)

    block_step(nf - 1, False, True)


def _convffn_ln(x2d, w_up_bf, w_ffn_conv, b_ffn_conv_row, w_down_bf, ln_g, ln_b, *,
                alpha, seq, d_ff):
    m, d = x2d.shape
    tm, tf = TM_FFN, TF_FFN
    nf = d_ff // tf
    kern = functools.partial(_convffn_ln_kernel, alpha=alpha, tiles_per_seq=seq // tm,
                             chunk=TR_FFN, nf=nf)
    return pl.pallas_call(
        kern,
        out_shape=jax.ShapeDtypeStruct((m, d), F32),
        grid=(m // tm,),
        in_specs=[
            pl.BlockSpec((tm, d), lambda i: (i, 0)),
            pl.BlockSpec(memory_space=pl.ANY),
            pl.BlockSpec((CONV_WIDTH, 2 * d_ff), lambda i: (0, 0)),
            pl.BlockSpec((1, 2 * d_ff), lambda i: (0, 0)),
            pl.BlockSpec(memory_space=pl.ANY),
            pl.BlockSpec((1, d), lambda i: (0, 0)),
            pl.BlockSpec((1, d), lambda i: (0, 0)),
        ],
        out_specs=pl.BlockSpec((tm, d), lambda i: (i, 0)),
        scratch_shapes=[
            pltpu.VMEM((tm, d), BF16),
            pltpu.VMEM((tm, d), F32),
            pltpu.VMEM((nf, SUBLANES, tf), F32),
            pltpu.VMEM((nf, SUBLANES, tf), F32),
            pltpu.VMEM((tm + SUBLANES, tf), F32),
            pltpu.VMEM((tm + SUBLANES, tf), F32),
            pltpu.VMEM((WEIGHT_SLOTS, d, tf), BF16),
            pltpu.VMEM((WEIGHT_SLOTS, d, tf), BF16),
            pltpu.VMEM((WEIGHT_SLOTS, tf, d), BF16),
            pltpu.SemaphoreType.DMA((N_WEIGHT_STREAMS, WEIGHT_SLOTS)),
        ],
        compiler_params=_params(("arbitrary",)),
        name="convffn_ln",
    )(x2d, w_up_bf, w_ffn_conv, b_ffn_conv_row, w_down_bf, ln_g, ln_b)


def kernel(x, w_in, b_gates, w_sc_conv, mh_gain, w_out, ln1_g, ln1_b,
           w_up, w_ffn_conv, b_ffn_conv, w_down, ln2_g, ln2_b):
    batch, seq, d = x.shape
    depth = w_in.shape[0]
    nh = N_HEADS_M
    w_conv = w_sc_conv.shape[-1]
    w_mlstm = mh_gain.shape[-1]
    dh = w_mlstm // nh
    d_ff = w_down.shape[1]
    n_gates = 2 * nh
    gate_col0 = 3 * w_conv + 4 * w_mlstm
    assert w_in.shape[-1] == gate_col0 + n_gates and w_conv + w_mlstm == d
    assert seq % TM_CONV == 0 and seq % TM_FFN == 0 and seq % MLSTM_CHUNK == 0
    assert w_conv % TC_CONV == 0 and d_ff % TF_FFN == 0
    assert (3 * w_conv) % TN_QKVO == 0 and (4 * w_mlstm) % TN_QKVO == 0
    assert (batch * seq) % TM_QKVO == 0
    alpha = (2 * depth) ** 0.25

    x2d = x.reshape(batch * seq, d)
    for l in range(depth):
        w_in_t = w_in[l].T
        w_gates_t_bf = jnp.pad(w_in_t[gate_col0:], ((0, LANES - n_gates), (0, 0))).astype(BF16)
        b_gates_row = jnp.pad(b_gates[l], (0, LANES - n_gates)).reshape(1, LANES)

        y_conv, xb2d, w_down_bf, w_out_bf = _inproj_conv(
            x2d, w_in_t, w_sc_conv[l], w_down[l], w_out[l], seq=seq, w_conv=w_conv)
        qkvo, w_up_bf = _inproj_qkvo(xb2d, w_in_t, w_up[l], col0=3 * w_conv, width=4 * w_mlstm,
                                     tf=TF_FFN)
        x2d = _mlstm_outproj(qkvo, xb2d, w_gates_t_bf, b_gates_row, mh_gain[l].reshape(1, w_mlstm),
                             x2d, y_conv, w_out_bf, ln1_g[l].reshape(1, d), ln1_b[l].reshape(1, d),
                             seq=seq, nh=nh, dh=dh, alpha=alpha)
        x2d = _convffn_ln(x2d, w_up_bf, w_ffn_conv[l],
                          b_ffn_conv[l].reshape(1, 2 * d_ff), w_down_bf,
                          ln2_g[l].reshape(1, d), ln2_b[l].reshape(1, d),
                          alpha=alpha, seq=seq, d_ff=d_ff)
    return x2d.reshape(batch, seq, d)
```

```python
import functools

import jax
import jax.numpy as jnp
from jax import lax
from jax.experimental import pallas as pl
from jax.experimental.pallas import tpu as pltpu

F32 = jnp.float32
BF16 = jnp.bfloat16

N_HEADS_M = 4
CONV_WIDTH = 3
LN_EPS = 1e-5
HEAD_NORM_EPS = 1e-6

SUBLANES = 8
LANES = 128
BF16_SUBLANES = 16
VMEM_LIMIT_BYTES = 56 * 1024 * 1024

TM_CONV = 512
TC_CONV = 512
TR_CONV = 256
TM_QKVO = 1024
TN_QKVO = 1024
MLSTM_CHUNK = 256
HEAD_GROUP = 4
OUT_COL_BLOCK = 512
TM_FFN = 512
TF_FFN = 512
TR_FFN = 256
WEIGHT_SLOTS = 3
N_WEIGHT_STREAMS = 3


def _dot(a, b):
    return jnp.dot(a, b, preferred_element_type=F32)


def _dot_t(a, b_t):
    return lax.dot_general(a, b_t, (((1,), (1,)), ((), ())), preferred_element_type=F32)


def _params(semantics):
    return pltpu.CompilerParams(dimension_semantics=semantics,
                                vmem_limit_bytes=VMEM_LIMIT_BYTES)


def _cast_rows_spec(rows, cols, n_steps, step_of):
    assert rows % n_steps == 0 and (rows // n_steps) % BF16_SUBLANES == 0
    return pl.BlockSpec((rows // n_steps, cols), lambda *ids: (step_of(*ids), 0))


def _inproj_conv_kernel(x_ref, wb_ref, wc_ref, wh_ref, cw_ref, wdn_ref, wout_ref,
                        y_ref, xb_ref, wdn_bf_ref, wout_bf_ref,
                        wbf_ref, carry_ref, buf_ref, *, tiles_per_seq, chunk):
    i = pl.program_id(1)
    rows = x_ref.shape[0]

    @pl.when(i == 0)
    def _():
        wbf_ref[0] = wb_ref[...].astype(BF16)
        wbf_ref[1] = wc_ref[...].astype(BF16)
        wbf_ref[2] = wh_ref[...].astype(BF16)

    @pl.when(pl.program_id(0) == 0)
    def _():
        xb_ref[...] = x_ref[...].astype(BF16)

    @pl.when(i % tiles_per_seq == 0)
    def _():
        carry_ref[...] = jnp.zeros(carry_ref.shape, F32)

    wdn_bf_ref[...] = wdn_ref[...].astype(BF16)
    wout_bf_ref[...] = wout_ref[...].astype(BF16)

    buf_ref[0:SUBLANES, :] = carry_ref[...]
    w = cw_ref[...]
    row_starts = list(range(0, rows, chunk))
    xbs = [x_ref[r0:r0 + chunk, :].astype(BF16) for r0 in row_starts]
    for r0, xb in zip(row_starts, xbs):
        u = _dot_t(xb, wbf_ref[1]) * _dot_t(xb, wbf_ref[2])
        buf_ref[SUBLANES + r0:SUBLANES + r0 + chunk, :] = u
    c_bs = [_dot_t(xb, wbf_ref[0]) for xb in xbs]
    for r0, c_b in zip(row_starts, c_bs):
        u = buf_ref[SUBLANES + r0:SUBLANES + r0 + chunk, :]
        um2 = buf_ref[SUBLANES - 2 + r0:SUBLANES - 2 + r0 + chunk, :]
        um1 = buf_ref[SUBLANES - 1 + r0:SUBLANES - 1 + r0 + chunk, :]
        y = w[0:1, :] * um2 + w[1:2, :] * um1 + w[2:3, :] * u
        y_ref[r0:r0 + chunk, :] = (c_b * y).astype(y_ref.dtype)
    carry_ref[...] = buf_ref[rows:rows + SUBLANES, :]


def _inproj_conv(x2d, w_in_t, w_sc_conv, w_down, w_out, *, seq, w_conv):
    m, d = x2d.shape
    tm, tc = TM_CONV, TC_CONV
    nj, ni = w_conv // tc, m // tm
    n_steps = nj * ni
    step_of = lambda j, i: j * ni + i
    once = pl.Buffered(1)
    kern = functools.partial(_inproj_conv_kernel, tiles_per_seq=seq // tm, chunk=TR_CONV)
    return pl.pallas_call(
        kern,
        out_shape=(jax.ShapeDtypeStruct((m, w_conv), BF16),
                   jax.ShapeDtypeStruct((m, d), BF16),
                   jax.ShapeDtypeStruct(w_down.shape, BF16),
                   jax.ShapeDtypeStruct(w_out.shape, BF16)),
        grid=(nj, ni),
        in_specs=[
            pl.BlockSpec((tm, d), lambda j, i: (i, 0)),
            pl.BlockSpec((tc, d), lambda j, i: (j, 0), pipeline_mode=once),
            pl.BlockSpec((tc, d), lambda j, i: (nj + j, 0), pipeline_mode=once),
            pl.BlockSpec((tc, d), lambda j, i: (2 * nj + j, 0), pipeline_mode=once),
            pl.BlockSpec((CONV_WIDTH, tc), lambda j, i: (0, j)),
            _cast_rows_spec(*w_down.shape, n_steps, step_of),
            _cast_rows_spec(*w_out.shape, n_steps, step_of),
        ],
        out_specs=(pl.BlockSpec((tm, tc), lambda j, i: (i, j)),
                   pl.BlockSpec((tm, d), lambda j, i: (jnp.where(j == 0, i, ni - 1), 0)),
                   _cast_rows_spec(*w_down.shape, n_steps, step_of),
                   _cast_rows_spec(*w_out.shape, n_steps, step_of)),
        scratch_shapes=[
            pltpu.VMEM((3, tc, d), BF16),
            pltpu.VMEM((SUBLANES, tc), F32),
            pltpu.VMEM((tm + SUBLANES, tc), F32),
        ],
        compiler_params=_params(("arbitrary", "arbitrary")),
        name="inproj_shortconv",
    )(x2d, w_in_t, w_in_t, w_in_t, w_sc_conv, w_down, w_out)


def _inproj_qkvo_kernel(xb_ref, w_ref, wup_ref, o_ref, wup_bf_ref, wbf_ref):
    @pl.when(pl.program_id(1) == 0)
    def _():
        wbf_ref[...] = w_ref[...].astype(BF16)

    n_blocks, _, tf = wup_bf_ref.shape
    for n in range(n_blocks):
        wup_bf_ref[n] = wup_ref[:, n * tf:(n + 1) * tf].astype(BF16)
    o_ref[...] = _dot_t(xb_ref[...], wbf_ref[...]).astype(o_ref.dtype)


def _inproj_qkvo(xb2d, w_in_t, w_up, *, col0, width, tf):
    m, d = xb2d.shape
    tm, tn = TM_QKVO, TN_QKVO
    j0 = col0 // tn
    nj, ni = width // tn, m // tm
    n_steps = nj * ni
    step_of = lambda j, i: j * ni + i
    up_rows, up_cols = w_up.shape
    n_blocks = up_cols // tf
    rows_per_step = up_rows // n_steps
    assert up_cols % tf == 0 and up_rows % n_steps == 0 and rows_per_step % BF16_SUBLANES == 0
    return pl.pallas_call(
        _inproj_qkvo_kernel,
        out_shape=(jax.ShapeDtypeStruct((m, width), BF16),
                   jax.ShapeDtypeStruct((n_blocks, up_rows, tf), BF16)),
        grid=(nj, ni),
        in_specs=[
            pl.BlockSpec((tm, d), lambda j, i: (i, 0)),
            pl.BlockSpec((tn, d), lambda j, i: (j0 + j, 0)),
            _cast_rows_spec(up_rows, up_cols, n_steps, step_of),
        ],
        out_specs=(pl.BlockSpec((tm, tn), lambda j, i: (i, j)),
                   pl.BlockSpec((n_blocks, rows_per_step, tf),
                                lambda j, i: (0, step_of(j, i), 0))),
        scratch_shapes=[pltpu.VMEM((tn, d), BF16)],
        compiler_params=_params(("arbitrary", "arbitrary")),
        name="inproj_mlstm",
    )(xb2d, w_in_t, w_up)


def _log_sigmoid(x):
    return jnp.minimum(x, 0.0) - jnp.log1p(jnp.exp(-jnp.abs(x)))


def _layer_norm(z, g, b):
    mu = jnp.mean(z, axis=-1, keepdims=True)
    zc = z - mu
    var = jnp.mean(zc * zc, axis=-1, keepdims=True)
    return zc * lax.rsqrt(var + LN_EPS) * g + b


def _mlstm_outproj_kernel(*refs, nh, dh, chunks_per_seq, alpha, n_split):
    (q_ref, k_ref, v_ref, o_ref, xb0_ref, xbn_ref, wg_ref, bg_ref, gain_ref,
     x_ref, yc_ref) = refs[:11]
    wc_refs = refs[11:11 + n_split]
    wm_refs = refs[11 + n_split:11 + 2 * n_split]
    (lng_ref, lnb_ref, x1_ref,
     c_ref, n_ref, m_ref, gates_ref, gates_t_ref, ym_ref) = refs[11 + 2 * n_split:]
    chunk = q_ref.shape[0]
    step = pl.program_id(0)

    @pl.when(step % chunks_per_seq == 0)
    def _():
        c_ref[...] = jnp.zeros(c_ref.shape, F32)
        n_ref[...] = jnp.zeros(n_ref.shape, F32)
        m_ref[...] = jnp.zeros(m_ref.shape, F32)

    row = lax.broadcasted_iota(jnp.int32, (chunk, chunk), 0)
    col = lax.broadcasted_iota(jnp.int32, (chunk, chunk), 1)
    causal = row >= col

    def gate_terms(xb):
        g = _dot_t(xb, wg_ref[...]) + bg_ref[...]
        lf = _log_sigmoid(g)
        lf_hi = lf.astype(BF16)
        lf_r1 = lf - lf_hi.astype(F32)
        lf_mid = lf_r1.astype(BF16)
        lf_lo = (lf_r1 - lf_mid.astype(F32)).astype(BF16)
        parts = _dot(causal.astype(BF16), jnp.concatenate([lf_hi, lf_mid, lf_lo], axis=1))
        bcum = parts[:, :LANES] + parts[:, LANES:2 * LANES] + parts[:, 2 * LANES:]
        gates_ref[0] = g
        gates_ref[1] = bcum
        gates_t_ref[0] = g.T
        gates_t_ref[1] = bcum.T

    @pl.when(step == 0)
    def _():
        gate_terms(xb0_ref[...])
        ym_ref[...] = jnp.zeros(ym_ref.shape, ym_ref.dtype)

    g = gates_ref[0]
    bcum = gates_ref[1]
    g_t = gates_t_ref[0]
    bcum_t = gates_t_ref[1]
    neg_inf = jnp.float32(-jnp.inf)
    cols = [slice(h * dh, (h + 1) * dh) for h in range(nh)]

    yc_prev = yc_ref[...]
    ym_prev = ym_ref[...]
    mix = jnp.concatenate(
        [_dot(yc_prev, wc[...]) + _dot(ym_prev, wm[...])
         for wc, wm in zip(wc_refs, wm_refs)], axis=1)

    def recurrence(heads):
        q = {h: q_ref[:, cols[h]] * jnp.asarray(dh ** -0.5, BF16) for h in heads}
        k = {h: k_ref[:, cols[h]] for h in heads}
        v = {h: v_ref[:, cols[h]] for h in heads}
        i_col = {h: g[:, h:h + 1] for h in heads}
        b_col = {h: bcum[:, nh + h:nh + h + 1] for h in heads}
        i_row = {h: g_t[h:h + 1, :] for h in heads}
        b_row = {h: bcum_t[nh + h:nh + h + 1, :] for h in heads}
        g_tot = {h: b_col[h][chunk - 1:chunk, :] for h in heads}
        m_prev = {h: m_ref[h, 0:1, 0:1] for h in heads}
        c_prev = {h: c_ref[h] for h in heads}
        n_prev = {h: n_ref[h, 0:1, :] for h in heads}

        qk = {h: _dot_t(q[h], k[h]) for h in heads}
        inter = {h: _dot(q[h], c_prev[h].astype(BF16)) for h in heads}
        qn = {h: jnp.sum(q[h].astype(F32) * n_prev[h], axis=-1, keepdims=True) for h in heads}

        log_d = {h: jnp.where(causal, b_col[h] - (b_row[h] - i_row[h]), neg_inf) for h in heads}
        inter_log = {h: b_col[h] + m_prev[h] for h in heads}
        m_t = {h: jnp.maximum(inter_log[h], jnp.max(log_d[h], axis=-1, keepdims=True))
               for h in heads}
        scores = {h: qk[h] * jnp.exp(log_d[h] - m_t[h]) for h in heads}
        inter_w = {h: jnp.exp(inter_log[h] - m_t[h]) for h in heads}
        num = {h: _dot(scores[h].astype(BF16), v[h]) + inter_w[h] * inter[h] for h in heads}
        den = {h: jnp.sum(scores[h], axis=-1, keepdims=True) + inter_w[h] * qn[h] for h in heads}
        hh = {h: num[h] / jnp.maximum(jnp.abs(den[h]), jnp.exp(-m_t[h])) for h in heads}

        w_log = {h: g_tot[h] - b_col[h] + i_col[h] for h in heads}
        m_new = {h: jnp.maximum(g_tot[h] + m_prev[h], jnp.max(w_log[h], axis=0, keepdims=True))
                 for h in heads}
        decay = {h: jnp.exp(g_tot[h] + m_prev[h] - m_new[h]) for h in heads}
        kw = {h: k[h].astype(F32) * jnp.exp(w_log[h] - m_new[h]) for h in heads}
        for h in heads:
            c_ref[h] = decay[h] * c_prev[h] + lax.dot_general(
                kw[h].astype(BF16), v[h], (((0,), (0,)), ((), ())), preferred_element_type=F32)
            n_ref[h, 0:1, :] = decay[h] * n_prev[h] + jnp.sum(kw[h], axis=0, keepdims=True)
            m_ref[h] = jnp.broadcast_to(m_new[h], m_ref.shape[1:])

        return hh

    def head_norm_gate(heads, hh):
        mu = {h: jnp.mean(hh[h], axis=-1, keepdims=True) for h in heads}
        hc = {h: hh[h] - mu[h] for h in heads}
        var = {h: jnp.mean(hc[h] * hc[h], axis=-1, keepdims=True) for h in heads}
        for h in heads:
            hn = hc[h] * lax.rsqrt(var[h] + HEAD_NORM_EPS) * gain_ref[:, cols[h]]
            o = o_ref[:, cols[h]].astype(F32)
            ym_ref[:, cols[h]] = (jax.nn.sigmoid(o) * hn).astype(ym_ref.dtype)

    groups = [tuple(range(h0, min(h0 + HEAD_GROUP, nh))) for h0 in range(0, nh, HEAD_GROUP)]
    hhs = [recurrence(group) for group in groups]
    z = alpha * x_ref[...] + mix
    x1_ref[...] = _layer_norm(z, lng_ref[...], lnb_ref[...])
    for group, hh in zip(groups, hhs):
        head_norm_gate(group, hh)

    gate_terms(xbn_ref[...])


def _mlstm_outproj(qkvo, xb2d, w_gates_t_bf, b_gates_row, gain_row, x2d, y_conv, w_out_bf,
                   ln_g, ln_b, *, seq, nh, dh, alpha):
    m, d = x2d.shape
    w = nh * dh
    wc = y_conv.shape[1]
    assert wc + w == w_out_bf.shape[0] and wc == w
    chunk = MLSTM_CHUNK
    n_chunks = m // chunk
    last = n_chunks - 1

    def cur(s):
        return jnp.minimum(s, last)

    def prev(s):
        return jnp.maximum(s - 1, 0)

    def blk(kind):
        return pl.BlockSpec((chunk, w), lambda s: (cur(s), kind))

    const = pl.Buffered(1)
    piece = OUT_COL_BLOCK
    assert d % piece == 0
    n_split = d // piece
    kern = functools.partial(_mlstm_outproj_kernel, nh=nh, dh=dh, chunks_per_seq=seq // chunk,
                             alpha=alpha, n_split=n_split)
    return pl.pallas_call(
        kern,
        out_shape=jax.ShapeDtypeStruct((m, d), F32),
        grid=(n_chunks + 1,),
        in_specs=[blk(0), blk(1), blk(2), blk(3),
                  pl.BlockSpec((chunk, d), lambda s: (0, 0)),
                  pl.BlockSpec((chunk, d), lambda s: (jnp.minimum(s + 1, last), 0)),
                  pl.BlockSpec((LANES, d), lambda s: (0, 0)),
                  pl.BlockSpec((1, LANES), lambda s: (0, 0)),
                  pl.BlockSpec((1, w), lambda s: (0, 0)),
                  pl.BlockSpec((chunk, d), lambda s: (prev(s), 0)),
                  pl.BlockSpec((chunk, wc), lambda s: (prev(s), 0)),
                  *[pl.BlockSpec((wc, piece), lambda s, p=p: (0, p), pipeline_mode=const)
                    for p in range(n_split)],
                  *[pl.BlockSpec((w, piece), lambda s, p=p: (1, p), pipeline_mode=const)
                    for p in range(n_split)],
                  pl.BlockSpec((1, d), lambda s: (0, 0)),
                  pl.BlockSpec((1, d), lambda s: (0, 0))],
        out_specs=pl.BlockSpec((chunk, d), lambda s: (prev(s), 0)),
        scratch_shapes=[
            pltpu.VMEM((nh, dh, dh), F32),
            pltpu.VMEM((nh, SUBLANES, dh), F32),
            pltpu.VMEM((nh, SUBLANES, LANES), F32),
            pltpu.VMEM((2, chunk, LANES), F32),
            pltpu.VMEM((2, LANES, chunk), F32),
            pltpu.VMEM((chunk, w), BF16),
        ],
        compiler_params=_params(("arbitrary",)),
        name="mlstm_outproj_ln",
    )(qkvo, qkvo, qkvo, qkvo, xb2d, xb2d, w_gates_t_bf, b_gates_row, gain_row,
      x2d, y_conv, *([w_out_bf] * (2 * n_split)), ln_g, ln_b)


def _convffn_ln_kernel(x_ref, wup_hbm, cw_ref, cb_ref, wdn_hbm,
                       g_ref, b_ref, o_ref, xb_ref, acc_ref, carry_v_ref, carry_g_ref,
                       buf_v_ref, buf_g_ref, wv_ring, wg_ring, wd_ring, sems, *,
                       alpha, tiles_per_seq, chunk, nf):
    i = pl.program_id(0)
    n_steps = pl.num_programs(0) * nf
    rows = x_ref.shape[0]
    tf = wv_ring.shape[2]
    first = (i % tiles_per_seq) == 0

    def weight_copies(to_slot, blk):
        return (
            pltpu.make_async_copy(wup_hbm.at[blk], wv_ring.at[to_slot], sems.at[0, to_slot]),
            pltpu.make_async_copy(wup_hbm.at[nf + blk], wg_ring.at[to_slot], sems.at[1, to_slot]),
            pltpu.make_async_copy(wdn_hbm.at[pl.ds(pl.multiple_of(blk * tf, tf), tf), :],
                                  wd_ring.at[to_slot], sems.at[2, to_slot]),
        )

    @pl.when(i == 0)
    def _():
        for ahead in range(WEIGHT_SLOTS - 1):
            for copy in weight_copies(ahead, ahead % nf):
                copy.start()

    def conv(buf_ref, cols, r0):
        h = buf_ref[SUBLANES + r0:SUBLANES + r0 + chunk, :]
        hm1 = buf_ref[SUBLANES - 1 + r0:SUBLANES - 1 + r0 + chunk, :]
        hm2 = buf_ref[SUBLANES - 2 + r0:SUBLANES - 2 + r0 + chunk, :]
        w = cw_ref[:, cols]
        return w[0:1, :] * hm2 + w[1:2, :] * hm1 + w[2:3, :] * h + cb_ref[:, cols]

    def block_step(f, is_first, is_last):
        step_id = i * nf + f
        slot = step_id % WEIGHT_SLOTS
        for copy in weight_copies(slot, f):
            copy.wait()

        @pl.when(step_id + (WEIGHT_SLOTS - 1) < n_steps)
        def _():
            ahead = step_id + (WEIGHT_SLOTS - 1)
            for copy in weight_copies(ahead % WEIGHT_SLOTS, ahead % nf):
                copy.start()

        wv_ref = wv_ring.at[slot]
        wg_ref = wg_ring.at[slot]
        wd_ref = wd_ring.at[slot]
        cols_v = pl.ds(pl.multiple_of(f * tf, tf), tf)
        cols_g = pl.ds(pl.multiple_of((nf + f) * tf, tf), tf)

        @pl.when(first)
        def _():
            carry_v_ref[f] = jnp.zeros(carry_v_ref.shape[1:], F32)
            carry_g_ref[f] = jnp.zeros(carry_g_ref.shape[1:], F32)

        row_starts = list(range(0, rows, chunk))
        buf_v_ref[0:SUBLANES, :] = carry_v_ref[f]
        buf_g_ref[0:SUBLANES, :] = carry_g_ref[f]
        for r0 in row_starts:
            if is_first:
                xb_ref[r0:r0 + chunk, :] = x_ref[r0:r0 + chunk, :].astype(BF16)
            xb = xb_ref[r0:r0 + chunk, :]
            buf_v_ref[SUBLANES + r0:SUBLANES + r0 + chunk, :] = _dot(xb, wv_ref[...])
            buf_g_ref[SUBLANES + r0:SUBLANES + r0 + chunk, :] = _dot(xb, wg_ref[...])
        for r0 in row_starts:
            rs = slice(r0, r0 + chunk)
            val = conv(buf_v_ref, cols_v, r0)
            gate = conv(buf_g_ref, cols_g, r0)
            act = (jax.nn.silu(gate) * val).astype(BF16)
            part = _dot(act, wd_ref[...])
            if is_first:
                acc_ref[rs, :] = part
            elif is_last:
                z = alpha * x_ref[rs, :] + (acc_ref[rs, :] + part)
                o_ref[rs, :] = _layer_norm(z, g_ref[...], b_ref[...])
            else:
                acc_ref[rs, :] += part
        carry_v_ref[f] = buf_v_ref[rows:rows + SUBLANES, :]
        carry_g_ref[f] = buf_g_ref[rows:rows + SUBLANES, :]

    block_step(0, True, False)

    @pl.loop(1, nf - 1)
    def _(f):
        block_step(f, False, False)

    block_step(nf - 1, False, True)


def _convffn_ln(x2d, w_up_bf, w_ffn_conv, b_ffn_conv_row, w_down_bf, ln_g, ln_b, *,
                alpha, seq, d_ff):
    m, d = x2d.shape
    tm, tf = TM_FFN, TF_FFN
    nf = d_ff // tf
    kern = functools.partial(_convffn_ln_kernel, alpha=alpha, tiles_per_seq=seq // tm,
                             chunk=TR_FFN, nf=nf)
    return pl.pallas_call(
        kern,
        out_shape=jax.ShapeDtypeStruct((m, d), F32),
        grid=(m // tm,),
        in_specs=[
            pl.BlockSpec((tm, d), lambda i: (i, 0)),
            pl.BlockSpec(memory_space=pl.ANY),
            pl.BlockSpec((CONV_WIDTH, 2 * d_ff), lambda i: (0, 0)),
            pl.BlockSpec((1, 2 * d_ff), lambda i: (0, 0)),
            pl.BlockSpec(memory_space=pl.ANY),
            pl.BlockSpec((1, d), lambda i: (0, 0)),
            pl.BlockSpec((1, d), lambda i: (0, 0)),
        ],
        out_specs=pl.BlockSpec((tm, d), lambda i: (i, 0)),
        scratch_shapes=[
            pltpu.VMEM((tm, d), BF16),
            pltpu.VMEM((tm, d), F32),
            pltpu.VMEM((nf, SUBLANES, tf), F32),
            pltpu.VMEM((nf, SUBLANES, tf), F32),
            pltpu.VMEM((tm + SUBLANES, tf), F32),
            pltpu.VMEM((tm + SUBLANES, tf), F32),
            pltpu.VMEM((WEIGHT_SLOTS, d, tf), BF16),
            pltpu.VMEM((WEIGHT_SLOTS, d, tf), BF16),
            pltpu.VMEM((WEIGHT_SLOTS, tf, d), BF16),
            pltpu.SemaphoreType.DMA((N_WEIGHT_STREAMS, WEIGHT_SLOTS)),
        ],
        compiler_params=_params(("arbitrary",)),
        name="convffn_ln",
    )(x2d, w_up_bf, w_ffn_conv, b_ffn_conv_row, w_down_bf, ln_g, ln_b)


def kernel(x, w_in, b_gates, w_sc_conv, mh_gain, w_out, ln1_g, ln1_b,
           w_up, w_ffn_conv, b_ffn_conv, w_down, ln2_g, ln2_b):
    batch, seq, d = x.shape
    depth = w_in.shape[0]
    nh = N_HEADS_M
    w_conv = w_sc_conv.shape[-1]
    w_mlstm = mh_gain.shape[-1]
    dh = w_mlstm // nh
    d_ff = w_down.shape[1]
    n_gates = 2 * nh
    gate_col0 = 3 * w_conv + 4 * w_mlstm
    assert w_in.shape[-1] == gate_col0 + n_gates and w_conv + w_mlstm == d
    assert seq % TM_CONV == 0 and seq % TM_FFN == 0 and seq % MLSTM_CHUNK == 0
    assert w_conv % TC_CONV == 0 and d_ff % TF_FFN == 0
    assert (3 * w_conv) % TN_QKVO == 0 and (4 * w_mlstm) % TN_QKVO == 0
    assert (batch * seq) % TM_QKVO == 0
    alpha = (2 * depth) ** 0.25

    x2d = x.reshape(batch * seq, d)
    for l in range(depth):
        w_in_t = w_in[l].T
        w_gates_t_bf = jnp.pad(w_in_t[gate_col0:], ((0, LANES - n_gates), (0, 0))).astype(BF16)
        b_gates_row = jnp.pad(b_gates[l], (0, LANES - n_gates)).reshape(1, LANES)

        y_conv, xb2d, w_down_bf, w_out_bf = _inproj_conv(
            x2d, w_in_t, w_sc_conv[l], w_down[l], w_out[l], seq=seq, w_conv=w_conv)
        qkvo, w_up_bf = _inproj_qkvo(xb2d, w_in_t, w_up[l], col0=3 * w_conv, width=4 * w_mlstm,
                                     tf=TF_FFN)
        x2d = _mlstm_outproj(qkvo, xb2d, w_gates_t_bf, b_gates_row, mh_gain[l].reshape(1, w_mlstm),
                             x2d, y_conv, w_out_bf, ln1_g[l].reshape(1, d), ln1_b[l].reshape(1, d),
                             seq=seq, nh=nh, dh=dh, alpha=alpha)
        x2d = _convffn_ln(x2d, w_up_bf, w_ffn_conv[l],
                          b_ffn_conv[l].reshape(1, 2 * d_ff), w_down_bf,
                          ln2_g[l].reshape(1, d), ln2_b[l].reshape(1, d),
                          alpha=alpha, seq=seq, d_ff=d_ff)
    return x2d.reshape(batch, seq, d)
```

```python
import functools

import jax
import jax.numpy as jnp
from jax import lax
from jax.experimental import pallas as pl
from jax.experimental.pallas import tpu as pltpu

F32 = jnp.float32
BF16 = jnp.bfloat16

N_HEADS_M = 4
CONV_WIDTH = 3
LN_EPS = 1e-5
HEAD_NORM_EPS = 1e-6

SUBLANES = 8
LANES = 128
BF16_SUBLANES = 16
VMEM_LIMIT_BYTES = 56 * 1024 * 1024

TM_CONV = 512
TC_CONV = 512
TR_CONV = 256
TM_QKVO = 1024
TN_QKVO = 1024
MLSTM_CHUNK = 256
HEAD_GROUP = 2
OUT_COL_BLOCK = 512
TM_FFN = 512
TF_FFN = 512
TR_FFN = 256
WEIGHT_SLOTS = 3
N_WEIGHT_STREAMS = 3


def _dot(a, b):
    return jnp.dot(a, b, preferred_element_type=F32)


def _dot_t(a, b_t):
    return lax.dot_general(a, b_t, (((1,), (1,)), ((), ())), preferred_element_type=F32)


def _params(semantics):
    return pltpu.CompilerParams(dimension_semantics=semantics,
                                vmem_limit_bytes=VMEM_LIMIT_BYTES)


def _cast_rows_spec(rows, cols, n_steps, step_of):
    assert rows % n_steps == 0 and (rows // n_steps) % BF16_SUBLANES == 0
    return pl.BlockSpec((rows // n_steps, cols), lambda *ids: (step_of(*ids), 0))


def _inproj_conv_kernel(x_ref, wb_ref, wc_ref, wh_ref, cw_ref, wdn_ref, wout_ref,
                        y_ref, xb_ref, wdn_bf_ref, wout_bf_ref,
                        wbf_ref, carry_ref, buf_ref, *, tiles_per_seq, chunk):
    i = pl.program_id(1)
    rows = x_ref.shape[0]

    @pl.when(i == 0)
    def _():
        wbf_ref[0] = wb_ref[...].astype(BF16)
        wbf_ref[1] = wc_ref[...].astype(BF16)
        wbf_ref[2] = wh_ref[...].astype(BF16)

    @pl.when(pl.program_id(0) == 0)
    def _():
        xb_ref[...] = x_ref[...].astype(BF16)

    @pl.when(i % tiles_per_seq == 0)
    def _():
        carry_ref[...] = jnp.zeros(carry_ref.shape, F32)

    wdn_bf_ref[...] = wdn_ref[...].astype(BF16)
    wout_bf_ref[...] = wout_ref[...].astype(BF16)

    buf_ref[0:SUBLANES, :] = carry_ref[...]
    w = cw_ref[...]
    row_starts = list(range(0, rows, chunk))
    xbs = [x_ref[r0:r0 + chunk, :].astype(BF16) for r0 in row_starts]
    for r0, xb in zip(row_starts, xbs):
        u = _dot_t(xb, wbf_ref[1]) * _dot_t(xb, wbf_ref[2])
        buf_ref[SUBLANES + r0:SUBLANES + r0 + chunk, :] = u
    c_bs = [_dot_t(xb, wbf_ref[0]) for xb in xbs]
    for r0, c_b in zip(row_starts, c_bs):
        u = buf_ref[SUBLANES + r0:SUBLANES + r0 + chunk, :]
        um2 = buf_ref[SUBLANES - 2 + r0:SUBLANES - 2 + r0 + chunk, :]
        um1 = buf_ref[SUBLANES - 1 + r0:SUBLANES - 1 + r0 + chunk, :]
        y = w[0:1, :] * um2 + w[1:2, :] * um1 + w[2:3, :] * u
        y_ref[r0:r0 + chunk, :] = (c_b * y).astype(y_ref.dtype)
    carry_ref[...] = buf_ref[rows:rows + SUBLANES, :]


def _inproj_conv(x2d, w_in_t, w_sc_conv, w_down, w_out, *, seq, w_conv):
    m, d = x2d.shape
    tm, tc = TM_CONV, TC_CONV
    nj, ni = w_conv // tc, m // tm
    n_steps = nj * ni
    step_of = lambda j, i: j * ni + i
    once = pl.Buffered(1)
    kern = functools.partial(_inproj_conv_kernel, tiles_per_seq=seq // tm, chunk=TR_CONV)
    return pl.pallas_call(
        kern,
        out_shape=(jax.ShapeDtypeStruct((m, w_conv), BF16),
                   jax.ShapeDtypeStruct((m, d), BF16),
                   jax.ShapeDtypeStruct(w_down.shape, BF16),
                   jax.ShapeDtypeStruct(w_out.shape, BF16)),
        grid=(nj, ni),
        in_specs=[
            pl.BlockSpec((tm, d), lambda j, i: (i, 0)),
            pl.BlockSpec((tc, d), lambda j, i: (j, 0), pipeline_mode=once),
            pl.BlockSpec((tc, d), lambda j, i: (nj + j, 0), pipeline_mode=once),
            pl.BlockSpec((tc, d), lambda j, i: (2 * nj + j, 0), pipeline_mode=once),
            pl.BlockSpec((CONV_WIDTH, tc), lambda j, i: (0, j)),
            _cast_rows_spec(*w_down.shape, n_steps, step_of),
            _cast_rows_spec(*w_out.shape, n_steps, step_of),
        ],
        out_specs=(pl.BlockSpec((tm, tc), lambda j, i: (i, j)),
                   pl.BlockSpec((tm, d), lambda j, i: (jnp.where(j == 0, i, ni - 1), 0)),
                   _cast_rows_spec(*w_down.shape, n_steps, step_of),
                   _cast_rows_spec(*w_out.shape, n_steps, step_of)),
        scratch_shapes=[
            pltpu.VMEM((3, tc, d), BF16),
            pltpu.VMEM((SUBLANES, tc), F32),
            pltpu.VMEM((tm + SUBLANES, tc), F32),
        ],
        compiler_params=_params(("arbitrary", "arbitrary")),
        name="inproj_shortconv",
    )(x2d, w_in_t, w_in_t, w_in_t, w_sc_conv, w_down, w_out)


def _inproj_qkvo_kernel(xb_ref, w_ref, wup_ref, o_ref, wup_bf_ref, wbf_ref):
    @pl.when(pl.program_id(1) == 0)
    def _():
        wbf_ref[...] = w_ref[...].astype(BF16)

    n_blocks, _, tf = wup_bf_ref.shape
    for n in range(n_blocks):
        wup_bf_ref[n] = wup_ref[:, n * tf:(n + 1) * tf].astype(BF16)
    o_ref[...] = _dot_t(xb_ref[...], wbf_ref[...]).astype(o_ref.dtype)


def _inproj_qkvo(xb2d, w_in_t, w_up, *, col0, width, tf):
    m, d = xb2d.shape
    tm, tn = TM_QKVO, TN_QKVO
    j0 = col0 // tn
    nj, ni = width // tn, m // tm
    n_steps = nj * ni
    step_of = lambda j, i: j * ni + i
    up_rows, up_cols = w_up.shape
    n_blocks = up_cols // tf
    rows_per_step = up_rows // n_steps
    assert up_cols % tf == 0 and up_rows % n_steps == 0 and rows_per_step % BF16_SUBLANES == 0
    return pl.pallas_call(
        _inproj_qkvo_kernel,
        out_shape=(jax.ShapeDtypeStruct((m, width), BF16),
                   jax.ShapeDtypeStruct((n_blocks, up_rows, tf), BF16)),
        grid=(nj, ni),
        in_specs=[
            pl.BlockSpec((tm, d), lambda j, i: (i, 0)),
            pl.BlockSpec((tn, d), lambda j, i: (j0 + j, 0)),
            _cast_rows_spec(up_rows, up_cols, n_steps, step_of),
        ],
        out_specs=(pl.BlockSpec((tm, tn), lambda j, i: (i, j)),
                   pl.BlockSpec((n_blocks, rows_per_step, tf),
                                lambda j, i: (0, step_of(j, i), 0))),
        scratch_shapes=[pltpu.VMEM((tn, d), BF16)],
        compiler_params=_params(("arbitrary", "arbitrary")),
        name="inproj_mlstm",
    )(xb2d, w_in_t, w_up)


def _log_sigmoid(x):
    return jnp.minimum(x, 0.0) - jnp.log1p(jnp.exp(-jnp.abs(x)))


def _layer_norm(z, g, b):
    mu = jnp.mean(z, axis=-1, keepdims=True)
    zc = z - mu
    var = jnp.mean(zc * zc, axis=-1, keepdims=True)
    return zc * lax.rsqrt(var + LN_EPS) * g + b


def _mlstm_outproj_kernel(*refs, nh, dh, chunks_per_seq, alpha, n_split):
    (q_ref, k_ref, v_ref, o_ref, xb0_ref, xbn_ref, wg_ref, bg_ref, gain_ref,
     x_ref, yc_ref) = refs[:11]
    wc_refs = refs[11:11 + n_split]
    wm_refs = refs[11 + n_split:11 + 2 * n_split]
    (lng_ref, lnb_ref, x1_ref,
     c_ref, n_ref, m_ref, gates_ref, gates_t_ref, ym_ref) = refs[11 + 2 * n_split:]
    chunk = q_ref.shape[0]
    step = pl.program_id(0)

    @pl.when(step % chunks_per_seq == 0)
    def _():
        c_ref[...] = jnp.zeros(c_ref.shape, F32)
        n_ref[...] = jnp.zeros(n_ref.shape, F32)
        m_ref[...] = jnp.zeros(m_ref.shape, F32)

    row = lax.broadcasted_iota(jnp.int32, (chunk, chunk), 0)
    col = lax.broadcasted_iota(jnp.int32, (chunk, chunk), 1)
    causal = row >= col

    def gate_terms(xb):
        g = _dot_t(xb, wg_ref[...]) + bg_ref[...]
        lf = _log_sigmoid(g)
        lf_hi = lf.astype(BF16)
        lf_r1 = lf - lf_hi.astype(F32)
        lf_mid = lf_r1.astype(BF16)
        lf_lo = (lf_r1 - lf_mid.astype(F32)).astype(BF16)
        parts = _dot(causal.astype(BF16), jnp.concatenate([lf_hi, lf_mid, lf_lo], axis=1))
        bcum = parts[:, :LANES] + parts[:, LANES:2 * LANES] + parts[:, 2 * LANES:]
        gates_ref[0] = g
        gates_ref[1] = bcum
        gates_t_ref[0] = g.T
        gates_t_ref[1] = bcum.T

    @pl.when(step == 0)
    def _():
        gate_terms(xb0_ref[...])
        ym_ref[...] = jnp.zeros(ym_ref.shape, ym_ref.dtype)

    g = gates_ref[0]
    bcum = gates_ref[1]
    g_t = gates_t_ref[0]
    bcum_t = gates_t_ref[1]
    neg_inf = jnp.float32(-jnp.inf)
    cols = [slice(h * dh, (h + 1) * dh) for h in range(nh)]

    yc_prev = yc_ref[...]
    ym_prev = ym_ref[...]
    mix = jnp.concatenate(
        [_dot(yc_prev, wc[...]) + _dot(ym_prev, wm[...])
         for wc, wm in zip(wc_refs, wm_refs)], axis=1)

    def recurrence(heads):
        q = {h: q_ref[:, cols[h]] * jnp.asarray(dh ** -0.5, BF16) for h in heads}
        k = {h: k_ref[:, cols[h]] for h in heads}
        v = {h: v_ref[:, cols[h]] for h in heads}
        i_col = {h: g[:, h:h + 1] for h in heads}
        b_col = {h: bcum[:, nh + h:nh + h + 1] for h in heads}
        i_row = {h: g_t[h:h + 1, :] for h in heads}
        b_row = {h: bcum_t[nh + h:nh + h + 1, :] for h in heads}
        g_tot = {h: b_col[h][chunk - 1:chunk, :] for h in heads}
        m_prev = {h: m_ref[h, 0:1, 0:1] for h in heads}
        c_prev = {h: c_ref[h] for h in heads}
        n_prev = {h: n_ref[h, 0:1, :] for h in heads}

        qk = {h: _dot_t(q[h], k[h]) for h in heads}
        inter = {h: _dot(q[h], c_prev[h].astype(BF16)) for h in heads}
        qn = {h: jnp.sum(q[h].astype(F32) * n_prev[h], axis=-1, keepdims=True) for h in heads}

        log_d = {h: jnp.where(causal, b_col[h] - (b_row[h] - i_row[h]), neg_inf) for h in heads}
        inter_log = {h: b_col[h] + m_prev[h] for h in heads}
        m_t = {h: jnp.maximum(inter_log[h], jnp.max(log_d[h], axis=-1, keepdims=True))
               for h in heads}
        scores = {h: qk[h] * jnp.exp(log_d[h] - m_t[h]) for h in heads}
        inter_w = {h: jnp.exp(inter_log[h] - m_t[h]) for h in heads}
        num = {h: _dot(scores[h].astype(BF16), v[h]) + inter_w[h] * inter[h] for h in heads}
        den = {h: jnp.sum(scores[h], axis=-1, keepdims=True) + inter_w[h] * qn[h] for h in heads}
        hh = {h: num[h] / jnp.maximum(jnp.abs(den[h]), jnp.exp(-m_t[h])) for h in heads}

        w_log = {h: g_tot[h] - b_col[h] + i_col[h] for h in heads}
        m_new = {h: jnp.maximum(g_tot[h] + m_prev[h], jnp.max(w_log[h], axis=0, keepdims=True))
                 for h in heads}
        decay = {h: jnp.exp(g_tot[h] + m_prev[h] - m_new[h]) for h in heads}
        kw = {h: k[h].astype(F32) * jnp.exp(w_log[h] - m_new[h]) for h in heads}
        for h in heads:
            c_ref[h] = decay[h] * c_prev[h] + lax.dot_general(
                kw[h].astype(BF16), v[h], (((0,), (0,)), ((), ())), preferred_element_type=F32)
            n_ref[h, 0:1, :] = decay[h] * n_prev[h] + jnp.sum(kw[h], axis=0, keepdims=True)
            m_ref[h] = jnp.broadcast_to(m_new[h], m_ref.shape[1:])

        return hh

    def head_norm_gate(heads, hh):
        mu = {h: jnp.mean(hh[h], axis=-1, keepdims=True) for h in heads}
        hc = {h: hh[h] - mu[h] for h in heads}
        var = {h: jnp.mean(hc[h] * hc[h], axis=-1, keepdims=True) for h in heads}
        for h in heads:
            hn = hc[h] * lax.rsqrt(var[h] + HEAD_NORM_EPS) * gain_ref[:, cols[h]]
            o = o_ref[:, cols[h]].astype(F32)
            ym_ref[:, cols[h]] = (jax.nn.sigmoid(o) * hn).astype(ym_ref.dtype)

    groups = [tuple(range(h0, min(h0 + HEAD_GROUP, nh))) for h0 in range(0, nh, HEAD_GROUP)]
    for g_idx, group in enumerate(groups):
        head_norm_gate(group, recurrence(group))
        if g_idx == 0:
            z = alpha * x_ref[...] + mix
            x1_ref[...] = _layer_norm(z, lng_ref[...], lnb_ref[...])

    gate_terms(xbn_ref[...])


def _mlstm_outproj(qkvo, xb2d, w_gates_t_bf, b_gates_row, gain_row, x2d, y_conv, w_out_bf,
                   ln_g, ln_b, *, seq, nh, dh, alpha):
    m, d = x2d.shape
    w = nh * dh
    wc = y_conv.shape[1]
    assert wc + w == w_out_bf.shape[0] and wc == w
    chunk = MLSTM_CHUNK
    n_chunks = m // chunk
    last = n_chunks - 1

    def cur(s):
        return jnp.minimum(s, last)

    def prev(s):
        return jnp.maximum(s - 1, 0)

    def blk(kind):
        return pl.BlockSpec((chunk, w), lambda s: (cur(s), kind))

    const = pl.Buffered(1)
    piece = OUT_COL_BLOCK
    assert d % piece == 0
    n_split = d // piece
    kern = functools.partial(_mlstm_outproj_kernel, nh=nh, dh=dh, chunks_per_seq=seq // chunk,
                             alpha=alpha, n_split=n_split)
    return pl.pallas_call(
        kern,
        out_shape=jax.ShapeDtypeStruct((m, d), F32),
        grid=(n_chunks + 1,),
        in_specs=[blk(0), blk(1), blk(2), blk(3),
                  pl.BlockSpec((chunk, d), lambda s: (0, 0)),
                  pl.BlockSpec((chunk, d), lambda s: (jnp.minimum(s + 1, last), 0)),
                  pl.BlockSpec((LANES, d), lambda s: (0, 0)),
                  pl.BlockSpec((1, LANES), lambda s: (0, 0)),
                  pl.BlockSpec((1, w), lambda s: (0, 0)),
                  pl.BlockSpec((chunk, d), lambda s: (prev(s), 0)),
                  pl.BlockSpec((chunk, wc), lambda s: (prev(s), 0)),
                  *[pl.BlockSpec((wc, piece), lambda s, p=p: (0, p), pipeline_mode=const)
                    for p in range(n_split)],
                  *[pl.BlockSpec((w, piece), lambda s, p=p: (1, p), pipeline_mode=const)
                    for p in range(n_split)],
                  pl.BlockSpec((1, d), lambda s: (0, 0)),
                  pl.BlockSpec((1, d), lambda s: (0, 0))],
        out_specs=pl.BlockSpec((chunk, d), lambda s: (prev(s), 0)),
        scratch_shapes=[
            pltpu.VMEM((nh, dh, dh), F32),
            pltpu.VMEM((nh, SUBLANES, dh), F32),
            pltpu.VMEM((nh, SUBLANES, LANES), F32),
            pltpu.VMEM((2, chunk, LANES), F32),
            pltpu.VMEM((2, LANES, chunk), F32),
            pltpu.VMEM((chunk, w), BF16),
        ],
        compiler_params=_params(("arbitrary",)),
        name="mlstm_outproj_ln",
    )(qkvo, qkvo, qkvo, qkvo, xb2d, xb2d, w_gates_t_bf, b_gates_row, gain_row,
      x2d, y_conv, *([w_out_bf] * (2 * n_split)), ln_g, ln_b)


def _convffn_ln_kernel(x_ref, wup_hbm, cw_ref, cb_ref, wdn_hbm,
                       g_ref, b_ref, o_ref, xb_ref, acc_ref, carry_v_ref, carry_g_ref,
                       buf_v_ref, buf_g_ref, wv_ring, wg_ring, wd_ring, sems, *,
                       alpha, tiles_per_seq, chunk, nf):
    i = pl.program_id(0)
    n_steps = pl.num_programs(0) * nf
    rows = x_ref.shape[0]
    tf = wv_ring.shape[2]
    first = (i % tiles_per_seq) == 0

    def weight_copies(to_slot, blk):
        return (
            pltpu.make_async_copy(wup_hbm.at[blk], wv_ring.at[to_slot], sems.at[0, to_slot]),
            pltpu.make_async_copy(wup_hbm.at[nf + blk], wg_ring.at[to_slot], sems.at[1, to_slot]),
            pltpu.make_async_copy(wdn_hbm.at[pl.ds(pl.multiple_of(blk * tf, tf), tf), :],
                                  wd_ring.at[to_slot], sems.at[2, to_slot]),
        )

    @pl.when(i == 0)
    def _():
        for ahead in range(WEIGHT_SLOTS - 1):
            for copy in weight_copies(ahead, ahead % nf):
                copy.start()

    def conv(buf_ref, cols, r0):
        h = buf_ref[SUBLANES + r0:SUBLANES + r0 + chunk, :]
        hm1 = buf_ref[SUBLANES - 1 + r0:SUBLANES - 1 + r0 + chunk, :]
        hm2 = buf_ref[SUBLANES - 2 + r0:SUBLANES - 2 + r0 + chunk, :]
        w = cw_ref[:, cols]
        return w[0:1, :] * hm2 + w[1:2, :] * hm1 + w[2:3, :] * h + cb_ref[:, cols]

    def block_step(f, is_first, is_last):
        step_id = i * nf + f
        slot = step_id % WEIGHT_SLOTS
        for copy in weight_copies(slot, f):
            copy.wait()

        @pl.when(step_id + (WEIGHT_SLOTS - 1) < n_steps)
        def _():
            ahead = step_id + (WEIGHT_SLOTS - 1)
            for copy in weight_copies(ahead % WEIGHT_SLOTS, ahead % nf):
                copy.start()

        wv_ref = wv_ring.at[slot]
        wg_ref = wg_ring.at[slot]
        wd_ref = wd_ring.at[slot]
        cols_v = pl.ds(pl.multiple_of(f * tf, tf), tf)
        cols_g = pl.ds(pl.multiple_of((nf + f) * tf, tf), tf)

        @pl.when(first)
        def _():
            carry_v_ref[f] = jnp.zeros(carry_v_ref.shape[1:], F32)
            carry_g_ref[f] = jnp.zeros(carry_g_ref.shape[1:], F32)

        row_starts = list(range(0, rows, chunk))
        buf_v_ref[0:SUBLANES, :] = carry_v_ref[f]
        buf_g_ref[0:SUBLANES, :] = carry_g_ref[f]
        for r0 in row_starts:
            if is_first:
                xb_ref[r0:r0 + chunk, :] = x_ref[r0:r0 + chunk, :].astype(BF16)
            xb = xb_ref[r0:r0 + chunk, :]
            buf_v_ref[SUBLANES + r0:SUBLANES + r0 + chunk, :] = _dot(xb, wv_ref[...])
            buf_g_ref[SUBLANES + r0:SUBLANES + r0 + chunk, :] = _dot(xb, wg_ref[...])
        for r0 in row_starts:
            rs = slice(r0, r0 + chunk)
            val = conv(buf_v_ref, cols_v, r0)
            gate = conv(buf_g_ref, cols_g, r0)
            act = (jax.nn.silu(gate) * val).astype(BF16)
            part = _dot(act, wd_ref[...])
            if is_first:
                acc_ref[rs, :] = part
            elif is_last:
                z = alpha * x_ref[rs, :] + (acc_ref[rs, :] + part)
                o_ref[rs, :] = _layer_norm(z, g_ref[...], b_ref[...])
            else:
                acc_ref[rs, :] += part
        carry_v_ref[f] = buf_v_ref[rows:rows + SUBLANES, :]
        carry_g_ref[f] = buf_g_ref[rows:rows + SUBLANES, :]

    block_step(0, True, False)

    @pl.loop(1, nf - 1)
    def _(f):
        block_step(f, False, False)

    block_step(nf - 1, False, True)


def _convffn_ln(x2d, w_up_bf, w_ffn_conv, b_ffn_conv_row, w_down_bf, ln_g, ln_b, *,
                alpha, seq, d_ff):
    m, d = x2d.shape
    tm, tf = TM_FFN, TF_FFN
    nf = d_ff // tf
    kern = functools.partial(_convffn_ln_kernel, alpha=alpha, tiles_per_seq=seq // tm,
                             chunk=TR_FFN, nf=nf)
    return pl.pallas_call(
        kern,
        out_shape=jax.ShapeDtypeStruct((m, d), F32),
        grid=(m // tm,),
        in_specs=[
            pl.BlockSpec((tm, d), lambda i: (i, 0)),
            pl.BlockSpec(memory_space=pl.ANY),
            pl.BlockSpec((CONV_WIDTH, 2 * d_ff), lambda i: (0, 0)),
            pl.BlockSpec((1, 2 * d_ff), lambda i: (0, 0)),
            pl.BlockSpec(memory_space=pl.ANY),
            pl.BlockSpec((1, d), lambda i: (0, 0)),
            pl.BlockSpec((1, d), lambda i: (0, 0)),
        ],
        out_specs=pl.BlockSpec((tm, d), lambda i: (i, 0)),
        scratch_shapes=[
            pltpu.VMEM((tm, d), BF16),
            pltpu.VMEM((tm, d), F32),
            pltpu.VMEM((nf, SUBLANES, tf), F32),
            pltpu.VMEM((nf, SUBLANES, tf), F32),
            pltpu.VMEM((tm + SUBLANES, tf), F32),
            pltpu.VMEM((tm + SUBLANES, tf), F32),
            pltpu.VMEM((WEIGHT_SLOTS, d, tf), BF16),
            pltpu.VMEM((WEIGHT_SLOTS, d, tf), BF16),
            pltpu.VMEM((WEIGHT_SLOTS, tf, d), BF16),
            pltpu.SemaphoreType.DMA((N_WEIGHT_STREAMS, WEIGHT_SLOTS)),
        ],
        compiler_params=_params(("arbitrary",)),
        name="convffn_ln",
    )(x2d, w_up_bf, w_ffn_conv, b_ffn_conv_row, w_down_bf, ln_g, ln_b)


def kernel(x, w_in, b_gates, w_sc_conv, mh_gain, w_out, ln1_g, ln1_b,
           w_up, w_ffn_conv, b_ffn_conv, w_down, ln2_g, ln2_b):
    batch, seq, d = x.shape
    depth = w_in.shape[0]
    nh = N_HEADS_M
    w_conv = w_sc_conv.shape[-1]
    w_mlstm = mh_gain.shape[-1]
    dh = w_mlstm // nh
    d_ff = w_down.shape[1]
    n_gates = 2 * nh
    gate_col0 = 3 * w_conv + 4 * w_mlstm
    assert w_in.shape[-1] == gate_col0 + n_gates and w_conv + w_mlstm == d
    assert seq % TM_CONV == 0 and seq % TM_FFN == 0 and seq % MLSTM_CHUNK == 0
    assert w_conv % TC_CONV == 0 and d_ff % TF_FFN == 0
    assert (3 * w_conv) % TN_QKVO == 0 and (4 * w_mlstm) % TN_QKVO == 0
    assert (batch * seq) % TM_QKVO == 0
    alpha = (2 * depth) ** 0.25

    x2d = x.reshape(batch * seq, d)
    for l in range(depth):
        w_in_t = w_in[l].T
        w_gates_t_bf = jnp.pad(w_in_t[gate_col0:], ((0, LANES - n_gates), (0, 0))).astype(BF16)
        b_gates_row = jnp.pad(b_gates[l], (0, LANES - n_gates)).reshape(1, LANES)

        y_conv, xb2d, w_down_bf, w_out_bf = _inproj_conv(
            x2d, w_in_t, w_sc_conv[l], w_down[l], w_out[l], seq=seq, w_conv=w_conv)
        qkvo, w_up_bf = _inproj_qkvo(xb2d, w_in_t, w_up[l], col0=3 * w_conv, width=4 * w_mlstm,
                                     tf=TF_FFN)
        x2d = _mlstm_outproj(qkvo, xb2d, w_gates_t_bf, b_gates_row, mh_gain[l].reshape(1, w_mlstm),
                             x2d, y_conv, w_out_bf, ln1_g[l].reshape(1, d), ln1_b[l].reshape(1, d),
                             seq=seq, nh=nh, dh=dh, alpha=alpha)
        x2d = _convffn_ln(x2d, w_up_bf, w_ffn_conv[l],
                          b_ffn_conv[l].reshape(1, 2 * d_ff), w_down_bf,
                          ln2_g[l].reshape(1, d), ln2_b[l].reshape(1, d),
                          alpha=alpha, seq=seq, d_ff=d_ff)
    return x2d.reshape(batch, seq, d)
```

```python
import functools

import jax
import jax.numpy as jnp
from jax import lax
from jax.experimental import pallas as pl
from jax.experimental.pallas import tpu as pltpu

F32 = jnp.float32
BF16 = jnp.bfloat16

N_HEADS_M = 4
CONV_WIDTH = 3
LN_EPS = 1e-5
HEAD_NORM_EPS = 1e-6

SUBLANES = 8
LANES = 128
BF16_SUBLANES = 16
VMEM_LIMIT_BYTES = 56 * 1024 * 1024

TM_CONV = 512
TC_CONV = 512
TR_CONV = 256
TM_QKVO = 1024
TN_QKVO = 1024
MLSTM_CHUNK = 256
HEAD_GROUP = 2
OUT_COL_BLOCK = 512
TM_FFN = 512
TF_FFN = 512
TR_FFN = 256
WEIGHT_SLOTS = 4
N_WEIGHT_STREAMS = 3


def _dot(a, b):
    return jnp.dot(a, b, preferred_element_type=F32)


def _dot_t(a, b_t):
    return lax.dot_general(a, b_t, (((1,), (1,)), ((), ())), preferred_element_type=F32)


def _params(semantics):
    return pltpu.CompilerParams(dimension_semantics=semantics,
                                vmem_limit_bytes=VMEM_LIMIT_BYTES)


def _cast_rows_spec(rows, cols, n_steps, step_of):
    assert rows % n_steps == 0 and (rows // n_steps) % BF16_SUBLANES == 0
    return pl.BlockSpec((rows // n_steps, cols), lambda *ids: (step_of(*ids), 0))


def _inproj_conv_kernel(x_ref, wb_ref, wc_ref, wh_ref, cw_ref, wdn_ref, wout_ref,
                        y_ref, xb_ref, wdn_bf_ref, wout_bf_ref,
                        wbf_ref, carry_ref, buf_ref, *, tiles_per_seq, chunk):
    i = pl.program_id(1)
    rows = x_ref.shape[0]

    @pl.when(i == 0)
    def _():
        wbf_ref[0] = wb_ref[...].astype(BF16)
        wbf_ref[1] = wc_ref[...].astype(BF16)
        wbf_ref[2] = wh_ref[...].astype(BF16)

    @pl.when(pl.program_id(0) == 0)
    def _():
        xb_ref[...] = x_ref[...].astype(BF16)

    @pl.when(i % tiles_per_seq == 0)
    def _():
        carry_ref[...] = jnp.zeros(carry_ref.shape, F32)

    wdn_bf_ref[...] = wdn_ref[...].astype(BF16)
    wout_bf_ref[...] = wout_ref[...].astype(BF16)

    buf_ref[0:SUBLANES, :] = carry_ref[...]
    w = cw_ref[...]
    row_starts = list(range(0, rows, chunk))
    xbs = [x_ref[r0:r0 + chunk, :].astype(BF16) for r0 in row_starts]
    for r0, xb in zip(row_starts, xbs):
        u = _dot_t(xb, wbf_ref[1]) * _dot_t(xb, wbf_ref[2])
        buf_ref[SUBLANES + r0:SUBLANES + r0 + chunk, :] = u
    c_bs = [_dot_t(xb, wbf_ref[0]) for xb in xbs]
    for r0, c_b in zip(row_starts, c_bs):
        u = buf_ref[SUBLANES + r0:SUBLANES + r0 + chunk, :]
        um2 = buf_ref[SUBLANES - 2 + r0:SUBLANES - 2 + r0 + chunk, :]
        um1 = buf_ref[SUBLANES - 1 + r0:SUBLANES - 1 + r0 + chunk, :]
        y = w[0:1, :] * um2 + w[1:2, :] * um1 + w[2:3, :] * u
        y_ref[r0:r0 + chunk, :] = (c_b * y).astype(y_ref.dtype)
    carry_ref[...] = buf_ref[rows:rows + SUBLANES, :]


def _inproj_conv(x2d, w_in_t, w_sc_conv, w_down, w_out, *, seq, w_conv):
    m, d = x2d.shape
    tm, tc = TM_CONV, TC_CONV
    nj, ni = w_conv // tc, m // tm
    n_steps = nj * ni
    step_of = lambda j, i: j * ni + i
    once = pl.Buffered(1)
    kern = functools.partial(_inproj_conv_kernel, tiles_per_seq=seq // tm, chunk=TR_CONV)
    return pl.pallas_call(
        kern,
        out_shape=(jax.ShapeDtypeStruct((m, w_conv), BF16),
                   jax.ShapeDtypeStruct((m, d), BF16),
                   jax.ShapeDtypeStruct(w_down.shape, BF16),
                   jax.ShapeDtypeStruct(w_out.shape, BF16)),
        grid=(nj, ni),
        in_specs=[
            pl.BlockSpec((tm, d), lambda j, i: (i, 0)),
            pl.BlockSpec((tc, d), lambda j, i: (j, 0), pipeline_mode=once),
            pl.BlockSpec((tc, d), lambda j, i: (nj + j, 0), pipeline_mode=once),
            pl.BlockSpec((tc, d), lambda j, i: (2 * nj + j, 0), pipeline_mode=once),
            pl.BlockSpec((CONV_WIDTH, tc), lambda j, i: (0, j)),
            _cast_rows_spec(*w_down.shape, n_steps, step_of),
            _cast_rows_spec(*w_out.shape, n_steps, step_of),
        ],
        out_specs=(pl.BlockSpec((tm, tc), lambda j, i: (i, j)),
                   pl.BlockSpec((tm, d), lambda j, i: (jnp.where(j == 0, i, ni - 1), 0)),
                   _cast_rows_spec(*w_down.shape, n_steps, step_of),
                   _cast_rows_spec(*w_out.shape, n_steps, step_of)),
        scratch_shapes=[
            pltpu.VMEM((3, tc, d), BF16),
            pltpu.VMEM((SUBLANES, tc), F32),
            pltpu.VMEM((tm + SUBLANES, tc), F32),
        ],
        compiler_params=_params(("arbitrary", "arbitrary")),
        name="inproj_shortconv",
    )(x2d, w_in_t, w_in_t, w_in_t, w_sc_conv, w_down, w_out)


def _inproj_qkvo_kernel(xb_ref, w_ref, wup_ref, o_ref, wup_bf_ref, wbf_ref):
    @pl.when(pl.program_id(1) == 0)
    def _():
        wbf_ref[...] = w_ref[...].astype(BF16)

    n_blocks, _, tf = wup_bf_ref.shape
    for n in range(n_blocks):
        wup_bf_ref[n] = wup_ref[:, n * tf:(n + 1) * tf].astype(BF16)
    o_ref[...] = _dot_t(xb_ref[...], wbf_ref[...]).astype(o_ref.dtype)


def _inproj_qkvo(xb2d, w_in_t, w_up, *, col0, width, tf):
    m, d = xb2d.shape
    tm, tn = TM_QKVO, TN_QKVO
    j0 = col0 // tn
    nj, ni = width // tn, m // tm
    n_steps = nj * ni
    step_of = lambda j, i: j * ni + i
    up_rows, up_cols = w_up.shape
    n_blocks = up_cols // tf
    rows_per_step = up_rows // n_steps
    assert up_cols % tf == 0 and up_rows % n_steps == 0 and rows_per_step % BF16_SUBLANES == 0
    return pl.pallas_call(
        _inproj_qkvo_kernel,
        out_shape=(jax.ShapeDtypeStruct((m, width), BF16),
                   jax.ShapeDtypeStruct((n_blocks, up_rows, tf), BF16)),
        grid=(nj, ni),
        in_specs=[
            pl.BlockSpec((tm, d), lambda j, i: (i, 0)),
            pl.BlockSpec((tn, d), lambda j, i: (j0 + j, 0)),
            _cast_rows_spec(up_rows, up_cols, n_steps, step_of),
        ],
        out_specs=(pl.BlockSpec((tm, tn), lambda j, i: (i, j)),
                   pl.BlockSpec((n_blocks, rows_per_step, tf),
                                lambda j, i: (0, step_of(j, i), 0))),
        scratch_shapes=[pltpu.VMEM((tn, d), BF16)],
        compiler_params=_params(("arbitrary", "arbitrary")),
        name="inproj_mlstm",
    )(xb2d, w_in_t, w_up)


def _log_sigmoid(x):
    return jnp.minimum(x, 0.0) - jnp.log1p(jnp.exp(-jnp.abs(x)))


def _layer_norm(z, g, b):
    mu = jnp.mean(z, axis=-1, keepdims=True)
    zc = z - mu
    var = jnp.mean(zc * zc, axis=-1, keepdims=True)
    return zc * lax.rsqrt(var + LN_EPS) * g + b


def _mlstm_outproj_kernel(*refs, nh, dh, chunks_per_seq, alpha, n_split):
    (q_ref, k_ref, v_ref, o_ref, xb0_ref, xbn_ref, wg_ref, bg_ref, gain_ref,
     x_ref, yc_ref) = refs[:11]
    wc_refs = refs[11:11 + n_split]
    wm_refs = refs[11 + n_split:11 + 2 * n_split]
    (lng_ref, lnb_ref, x1_ref,
     c_ref, n_ref, m_ref, gates_ref, gates_t_ref, ym_ref) = refs[11 + 2 * n_split:]
    chunk = q_ref.shape[0]
    step = pl.program_id(0)

    @pl.when(step % chunks_per_seq == 0)
    def _():
        c_ref[...] = jnp.zeros(c_ref.shape, F32)
        n_ref[...] = jnp.zeros(n_ref.shape, F32)
        m_ref[...] = jnp.zeros(m_ref.shape, F32)

    row = lax.broadcasted_iota(jnp.int32, (chunk, chunk), 0)
    col = lax.broadcasted_iota(jnp.int32, (chunk, chunk), 1)
    causal = row >= col

    def gate_terms(xb):
        g = _dot_t(xb, wg_ref[...]) + bg_ref[...]
        lf = _log_sigmoid(g)
        lf_hi = lf.astype(BF16)
        lf_r1 = lf - lf_hi.astype(F32)
        lf_mid = lf_r1.astype(BF16)
        lf_lo = (lf_r1 - lf_mid.astype(F32)).astype(BF16)
        parts = _dot(causal.astype(BF16), jnp.concatenate([lf_hi, lf_mid, lf_lo], axis=1))
        bcum = parts[:, :LANES] + parts[:, LANES:2 * LANES] + parts[:, 2 * LANES:]
        gates_ref[0] = g
        gates_ref[1] = bcum
        gates_t_ref[0] = g.T
        gates_t_ref[1] = bcum.T

    @pl.when(step == 0)
    def _():
        gate_terms(xb0_ref[...])
        ym_ref[...] = jnp.zeros(ym_ref.shape, ym_ref.dtype)

    g = gates_ref[0]
    bcum = gates_ref[1]
    g_t = gates_t_ref[0]
    bcum_t = gates_t_ref[1]
    neg_inf = jnp.float32(-jnp.inf)
    cols = [slice(h * dh, (h + 1) * dh) for h in range(nh)]

    yc_prev = yc_ref[...]
    ym_prev = ym_ref[...]
    mix = jnp.concatenate(
        [_dot(yc_prev, wc[...]) + _dot(ym_prev, wm[...])
         for wc, wm in zip(wc_refs, wm_refs)], axis=1)

    def recurrence(heads):
        q = {h: q_ref[:, cols[h]] * jnp.asarray(dh ** -0.5, BF16) for h in heads}
        k = {h: k_ref[:, cols[h]] for h in heads}
        v = {h: v_ref[:, cols[h]] for h in heads}
        i_col = {h: g[:, h:h + 1] for h in heads}
        b_col = {h: bcum[:, nh + h:nh + h + 1] for h in heads}
        i_row = {h: g_t[h:h + 1, :] for h in heads}
        b_row = {h: bcum_t[nh + h:nh + h + 1, :] for h in heads}
        g_tot = {h: b_col[h][chunk - 1:chunk, :] for h in heads}
        m_prev = {h: m_ref[h, 0:1, 0:1] for h in heads}
        c_prev = {h: c_ref[h] for h in heads}
        n_prev = {h: n_ref[h, 0:1, :] for h in heads}

        qk = {h: _dot_t(q[h], k[h]) for h in heads}
        inter = {h: _dot(q[h], c_prev[h].astype(BF16)) for h in heads}
        qn = {h: jnp.sum(q[h].astype(F32) * n_prev[h], axis=-1, keepdims=True) for h in heads}

        log_d = {h: jnp.where(causal, b_col[h] - (b_row[h] - i_row[h]), neg_inf) for h in heads}
        inter_log = {h: b_col[h] + m_prev[h] for h in heads}
        m_t = {h: jnp.maximum(inter_log[h], jnp.max(log_d[h], axis=-1, keepdims=True))
               for h in heads}
        scores = {h: qk[h] * jnp.exp(log_d[h] - m_t[h]) for h in heads}
        inter_w = {h: jnp.exp(inter_log[h] - m_t[h]) for h in heads}
        num = {h: _dot(scores[h].astype(BF16), v[h]) + inter_w[h] * inter[h] for h in heads}
        den = {h: jnp.sum(scores[h], axis=-1, keepdims=True) + inter_w[h] * qn[h] for h in heads}
        hh = {h: num[h] / jnp.maximum(jnp.abs(den[h]), jnp.exp(-m_t[h])) for h in heads}

        w_log = {h: g_tot[h] - b_col[h] + i_col[h] for h in heads}
        m_new = {h: jnp.maximum(g_tot[h] + m_prev[h], jnp.max(w_log[h], axis=0, keepdims=True))
                 for h in heads}
        decay = {h: jnp.exp(g_tot[h] + m_prev[h] - m_new[h]) for h in heads}
        kw = {h: k[h].astype(F32) * jnp.exp(w_log[h] - m_new[h]) for h in heads}
        for h in heads:
            c_ref[h] = decay[h] * c_prev[h] + lax.dot_general(
                kw[h].astype(BF16), v[h], (((0,), (0,)), ((), ())), preferred_element_type=F32)
            n_ref[h, 0:1, :] = decay[h] * n_prev[h] + jnp.sum(kw[h], axis=0, keepdims=True)
            m_ref[h] = jnp.broadcast_to(m_new[h], m_ref.shape[1:])

        return hh

    def head_norm_gate(heads, hh):
        mu = {h: jnp.mean(hh[h], axis=-1, keepdims=True) for h in heads}
        hc = {h: hh[h] - mu[h] for h in heads}
        var = {h: jnp.mean(hc[h] * hc[h], axis=-1, keepdims=True) for h in heads}
        for h in heads:
            hn = hc[h] * lax.rsqrt(var[h] + HEAD_NORM_EPS) * gain_ref[:, cols[h]]
            o = o_ref[:, cols[h]].astype(F32)
            ym_ref[:, cols[h]] = (jax.nn.sigmoid(o) * hn).astype(ym_ref.dtype)

    groups = [tuple(range(h0, min(h0 + HEAD_GROUP, nh))) for h0 in range(0, nh, HEAD_GROUP)]
    for g_idx, group in enumerate(groups):
        head_norm_gate(group, recurrence(group))
        if g_idx == 0:
            z = alpha * x_ref[...] + mix
            x1_ref[...] = _layer_norm(z, lng_ref[...], lnb_ref[...])

    gate_terms(xbn_ref[...])


def _mlstm_outproj(qkvo, xb2d, w_gates_t_bf, b_gates_row, gain_row, x2d, y_conv, w_out_bf,
                   ln_g, ln_b, *, seq, nh, dh, alpha):
    m, d = x2d.shape
    w = nh * dh
    wc = y_conv.shape[1]
    assert wc + w == w_out_bf.shape[0] and wc == w
    chunk = MLSTM_CHUNK
    n_chunks = m // chunk
    last = n_chunks - 1

    def cur(s):
        return jnp.minimum(s, last)

    def prev(s):
        return jnp.maximum(s - 1, 0)

    def blk(kind):
        return pl.BlockSpec((chunk, w), lambda s: (cur(s), kind))

    const = pl.Buffered(1)
    piece = OUT_COL_BLOCK
    assert d % piece == 0
    n_split = d // piece
    kern = functools.partial(_mlstm_outproj_kernel, nh=nh, dh=dh, chunks_per_seq=seq // chunk,
                             alpha=alpha, n_split=n_split)
    return pl.pallas_call(
        kern,
        out_shape=jax.ShapeDtypeStruct((m, d), F32),
        grid=(n_chunks + 1,),
        in_specs=[blk(0), blk(1), blk(2), blk(3),
                  pl.BlockSpec((chunk, d), lambda s: (0, 0)),
                  pl.BlockSpec((chunk, d), lambda s: (jnp.minimum(s + 1, last), 0)),
                  pl.BlockSpec((LANES, d), lambda s: (0, 0)),
                  pl.BlockSpec((1, LANES), lambda s: (0, 0)),
                  pl.BlockSpec((1, w), lambda s: (0, 0)),
                  pl.BlockSpec((chunk, d), lambda s: (prev(s), 0)),
                  pl.BlockSpec((chunk, wc), lambda s: (prev(s), 0)),
                  *[pl.BlockSpec((wc, piece), lambda s, p=p: (0, p), pipeline_mode=const)
                    for p in range(n_split)],
                  *[pl.BlockSpec((w, piece), lambda s, p=p: (1, p), pipeline_mode=const)
                    for p in range(n_split)],
                  pl.BlockSpec((1, d), lambda s: (0, 0)),
                  pl.BlockSpec((1, d), lambda s: (0, 0))],
        out_specs=pl.BlockSpec((chunk, d), lambda s: (prev(s), 0)),
        scratch_shapes=[
            pltpu.VMEM((nh, dh, dh), F32),
            pltpu.VMEM((nh, SUBLANES, dh), F32),
            pltpu.VMEM((nh, SUBLANES, LANES), F32),
            pltpu.VMEM((2, chunk, LANES), F32),
            pltpu.VMEM((2, LANES, chunk), F32),
            pltpu.VMEM((chunk, w), BF16),
        ],
        compiler_params=_params(("arbitrary",)),
        name="mlstm_outproj_ln",
    )(qkvo, qkvo, qkvo, qkvo, xb2d, xb2d, w_gates_t_bf, b_gates_row, gain_row,
      x2d, y_conv, *([w_out_bf] * (2 * n_split)), ln_g, ln_b)


def _convffn_ln_kernel(x_ref, wup_hbm, cw_ref, cb_ref, wdn_hbm,
                       g_ref, b_ref, o_ref, xb_ref, acc_ref, carry_v_ref, carry_g_ref,
                       buf_v_ref, buf_g_ref, wv_ring, wg_ring, wd_ring, sems, *,
                       alpha, tiles_per_seq, chunk, nf):
    i = pl.program_id(0)
    n_steps = pl.num_programs(0) * nf
    rows = x_ref.shape[0]
    tf = wv_ring.shape[2]
    first = (i % tiles_per_seq) == 0

    def weight_copies(to_slot, blk):
        return (
            pltpu.make_async_copy(wup_hbm.at[blk], wv_ring.at[to_slot], sems.at[0, to_slot]),
            pltpu.make_async_copy(wup_hbm.at[nf + blk], wg_ring.at[to_slot], sems.at[1, to_slot]),
            pltpu.make_async_copy(wdn_hbm.at[pl.ds(pl.multiple_of(blk * tf, tf), tf), :],
                                  wd_ring.at[to_slot], sems.at[2, to_slot]),
        )

    @pl.when(i == 0)
    def _():
        for ahead in range(WEIGHT_SLOTS - 1):
            for copy in weight_copies(ahead, ahead % nf):
                copy.start()

    def conv(buf_ref, cols, r0):
        h = buf_ref[SUBLANES + r0:SUBLANES + r0 + chunk, :]
        hm1 = buf_ref[SUBLANES - 1 + r0:SUBLANES - 1 + r0 + chunk, :]
        hm2 = buf_ref[SUBLANES - 2 + r0:SUBLANES - 2 + r0 + chunk, :]
        w = cw_ref[:, cols]
        return w[0:1, :] * hm2 + w[1:2, :] * hm1 + w[2:3, :] * h + cb_ref[:, cols]

    def block_step(f, is_first, is_last):
        step_id = i * nf + f
        slot = step_id % WEIGHT_SLOTS
        for copy in weight_copies(slot, f):
            copy.wait()

        @pl.when(step_id + (WEIGHT_SLOTS - 1) < n_steps)
        def _():
            ahead = step_id + (WEIGHT_SLOTS - 1)
            for copy in weight_copies(ahead % WEIGHT_SLOTS, ahead % nf):
                copy.start()

        wv_ref = wv_ring.at[slot]
        wg_ref = wg_ring.at[slot]
        wd_ref = wd_ring.at[slot]
        cols_v = pl.ds(pl.multiple_of(f * tf, tf), tf)
        cols_g = pl.ds(pl.multiple_of((nf + f) * tf, tf), tf)

        @pl.when(first)
        def _():
            carry_v_ref[f] = jnp.zeros(carry_v_ref.shape[1:], F32)
            carry_g_ref[f] = jnp.zeros(carry_g_ref.shape[1:], F32)

        row_starts = list(range(0, rows, chunk))
        buf_v_ref[0:SUBLANES, :] = carry_v_ref[f]
        buf_g_ref[0:SUBLANES, :] = carry_g_ref[f]
        for r0 in row_starts:
            if is_first:
                xb_ref[r0:r0 + chunk, :] = x_ref[r0:r0 + chunk, :].astype(BF16)
            xb = xb_ref[r0:r0 + chunk, :]
            buf_v_ref[SUBLANES + r0:SUBLANES + r0 + chunk, :] = _dot(xb, wv_ref[...])
            buf_g_ref[SUBLANES + r0:SUBLANES + r0 + chunk, :] = _dot(xb, wg_ref[...])
        for r0 in row_starts:
            rs = slice(r0, r0 + chunk)
            val = conv(buf_v_ref, cols_v, r0)
            gate = conv(buf_g_ref, cols_g, r0)
            act = (jax.nn.silu(gate) * val).astype(BF16)
            part = _dot(act, wd_ref[...])
            if is_first:
                acc_ref[rs, :] = part
            elif is_last:
                z = alpha * x_ref[rs, :] + (acc_ref[rs, :] + part)
                o_ref[rs, :] = _layer_norm(z, g_ref[...], b_ref[...])
            else:
                acc_ref[rs, :] += part
        carry_v_ref[f] = buf_v_ref[rows:rows + SUBLANES, :]
        carry_g_ref[f] = buf_g_ref[rows:rows + SUBLANES, :]

    block_step(0, True, False)

    @pl.loop(1, nf - 1)
    def _(f):
        block_step(f, False, False)

    block_step(nf - 1, False, True)


def _convffn_ln(x2d, w_up_bf, w_ffn_conv, b_ffn_conv_row, w_down_bf, ln_g, ln_b, *,
                alpha, seq, d_ff):
    m, d = x2d.shape
    tm, tf = TM_FFN, TF_FFN
    nf = d_ff // tf
    kern = functools.partial(_convffn_ln_kernel, alpha=alpha, tiles_per_seq=seq // tm,
                             chunk=TR_FFN, nf=nf)
    return pl.pallas_call(
        kern,
        out_shape=jax.ShapeDtypeStruct((m, d), F32),
        grid=(m // tm,),
        in_specs=[
            pl.BlockSpec((tm, d), lambda i: (i, 0)),
            pl.BlockSpec(memory_space=pl.ANY),
            pl.BlockSpec((CONV_WIDTH, 2 * d_ff), lambda i: (0, 0)),
            pl.BlockSpec((1, 2 * d_ff), lambda i: (0, 0)),
            pl.BlockSpec(memory_space=pl.ANY),
            pl.BlockSpec((1, d), lambda i: (0, 0)),
            pl.BlockSpec((1, d), lambda i: (0, 0)),
        ],
        out_specs=pl.BlockSpec((tm, d), lambda i: (i, 0)),
        scratch_shapes=[
            pltpu.VMEM((tm, d), BF16),
            pltpu.VMEM((tm, d), F32),
            pltpu.VMEM((nf, SUBLANES, tf), F32),
            pltpu.VMEM((nf, SUBLANES, tf), F32),
            pltpu.VMEM((tm + SUBLANES, tf), F32),
            pltpu.VMEM((tm + SUBLANES, tf), F32),
            pltpu.VMEM((WEIGHT_SLOTS, d, tf), BF16),
            pltpu.VMEM((WEIGHT_SLOTS, d, tf), BF16),
            pltpu.VMEM((WEIGHT_SLOTS, tf, d), BF16),
            pltpu.SemaphoreType.DMA((N_WEIGHT_STREAMS, WEIGHT_SLOTS)),
        ],
        compiler_params=_params(("arbitrary",)),
        name="convffn_ln",
    )(x2d, w_up_bf, w_ffn_conv, b_ffn_conv_row, w_down_bf, ln_g, ln_b)


def kernel(x, w_in, b_gates, w_sc_conv, mh_gain, w_out, ln1_g, ln1_b,
           w_up, w_ffn_conv, b_ffn_conv, w_down, ln2_g, ln2_b):
    batch, seq, d = x.shape
    depth = w_in.shape[0]
    nh = N_HEADS_M
    w_conv = w_sc_conv.shape[-1]
    w_mlstm = mh_gain.shape[-1]
    dh = w_mlstm // nh
    d_ff = w_down.shape[1]
    n_gates = 2 * nh
    gate_col0 = 3 * w_conv + 4 * w_mlstm
    assert w_in.shape[-1] == gate_col0 + n_gates and w_conv + w_mlstm == d
    assert seq % TM_CONV == 0 and seq % TM_FFN == 0 and seq % MLSTM_CHUNK == 0
    assert w_conv % TC_CONV == 0 and d_ff % TF_FFN == 0
    assert (3 * w_conv) % TN_QKVO == 0 and (4 * w_mlstm) % TN_QKVO == 0
    assert (batch * seq) % TM_QKVO == 0
    alpha = (2 * depth) ** 0.25

    x2d = x.reshape(batch * seq, d)
    for l in range(depth):
        w_in_t = w_in[l].T
        w_gates_t_bf = jnp.pad(w_in_t[gate_col0:], ((0, LANES - n_gates), (0, 0))).astype(BF16)
        b_gates_row = jnp.pad(b_gates[l], (0, LANES - n_gates)).reshape(1, LANES)

        y_conv, xb2d, w_down_bf, w_out_bf = _inproj_conv(
            x2d, w_in_t, w_sc_conv[l], w_down[l], w_out[l], seq=seq, w_conv=w_conv)
        qkvo, w_up_bf = _inproj_qkvo(xb2d, w_in_t, w_up[l], col0=3 * w_conv, width=4 * w_mlstm,
                                     tf=TF_FFN)
        x2d = _mlstm_outproj(qkvo, xb2d, w_gates_t_bf, b_gates_row, mh_gain[l].reshape(1, w_mlstm),
                             x2d, y_conv, w_out_bf, ln1_g[l].reshape(1, d), ln1_b[l].reshape(1, d),
                             seq=seq, nh=nh, dh=dh, alpha=alpha)
        x2d = _convffn_ln(x2d, w_up_bf, w_ffn_conv[l],
                          b_ffn_conv[l].reshape(1, 2 * d_ff), w_down_bf,
                          ln2_g[l].reshape(1, d), ln2_b[l].reshape(1, d),
                          alpha=alpha, seq=seq, d_ff=d_ff)
    return x2d.reshape(batch, seq, d)
```

```python
import functools

import jax
import jax.numpy as jnp
from jax import lax
from jax.experimental import pallas as pl
from jax.experimental.pallas import tpu as pltpu

F32 = jnp.float32
BF16 = jnp.bfloat16

N_HEADS_M = 4
CONV_WIDTH = 3
LN_EPS = 1e-5
HEAD_NORM_EPS = 1e-6

SUBLANES = 8
LANES = 128
BF16_SUBLANES = 16
VMEM_LIMIT_BYTES = 56 * 1024 * 1024

TM_CONV = 512
TC_CONV = 512
TR_CONV = 256
TM_QKVO = 2048
TN_QKVO = 512
MLSTM_CHUNK = 256
HEAD_GROUP = 2
OUT_COL_BLOCK = 512
TM_FFN = 512
TF_FFN = 512
TR_FFN = 256
WEIGHT_SLOTS = 3
N_WEIGHT_STREAMS = 3


def _dot(a, b):
    return jnp.dot(a, b, preferred_element_type=F32)


def _dot_t(a, b_t):
    return lax.dot_general(a, b_t, (((1,), (1,)), ((), ())), preferred_element_type=F32)


def _params(semantics):
    return pltpu.CompilerParams(dimension_semantics=semantics,
                                vmem_limit_bytes=VMEM_LIMIT_BYTES)


def _cast_rows_spec(rows, cols, n_steps, step_of):
    assert rows % n_steps == 0 and (rows // n_steps) % BF16_SUBLANES == 0
    return pl.BlockSpec((rows // n_steps, cols), lambda *ids: (step_of(*ids), 0))


def _inproj_conv_kernel(x_ref, wb_ref, wc_ref, wh_ref, cw_ref, wdn_ref, wout_ref,
                        y_ref, xb_ref, wdn_bf_ref, wout_bf_ref,
                        wbf_ref, carry_ref, buf_ref, *, tiles_per_seq, chunk):
    i = pl.program_id(1)
    rows = x_ref.shape[0]

    @pl.when(i == 0)
    def _():
        wbf_ref[0] = wb_ref[...].astype(BF16)
        wbf_ref[1] = wc_ref[...].astype(BF16)
        wbf_ref[2] = wh_ref[...].astype(BF16)

    @pl.when(pl.program_id(0) == 0)
    def _():
        xb_ref[...] = x_ref[...].astype(BF16)

    @pl.when(i % tiles_per_seq == 0)
    def _():
        carry_ref[...] = jnp.zeros(carry_ref.shape, F32)

    wdn_bf_ref[...] = wdn_ref[...].astype(BF16)
    wout_bf_ref[...] = wout_ref[...].astype(BF16)

    buf_ref[0:SUBLANES, :] = carry_ref[...]
    w = cw_ref[...]
    row_starts = list(range(0, rows, chunk))
    xbs = [x_ref[r0:r0 + chunk, :].astype(BF16) for r0 in row_starts]
    for r0, xb in zip(row_starts, xbs):
        u = _dot_t(xb, wbf_ref[1]) * _dot_t(xb, wbf_ref[2])
        buf_ref[SUBLANES + r0:SUBLANES + r0 + chunk, :] = u
    c_bs = [_dot_t(xb, wbf_ref[0]) for xb in xbs]
    for r0, c_b in zip(row_starts, c_bs):
        u = buf_ref[SUBLANES + r0:SUBLANES + r0 + chunk, :]
        um2 = buf_ref[SUBLANES - 2 + r0:SUBLANES - 2 + r0 + chunk, :]
        um1 = buf_ref[SUBLANES - 1 + r0:SUBLANES - 1 + r0 + chunk, :]
        y = w[0:1, :] * um2 + w[1:2, :] * um1 + w[2:3, :] * u
        y_ref[r0:r0 + chunk, :] = (c_b * y).astype(y_ref.dtype)
    carry_ref[...] = buf_ref[rows:rows + SUBLANES, :]


def _inproj_conv(x2d, w_in_t, w_sc_conv, w_down, w_out, *, seq, w_conv):
    m, d = x2d.shape
    tm, tc = TM_CONV, TC_CONV
    nj, ni = w_conv // tc, m // tm
    n_steps = nj * ni
    step_of = lambda j, i: j * ni + i
    once = pl.Buffered(1)
    kern = functools.partial(_inproj_conv_kernel, tiles_per_seq=seq // tm, chunk=TR_CONV)
    return pl.pallas_call(
        kern,
        out_shape=(jax.ShapeDtypeStruct((m, w_conv), BF16),
                   jax.ShapeDtypeStruct((m, d), BF16),
                   jax.ShapeDtypeStruct(w_down.shape, BF16),
                   jax.ShapeDtypeStruct(w_out.shape, BF16)),
        grid=(nj, ni),
        in_specs=[
            pl.BlockSpec((tm, d), lambda j, i: (i, 0)),
            pl.BlockSpec((tc, d), lambda j, i: (j, 0), pipeline_mode=once),
            pl.BlockSpec((tc, d), lambda j, i: (nj + j, 0), pipeline_mode=once),
            pl.BlockSpec((tc, d), lambda j, i: (2 * nj + j, 0), pipeline_mode=once),
            pl.BlockSpec((CONV_WIDTH, tc), lambda j, i: (0, j)),
            _cast_rows_spec(*w_down.shape, n_steps, step_of),
            _cast_rows_spec(*w_out.shape, n_steps, step_of),
        ],
        out_specs=(pl.BlockSpec((tm, tc), lambda j, i: (i, j)),
                   pl.BlockSpec((tm, d), lambda j, i: (jnp.where(j == 0, i, ni - 1), 0)),
                   _cast_rows_spec(*w_down.shape, n_steps, step_of),
                   _cast_rows_spec(*w_out.shape, n_steps, step_of)),
        scratch_shapes=[
            pltpu.VMEM((3, tc, d), BF16),
            pltpu.VMEM((SUBLANES, tc), F32),
            pltpu.VMEM((tm + SUBLANES, tc), F32),
        ],
        compiler_params=_params(("arbitrary", "arbitrary")),
        name="inproj_shortconv",
    )(x2d, w_in_t, w_in_t, w_in_t, w_sc_conv, w_down, w_out)


def _inproj_qkvo_kernel(xb_ref, w_ref, wup_ref, o_ref, wup_bf_ref, wbf_ref):
    @pl.when(pl.program_id(1) == 0)
    def _():
        wbf_ref[...] = w_ref[...].astype(BF16)

    n_blocks, _, tf = wup_bf_ref.shape
    for n in range(n_blocks):
        wup_bf_ref[n] = wup_ref[:, n * tf:(n + 1) * tf].astype(BF16)
    o_ref[...] = _dot_t(xb_ref[...], wbf_ref[...]).astype(o_ref.dtype)


def _inproj_qkvo(xb2d, w_in_t, w_up, *, col0, width, tf):
    m, d = xb2d.shape
    tm, tn = TM_QKVO, TN_QKVO
    j0 = col0 // tn
    nj, ni = width // tn, m // tm
    n_steps = nj * ni
    step_of = lambda j, i: j * ni + i
    up_rows, up_cols = w_up.shape
    n_blocks = up_cols // tf
    rows_per_step = up_rows // n_steps
    assert up_cols % tf == 0 and up_rows % n_steps == 0 and rows_per_step % BF16_SUBLANES == 0
    return pl.pallas_call(
        _inproj_qkvo_kernel,
        out_shape=(jax.ShapeDtypeStruct((m, width), BF16),
                   jax.ShapeDtypeStruct((n_blocks, up_rows, tf), BF16)),
        grid=(nj, ni),
        in_specs=[
            pl.BlockSpec((tm, d), lambda j, i: (i, 0)),
            pl.BlockSpec((tn, d), lambda j, i: (j0 + j, 0)),
            _cast_rows_spec(up_rows, up_cols, n_steps, step_of),
        ],
        out_specs=(pl.BlockSpec((tm, tn), lambda j, i: (i, j)),
                   pl.BlockSpec((n_blocks, rows_per_step, tf),
                                lambda j, i: (0, step_of(j, i), 0))),
        scratch_shapes=[pltpu.VMEM((tn, d), BF16)],
        compiler_params=_params(("arbitrary", "arbitrary")),
        name="inproj_mlstm",
    )(xb2d, w_in_t, w_up)


def _log_sigmoid(x):
    return jnp.minimum(x, 0.0) - jnp.log1p(jnp.exp(-jnp.abs(x)))


def _layer_norm(z, g, b):
    mu = jnp.mean(z, axis=-1, keepdims=True)
    zc = z - mu
    var = jnp.mean(zc * zc, axis=-1, keepdims=True)
    return zc * lax.rsqrt(var + LN_EPS) * g + b


def _mlstm_outproj_kernel(*refs, nh, dh, chunks_per_seq, alpha, n_split):
    (q_ref, k_ref, v_ref, o_ref, xb0_ref, xbn_ref, wg_ref, bg_ref, gain_ref,
     x_ref, yc_ref) = refs[:11]
    wc_refs = refs[11:11 + n_split]
    wm_refs = refs[11 + n_split:11 + 2 * n_split]
    (lng_ref, lnb_ref, x1_ref,
     c_ref, n_ref, m_ref, gates_ref, gates_t_ref, ym_ref) = refs[11 + 2 * n_split:]
    chunk = q_ref.shape[0]
    step = pl.program_id(0)

    @pl.when(step % chunks_per_seq == 0)
    def _():
        c_ref[...] = jnp.zeros(c_ref.shape, F32)
        n_ref[...] = jnp.zeros(n_ref.shape, F32)
        m_ref[...] = jnp.zeros(m_ref.shape, F32)

    row = lax.broadcasted_iota(jnp.int32, (chunk, chunk), 0)
    col = lax.broadcasted_iota(jnp.int32, (chunk, chunk), 1)
    causal = row >= col

    def gate_terms(xb):
        g = _dot_t(xb, wg_ref[...]) + bg_ref[...]
        lf = _log_sigmoid(g)
        lf_hi = lf.astype(BF16)
        lf_r1 = lf - lf_hi.astype(F32)
        lf_mid = lf_r1.astype(BF16)
        lf_lo = (lf_r1 - lf_mid.astype(F32)).astype(BF16)
        parts = _dot(causal.astype(BF16), jnp.concatenate([lf_hi, lf_mid, lf_lo], axis=1))
        bcum = parts[:, :LANES] + parts[:, LANES:2 * LANES] + parts[:, 2 * LANES:]
        gates_ref[0] = g
        gates_ref[1] = bcum
        gates_t_ref[0] = g.T
        gates_t_ref[1] = bcum.T

    @pl.when(step == 0)
    def _():
        gate_terms(xb0_ref[...])
        ym_ref[...] = jnp.zeros(ym_ref.shape, ym_ref.dtype)

    g = gates_ref[0]
    bcum = gates_ref[1]
    g_t = gates_t_ref[0]
    bcum_t = gates_t_ref[1]
    neg_inf = jnp.float32(-jnp.inf)
    cols = [slice(h * dh, (h + 1) * dh) for h in range(nh)]

    yc_prev = yc_ref[...]
    ym_prev = ym_ref[...]
    mix = jnp.concatenate(
        [_dot(yc_prev, wc[...]) + _dot(ym_prev, wm[...])
         for wc, wm in zip(wc_refs, wm_refs)], axis=1)

    def recurrence(heads):
        q = {h: q_ref[:, cols[h]] * jnp.asarray(dh ** -0.5, BF16) for h in heads}
        k = {h: k_ref[:, cols[h]] for h in heads}
        v = {h: v_ref[:, cols[h]] for h in heads}
        i_col = {h: g[:, h:h + 1] for h in heads}
        b_col = {h: bcum[:, nh + h:nh + h + 1] for h in heads}
        i_row = {h: g_t[h:h + 1, :] for h in heads}
        b_row = {h: bcum_t[nh + h:nh + h + 1, :] for h in heads}
        g_tot = {h: b_col[h][chunk - 1:chunk, :] for h in heads}
        m_prev = {h: m_ref[h, 0:1, 0:1] for h in heads}
        c_prev = {h: c_ref[h] for h in heads}
        n_prev = {h: n_ref[h, 0:1, :] for h in heads}

        qk = {h: _dot_t(q[h], k[h]) for h in heads}
        inter = {h: _dot(q[h], c_prev[h].astype(BF16)) for h in heads}
        qn = {h: jnp.sum(q[h].astype(F32) * n_prev[h], axis=-1, keepdims=True) for h in heads}

        log_d = {h: jnp.where(causal, b_col[h] - (b_row[h] - i_row[h]), neg_inf) for h in heads}
        inter_log = {h: b_col[h] + m_prev[h] for h in heads}
        m_t = {h: jnp.maximum(inter_log[h], jnp.max(log_d[h], axis=-1, keepdims=True))
               for h in heads}
        scores = {h: qk[h] * jnp.exp(log_d[h] - m_t[h]) for h in heads}
        inter_w = {h: jnp.exp(inter_log[h] - m_t[h]) for h in heads}
        num = {h: _dot(scores[h].astype(BF16), v[h]) + inter_w[h] * inter[h] for h in heads}
        den = {h: jnp.sum(scores[h], axis=-1, keepdims=True) + inter_w[h] * qn[h] for h in heads}
        hh = {h: num[h] / jnp.maximum(jnp.abs(den[h]), jnp.exp(-m_t[h])) for h in heads}

        w_log = {h: g_tot[h] - b_col[h] + i_col[h] for h in heads}
        m_new = {h: jnp.maximum(g_tot[h] + m_prev[h], jnp.max(w_log[h], axis=0, keepdims=True))
                 for h in heads}
        decay = {h: jnp.exp(g_tot[h] + m_prev[h] - m_new[h]) for h in heads}
        kw = {h: k[h].astype(F32) * jnp.exp(w_log[h] - m_new[h]) for h in heads}
        for h in heads:
            c_ref[h] = decay[h] * c_prev[h] + lax.dot_general(
                kw[h].astype(BF16), v[h], (((0,), (0,)), ((), ())), preferred_element_type=F32)
            n_ref[h, 0:1, :] = decay[h] * n_prev[h] + jnp.sum(kw[h], axis=0, keepdims=True)
            m_ref[h] = jnp.broadcast_to(m_new[h], m_ref.shape[1:])

        return hh

    def head_norm_gate(heads, hh):
        mu = {h: jnp.mean(hh[h], axis=-1, keepdims=True) for h in heads}
        hc = {h: hh[h] - mu[h] for h in heads}
        var = {h: jnp.mean(hc[h] * hc[h], axis=-1, keepdims=True) for h in heads}
        for h in heads:
            hn = hc[h] * lax.rsqrt(var[h] + HEAD_NORM_EPS) * gain_ref[:, cols[h]]
            o = o_ref[:, cols[h]].astype(F32)
            ym_ref[:, cols[h]] = (jax.nn.sigmoid(o) * hn).astype(ym_ref.dtype)

    groups = [tuple(range(h0, min(h0 + HEAD_GROUP, nh))) for h0 in range(0, nh, HEAD_GROUP)]
    for g_idx, group in enumerate(groups):
        head_norm_gate(group, recurrence(group))
        if g_idx == 0:
            z = alpha * x_ref[...] + mix
            x1_ref[...] = _layer_norm(z, lng_ref[...], lnb_ref[...])

    gate_terms(xbn_ref[...])


def _mlstm_outproj(qkvo, xb2d, w_gates_t_bf, b_gates_row, gain_row, x2d, y_conv, w_out_bf,
                   ln_g, ln_b, *, seq, nh, dh, alpha):
    m, d = x2d.shape
    w = nh * dh
    wc = y_conv.shape[1]
    assert wc + w == w_out_bf.shape[0] and wc == w
    chunk = MLSTM_CHUNK
    n_chunks = m // chunk
    last = n_chunks - 1

    def cur(s):
        return jnp.minimum(s, last)

    def prev(s):
        return jnp.maximum(s - 1, 0)

    def blk(kind):
        return pl.BlockSpec((chunk, w), lambda s: (cur(s), kind))

    const = pl.Buffered(1)
    piece = OUT_COL_BLOCK
    assert d % piece == 0
    n_split = d // piece
    kern = functools.partial(_mlstm_outproj_kernel, nh=nh, dh=dh, chunks_per_seq=seq // chunk,
                             alpha=alpha, n_split=n_split)
    return pl.pallas_call(
        kern,
        out_shape=jax.ShapeDtypeStruct((m, d), F32),
        grid=(n_chunks + 1,),
        in_specs=[blk(0), blk(1), blk(2), blk(3),
                  pl.BlockSpec((chunk, d), lambda s: (0, 0)),
                  pl.BlockSpec((chunk, d), lambda s: (jnp.minimum(s + 1, last), 0)),
                  pl.BlockSpec((LANES, d), lambda s: (0, 0)),
                  pl.BlockSpec((1, LANES), lambda s: (0, 0)),
                  pl.BlockSpec((1, w), lambda s: (0, 0)),
                  pl.BlockSpec((chunk, d), lambda s: (prev(s), 0)),
                  pl.BlockSpec((chunk, wc), lambda s: (prev(s), 0)),
                  *[pl.BlockSpec((wc, piece), lambda s, p=p: (0, p), pipeline_mode=const)
                    for p in range(n_split)],
                  *[pl.BlockSpec((w, piece), lambda s, p=p: (1, p), pipeline_mode=const)
                    for p in range(n_split)],
                  pl.BlockSpec((1, d), lambda s: (0, 0)),
                  pl.BlockSpec((1, d), lambda s: (0, 0))],
        out_specs=pl.BlockSpec((chunk, d), lambda s: (prev(s), 0)),
        scratch_shapes=[
            pltpu.VMEM((nh, dh, dh), F32),
            pltpu.VMEM((nh, SUBLANES, dh), F32),
            pltpu.VMEM((nh, SUBLANES, LANES), F32),
            pltpu.VMEM((2, chunk, LANES), F32),
            pltpu.VMEM((2, LANES, chunk), F32),
            pltpu.VMEM((chunk, w), BF16),
        ],
        compiler_params=_params(("arbitrary",)),
        name="mlstm_outproj_ln",
    )(qkvo, qkvo, qkvo, qkvo, xb2d, xb2d, w_gates_t_bf, b_gates_row, gain_row,
      x2d, y_conv, *([w_out_bf] * (2 * n_split)), ln_g, ln_b)


def _convffn_ln_kernel(x_ref, wup_hbm, cw_ref, cb_ref, wdn_hbm,
                       g_ref, b_ref, o_ref, xb_ref, acc_ref, carry_v_ref, carry_g_ref,
                       buf_v_ref, buf_g_ref, wv_ring, wg_ring, wd_ring, sems, *,
                       alpha, tiles_per_seq, chunk, nf):
    i = pl.program_id(0)
    n_steps = pl.num_programs(0) * nf
    rows = x_ref.shape[0]
    tf = wv_ring.shape[2]
    first = (i % tiles_per_seq) == 0

    def weight_copies(to_slot, blk):
        return (
            pltpu.make_async_copy(wup_hbm.at[blk], wv_ring.at[to_slot], sems.at[0, to_slot]),
            pltpu.make_async_copy(wup_hbm.at[nf + blk], wg_ring.at[to_slot], sems.at[1, to_slot]),
            pltpu.make_async_copy(wdn_hbm.at[pl.ds(pl.multiple_of(blk * tf, tf), tf), :],
                                  wd_ring.at[to_slot], sems.at[2, to_slot]),
        )

    @pl.when(i == 0)
    def _():
        for ahead in range(WEIGHT_SLOTS - 1):
            for copy in weight_copies(ahead, ahead % nf):
                copy.start()

    def conv(buf_ref, cols, r0):
        h = buf_ref[SUBLANES + r0:SUBLANES + r0 + chunk, :]
        hm1 = buf_ref[SUBLANES - 1 + r0:SUBLANES - 1 + r0 + chunk, :]
        hm2 = buf_ref[SUBLANES - 2 + r0:SUBLANES - 2 + r0 + chunk, :]
        w = cw_ref[:, cols]
        return w[0:1, :] * hm2 + w[1:2, :] * hm1 + w[2:3, :] * h + cb_ref[:, cols]

    def block_step(f, is_first, is_last):
        step_id = i * nf + f
        slot = step_id % WEIGHT_SLOTS
        for copy in weight_copies(slot, f):
            copy.wait()

        @pl.when(step_id + (WEIGHT_SLOTS - 1) < n_steps)
        def _():
            ahead = step_id + (WEIGHT_SLOTS - 1)
            for copy in weight_copies(ahead % WEIGHT_SLOTS, ahead % nf):
                copy.start()

        wv_ref = wv_ring.at[slot]
        wg_ref = wg_ring.at[slot]
        wd_ref = wd_ring.at[slot]
        cols_v = pl.ds(pl.multiple_of(f * tf, tf), tf)
        cols_g = pl.ds(pl.multiple_of((nf + f) * tf, tf), tf)

        @pl.when(first)
        def _():
            carry_v_ref[f] = jnp.zeros(carry_v_ref.shape[1:], F32)
            carry_g_ref[f] = jnp.zeros(carry_g_ref.shape[1:], F32)

        row_starts = list(range(0, rows, chunk))
        buf_v_ref[0:SUBLANES, :] = carry_v_ref[f]
        buf_g_ref[0:SUBLANES, :] = carry_g_ref[f]
        for r0 in row_starts:
            if is_first:
                xb_ref[r0:r0 + chunk, :] = x_ref[r0:r0 + chunk, :].astype(BF16)
            xb = xb_ref[r0:r0 + chunk, :]
            buf_v_ref[SUBLANES + r0:SUBLANES + r0 + chunk, :] = _dot(xb, wv_ref[...])
            buf_g_ref[SUBLANES + r0:SUBLANES + r0 + chunk, :] = _dot(xb, wg_ref[...])
        for r0 in row_starts:
            rs = slice(r0, r0 + chunk)
            val = conv(buf_v_ref, cols_v, r0)
            gate = conv(buf_g_ref, cols_g, r0)
            act = (jax.nn.silu(gate) * val).astype(BF16)
            part = _dot(act, wd_ref[...])
            if is_first:
                acc_ref[rs, :] = part
            elif is_last:
                z = alpha * x_ref[rs, :] + (acc_ref[rs, :] + part)
                o_ref[rs, :] = _layer_norm(z, g_ref[...], b_ref[...])
            else:
                acc_ref[rs, :] += part
        carry_v_ref[f] = buf_v_ref[rows:rows + SUBLANES, :]
        carry_g_ref[f] = buf_g_ref[rows:rows + SUBLANES, :]

    block_step(0, True, False)

    @pl.loop(1, nf - 1)
    def _(f):
        block_step(f, False, False)

    block_step(nf - 1, False, True)


def _convffn_ln(x2d, w_up_bf, w_ffn_conv, b_ffn_conv_row, w_down_bf, ln_g, ln_b, *,
                alpha, seq, d_ff):
    m, d = x2d.shape
    tm, tf = TM_FFN, TF_FFN
    nf = d_ff // tf
    kern = functools.partial(_convffn_ln_kernel, alpha=alpha, tiles_per_seq=seq // tm,
                             chunk=TR_FFN, nf=nf)
    return pl.pallas_call(
        kern,
        out_shape=jax.ShapeDtypeStruct((m, d), F32),
        grid=(m // tm,),
        in_specs=[
            pl.BlockSpec((tm, d), lambda i: (i, 0)),
            pl.BlockSpec(memory_space=pl.ANY),
            pl.BlockSpec((CONV_WIDTH, 2 * d_ff), lambda i: (0, 0)),
            pl.BlockSpec((1, 2 * d_ff), lambda i: (0, 0)),
            pl.BlockSpec(memory_space=pl.ANY),
            pl.BlockSpec((1, d), lambda i: (0, 0)),
            pl.BlockSpec((1, d), lambda i: (0, 0)),
        ],
        out_specs=pl.BlockSpec((tm, d), lambda i: (i, 0)),
        scratch_shapes=[
            pltpu.VMEM((tm, d), BF16),
            pltpu.VMEM((tm, d), F32),
            pltpu.VMEM((nf, SUBLANES, tf), F32),
            pltpu.VMEM((nf, SUBLANES, tf), F32),
            pltpu.VMEM((tm + SUBLANES, tf), F32),
            pltpu.VMEM((tm + SUBLANES, tf), F32),
            pltpu.VMEM((WEIGHT_SLOTS, d, tf), BF16),
            pltpu.VMEM((WEIGHT_SLOTS, d, tf), BF16),
            pltpu.VMEM((WEIGHT_SLOTS, tf, d), BF16),
            pltpu.SemaphoreType.DMA((N_WEIGHT_STREAMS, WEIGHT_SLOTS)),
        ],
        compiler_params=_params(("arbitrary",)),
        name="convffn_ln",
    )(x2d, w_up_bf, w_ffn_conv, b_ffn_conv_row, w_down_bf, ln_g, ln_b)


def kernel(x, w_in, b_gates, w_sc_conv, mh_gain, w_out, ln1_g, ln1_b,
           w_up, w_ffn_conv, b_ffn_conv, w_down, ln2_g, ln2_b):
    batch, seq, d = x.shape
    depth = w_in.shape[0]
    nh = N_HEADS_M
    w_conv = w_sc_conv.shape[-1]
    w_mlstm = mh_gain.shape[-1]
    dh = w_mlstm // nh
    d_ff = w_down.shape[1]
    n_gates = 2 * nh
    gate_col0 = 3 * w_conv + 4 * w_mlstm
    assert w_in.shape[-1] == gate_col0 + n_gates and w_conv + w_mlstm == d
    assert seq % TM_CONV == 0 and seq % TM_FFN == 0 and seq % MLSTM_CHUNK == 0
    assert w_conv % TC_CONV == 0 and d_ff % TF_FFN == 0
    assert (3 * w_conv) % TN_QKVO == 0 and (4 * w_mlstm) % TN_QKVO == 0
    assert (batch * seq) % TM_QKVO == 0
    alpha = (2 * depth) ** 0.25

    x2d = x.reshape(batch * seq, d)
    for l in range(depth):
        w_in_t = w_in[l].T
        w_gates_t_bf = jnp.pad(w_in_t[gate_col0:], ((0, LANES - n_gates), (0, 0))).astype(BF16)
        b_gates_row = jnp.pad(b_gates[l], (0, LANES - n_gates)).reshape(1, LANES)

        y_conv, xb2d, w_down_bf, w_out_bf = _inproj_conv(
            x2d, w_in_t, w_sc_conv[l], w_down[l], w_out[l], seq=seq, w_conv=w_conv)
        qkvo, w_up_bf = _inproj_qkvo(xb2d, w_in_t, w_up[l], col0=3 * w_conv, width=4 * w_mlstm,
                                     tf=TF_FFN)
        x2d = _mlstm_outproj(qkvo, xb2d, w_gates_t_bf, b_gates_row, mh_gain[l].reshape(1, w_mlstm),
                             x2d, y_conv, w_out_bf, ln1_g[l].reshape(1, d), ln1_b[l].reshape(1, d),
                             seq=seq, nh=nh, dh=dh, alpha=alpha)
        x2d = _convffn_ln(x2d, w_up_bf, w_ffn_conv[l],
                          b_ffn_conv[l].reshape(1, 2 * d_ff), w_down_bf,
                          ln2_g[l].reshape(1, d), ln2_b[l].reshape(1, d),
                          alpha=alpha, seq=seq, d_ff=d_ff)
    return x2d.reshape(batch, seq, d)
```

```python
import functools

import jax
import jax.numpy as jnp
from jax import lax
from jax.experimental import pallas as pl
from jax.experimental.pallas import tpu as pltpu

F32 = jnp.float32
BF16 = jnp.bfloat16

N_HEADS_M = 4
CONV_WIDTH = 3
LN_EPS = 1e-5
HEAD_NORM_EPS = 1e-6

SUBLANES = 8
LANES = 128
BF16_SUBLANES = 16
VMEM_LIMIT_BYTES = 56 * 1024 * 1024

TM_CONV = 512
TC_CONV = 512
TR_CONV = 256
TM_QKVO = 1024
TN_QKVO = 1024
INPUT_SLOTS = 3
MLSTM_CHUNK = 256
HEAD_GROUP = 2
OUT_COL_BLOCK = 512
TM_FFN = 512
TF_FFN = 512
TR_FFN = 256
WEIGHT_SLOTS = 3
N_WEIGHT_STREAMS = 3


def _dot(a, b):
    return jnp.dot(a, b, preferred_element_type=F32)


def _dot_t(a, b_t):
    return lax.dot_general(a, b_t, (((1,), (1,)), ((), ())), preferred_element_type=F32)


def _params(semantics):
    return pltpu.CompilerParams(dimension_semantics=semantics,
                                vmem_limit_bytes=VMEM_LIMIT_BYTES)


def _cast_rows_spec(rows, cols, n_steps, step_of):
    assert rows % n_steps == 0 and (rows // n_steps) % BF16_SUBLANES == 0
    return pl.BlockSpec((rows // n_steps, cols), lambda *ids: (step_of(*ids), 0))


def _inproj_conv_kernel(x_ref, wb_ref, wc_ref, wh_ref, cw_ref, wdn_ref, wout_ref,
                        y_ref, xb_ref, wdn_bf_ref, wout_bf_ref,
                        wbf_ref, carry_ref, buf_ref, *, tiles_per_seq, chunk):
    i = pl.program_id(1)
    rows = x_ref.shape[0]

    @pl.when(i == 0)
    def _():
        wbf_ref[0] = wb_ref[...].astype(BF16)
        wbf_ref[1] = wc_ref[...].astype(BF16)
        wbf_ref[2] = wh_ref[...].astype(BF16)

    @pl.when(pl.program_id(0) == 0)
    def _():
        xb_ref[...] = x_ref[...].astype(BF16)

    @pl.when(i % tiles_per_seq == 0)
    def _():
        carry_ref[...] = jnp.zeros(carry_ref.shape, F32)

    wdn_bf_ref[...] = wdn_ref[...].astype(BF16)
    wout_bf_ref[...] = wout_ref[...].astype(BF16)

    buf_ref[0:SUBLANES, :] = carry_ref[...]
    w = cw_ref[...]
    row_starts = list(range(0, rows, chunk))
    xbs = [x_ref[r0:r0 + chunk, :].astype(BF16) for r0 in row_starts]
    for r0, xb in zip(row_starts, xbs):
        u = _dot_t(xb, wbf_ref[1]) * _dot_t(xb, wbf_ref[2])
        buf_ref[SUBLANES + r0:SUBLANES + r0 + chunk, :] = u
    c_bs = [_dot_t(xb, wbf_ref[0]) for xb in xbs]
    for r0, c_b in zip(row_starts, c_bs):
        u = buf_ref[SUBLANES + r0:SUBLANES + r0 + chunk, :]
        um2 = buf_ref[SUBLANES - 2 + r0:SUBLANES - 2 + r0 + chunk, :]
        um1 = buf_ref[SUBLANES - 1 + r0:SUBLANES - 1 + r0 + chunk, :]
        y = w[0:1, :] * um2 + w[1:2, :] * um1 + w[2:3, :] * u
        y_ref[r0:r0 + chunk, :] = (c_b * y).astype(y_ref.dtype)
    carry_ref[...] = buf_ref[rows:rows + SUBLANES, :]


def _inproj_conv(x2d, w_in_t, w_sc_conv, w_down, w_out, *, seq, w_conv):
    m, d = x2d.shape
    tm, tc = TM_CONV, TC_CONV
    nj, ni = w_conv // tc, m // tm
    n_steps = nj * ni
    step_of = lambda j, i: j * ni + i
    once = pl.Buffered(1)
    kern = functools.partial(_inproj_conv_kernel, tiles_per_seq=seq // tm, chunk=TR_CONV)
    return pl.pallas_call(
        kern,
        out_shape=(jax.ShapeDtypeStruct((m, w_conv), BF16),
                   jax.ShapeDtypeStruct((m, d), BF16),
                   jax.ShapeDtypeStruct(w_down.shape, BF16),
                   jax.ShapeDtypeStruct(w_out.shape, BF16)),
        grid=(nj, ni),
        in_specs=[
            pl.BlockSpec((tm, d), lambda j, i: (i, 0)),
            pl.BlockSpec((tc, d), lambda j, i: (j, 0), pipeline_mode=once),
            pl.BlockSpec((tc, d), lambda j, i: (nj + j, 0), pipeline_mode=once),
            pl.BlockSpec((tc, d), lambda j, i: (2 * nj + j, 0), pipeline_mode=once),
            pl.BlockSpec((CONV_WIDTH, tc), lambda j, i: (0, j)),
            _cast_rows_spec(*w_down.shape, n_steps, step_of),
            _cast_rows_spec(*w_out.shape, n_steps, step_of),
        ],
        out_specs=(pl.BlockSpec((tm, tc), lambda j, i: (i, j)),
                   pl.BlockSpec((tm, d), lambda j, i: (jnp.where(j == 0, i, ni - 1), 0)),
                   _cast_rows_spec(*w_down.shape, n_steps, step_of),
                   _cast_rows_spec(*w_out.shape, n_steps, step_of)),
        scratch_shapes=[
            pltpu.VMEM((3, tc, d), BF16),
            pltpu.VMEM((SUBLANES, tc), F32),
            pltpu.VMEM((tm + SUBLANES, tc), F32),
        ],
        compiler_params=_params(("arbitrary", "arbitrary")),
        name="inproj_shortconv",
    )(x2d, w_in_t, w_in_t, w_in_t, w_sc_conv, w_down, w_out)


def _inproj_qkvo_kernel(xb_hbm, w_ref, wup_hbm, o_ref, wup_bf_ref, wbf_ref, xb_ring, wup_ring, sems):
    ni = pl.num_programs(1)
    n_steps = pl.num_programs(0) * ni
    step_id = pl.program_id(0) * ni + pl.program_id(1)
    slot = step_id % INPUT_SLOTS
    tm = xb_ring.shape[1]
    up_rows = wup_ring.shape[1]

    def input_copies(to_slot, step):
        row_tile = step % ni
        return (
            pltpu.make_async_copy(xb_hbm.at[pl.ds(pl.multiple_of(row_tile * tm, tm), tm), :],
                                  xb_ring.at[to_slot], sems.at[0, to_slot]),
            pltpu.make_async_copy(wup_hbm.at[pl.ds(pl.multiple_of(step * up_rows, up_rows), up_rows), :],
                                  wup_ring.at[to_slot], sems.at[1, to_slot]),
        )

    @pl.when(step_id == 0)
    def _():
        for ahead in range(INPUT_SLOTS - 1):
            for copy in input_copies(ahead, ahead):
                copy.start()

    for copy in input_copies(slot, step_id):
        copy.wait()

    @pl.when(step_id + (INPUT_SLOTS - 1) < n_steps)
    def _():
        ahead = step_id + (INPUT_SLOTS - 1)
        for copy in input_copies(ahead % INPUT_SLOTS, ahead):
            copy.start()

    @pl.when(pl.program_id(1) == 0)
    def _():
        wbf_ref[...] = w_ref[...].astype(BF16)

    wup_ref = wup_ring.at[slot]
    n_blocks, _, tf = wup_bf_ref.shape
    for n in range(n_blocks):
        wup_bf_ref[n] = wup_ref[:, n * tf:(n + 1) * tf].astype(BF16)
    o_ref[...] = _dot_t(xb_ring[slot], wbf_ref[...]).astype(o_ref.dtype)


def _inproj_qkvo(xb2d, w_in_t, w_up, *, col0, width, tf):
    m, d = xb2d.shape
    tm, tn = TM_QKVO, TN_QKVO
    j0 = col0 // tn
    nj, ni = width // tn, m // tm
    n_steps = nj * ni
    step_of = lambda j, i: j * ni + i
    up_rows, up_cols = w_up.shape
    n_blocks = up_cols // tf
    rows_per_step = up_rows // n_steps
    assert up_cols % tf == 0 and up_rows % n_steps == 0 and rows_per_step % BF16_SUBLANES == 0
    return pl.pallas_call(
        _inproj_qkvo_kernel,
        out_shape=(jax.ShapeDtypeStruct((m, width), BF16),
                   jax.ShapeDtypeStruct((n_blocks, up_rows, tf), BF16)),
        grid=(nj, ni),
        in_specs=[
            pl.BlockSpec(memory_space=pl.ANY),
            pl.BlockSpec((tn, d), lambda j, i: (j0 + j, 0)),
            pl.BlockSpec(memory_space=pl.ANY),
        ],
        out_specs=(pl.BlockSpec((tm, tn), lambda j, i: (i, j)),
                   pl.BlockSpec((n_blocks, rows_per_step, tf),
                                lambda j, i: (0, step_of(j, i), 0))),
        scratch_shapes=[pltpu.VMEM((tn, d), BF16),
                        pltpu.VMEM((INPUT_SLOTS, tm, d), BF16),
                        pltpu.VMEM((INPUT_SLOTS, rows_per_step, up_cols), F32),
                        pltpu.SemaphoreType.DMA((2, INPUT_SLOTS))],
        compiler_params=_params(("arbitrary", "arbitrary")),
        name="inproj_mlstm",
    )(xb2d, w_in_t, w_up)


def _log_sigmoid(x):
    return jnp.minimum(x, 0.0) - jnp.log1p(jnp.exp(-jnp.abs(x)))


def _layer_norm(z, g, b):
    mu = jnp.mean(z, axis=-1, keepdims=True)
    zc = z - mu
    var = jnp.mean(zc * zc, axis=-1, keepdims=True)
    return zc * lax.rsqrt(var + LN_EPS) * g + b


def _mlstm_outproj_kernel(*refs, nh, dh, chunks_per_seq, alpha, n_split):
    (q_ref, k_ref, v_ref, o_ref, xb0_ref, xbn_ref, wg_ref, bg_ref, gain_ref,
     x_ref, yc_ref) = refs[:11]
    wc_refs = refs[11:11 + n_split]
    wm_refs = refs[11 + n_split:11 + 2 * n_split]
    (lng_ref, lnb_ref, x1_ref,
     c_ref, n_ref, m_ref, gates_ref, gates_t_ref, ym_ref) = refs[11 + 2 * n_split:]
    chunk = q_ref.shape[0]
    step = pl.program_id(0)

    @pl.when(step % chunks_per_seq == 0)
    def _():
        c_ref[...] = jnp.zeros(c_ref.shape, F32)
        n_ref[...] = jnp.zeros(n_ref.shape, F32)
        m_ref[...] = jnp.zeros(m_ref.shape, F32)

    row = lax.broadcasted_iota(jnp.int32, (chunk, chunk), 0)
    col = lax.broadcasted_iota(jnp.int32, (chunk, chunk), 1)
    causal = row >= col

    def gate_terms(xb):
        g = _dot_t(xb, wg_ref[...]) + bg_ref[...]
        lf = _log_sigmoid(g)
        lf_hi = lf.astype(BF16)
        lf_r1 = lf - lf_hi.astype(F32)
        lf_mid = lf_r1.astype(BF16)
        lf_lo = (lf_r1 - lf_mid.astype(F32)).astype(BF16)
        parts = _dot(causal.astype(BF16), jnp.concatenate([lf_hi, lf_mid, lf_lo], axis=1))
        bcum = parts[:, :LANES] + parts[:, LANES:2 * LANES] + parts[:, 2 * LANES:]
        gates_ref[0] = g
        gates_ref[1] = bcum
        gates_t_ref[0] = g.T
        gates_t_ref[1] = bcum.T

    @pl.when(step == 0)
    def _():
        gate_terms(xb0_ref[...])
        ym_ref[...] = jnp.zeros(ym_ref.shape, ym_ref.dtype)

    g = gates_ref[0]
    bcum = gates_ref[1]
    g_t = gates_t_ref[0]
    bcum_t = gates_t_ref[1]
    neg_inf = jnp.float32(-jnp.inf)
    cols = [slice(h * dh, (h + 1) * dh) for h in range(nh)]

    yc_prev = yc_ref[...]
    ym_prev = ym_ref[...]
    mix = jnp.concatenate(
        [_dot(yc_prev, wc[...]) + _dot(ym_prev, wm[...])
         for wc, wm in zip(wc_refs, wm_refs)], axis=1)

    def recurrence(heads):
        q = {h: q_ref[:, cols[h]] * jnp.asarray(dh ** -0.5, BF16) for h in heads}
        k = {h: k_ref[:, cols[h]] for h in heads}
        v = {h: v_ref[:, cols[h]] for h in heads}
        i_col = {h: g[:, h:h + 1] for h in heads}
        b_col = {h: bcum[:, nh + h:nh + h + 1] for h in heads}
        i_row = {h: g_t[h:h + 1, :] for h in heads}
        b_row = {h: bcum_t[nh + h:nh + h + 1, :] for h in heads}
        g_tot = {h: b_col[h][chunk - 1:chunk, :] for h in heads}
        m_prev = {h: m_ref[h, 0:1, 0:1] for h in heads}
        c_prev = {h: c_ref[h] for h in heads}
        n_prev = {h: n_ref[h, 0:1, :] for h in heads}

        qk = {h: _dot_t(q[h], k[h]) for h in heads}
        inter = {h: _dot(q[h], c_prev[h].astype(BF16)) for h in heads}
        qn = {h: jnp.sum(q[h].astype(F32) * n_prev[h], axis=-1, keepdims=True) for h in heads}

        log_d = {h: jnp.where(causal, b_col[h] - (b_row[h] - i_row[h]), neg_inf) for h in heads}
        inter_log = {h: b_col[h] + m_prev[h] for h in heads}
        m_t = {h: jnp.maximum(inter_log[h], jnp.max(log_d[h], axis=-1, keepdims=True))
               for h in heads}
        scores = {h: qk[h] * jnp.exp(log_d[h] - m_t[h]) for h in heads}
        inter_w = {h: jnp.exp(inter_log[h] - m_t[h]) for h in heads}
        num = {h: _dot(scores[h].astype(BF16), v[h]) + inter_w[h] * inter[h] for h in heads}
        den = {h: jnp.sum(scores[h], axis=-1, keepdims=True) + inter_w[h] * qn[h] for h in heads}
        hh = {h: num[h] / jnp.maximum(jnp.abs(den[h]), jnp.exp(-m_t[h])) for h in heads}

        w_log = {h: g_tot[h] - b_col[h] + i_col[h] for h in heads}
        m_new = {h: jnp.maximum(g_tot[h] + m_prev[h], jnp.max(w_log[h], axis=0, keepdims=True))
                 for h in heads}
        decay = {h: jnp.exp(g_tot[h] + m_prev[h] - m_new[h]) for h in heads}
        kw = {h: k[h].astype(F32) * jnp.exp(w_log[h] - m_new[h]) for h in heads}
        for h in heads:
            c_ref[h] = decay[h] * c_prev[h] + lax.dot_general(
                kw[h].astype(BF16), v[h], (((0,), (0,)), ((), ())), preferred_element_type=F32)
            n_ref[h, 0:1, :] = decay[h] * n_prev[h] + jnp.sum(kw[h], axis=0, keepdims=True)
            m_ref[h] = jnp.broadcast_to(m_new[h], m_ref.shape[1:])

        return hh

    def head_norm_gate(heads, hh):
        mu = {h: jnp.mean(hh[h], axis=-1, keepdims=True) for h in heads}
        hc = {h: hh[h] - mu[h] for h in heads}
        var = {h: jnp.mean(hc[h] * hc[h], axis=-1, keepdims=True) for h in heads}
        for h in heads:
            hn = hc[h] * lax.rsqrt(var[h] + HEAD_NORM_EPS) * gain_ref[:, cols[h]]
            o = o_ref[:, cols[h]].astype(F32)
            ym_ref[:, cols[h]] = (jax.nn.sigmoid(o) * hn).astype(ym_ref.dtype)

    groups = [tuple(range(h0, min(h0 + HEAD_GROUP, nh))) for h0 in range(0, nh, HEAD_GROUP)]
    for g_idx, group in enumerate(groups):
        head_norm_gate(group, recurrence(group))
        if g_idx == 0:
            z = alpha * x_ref[...] + mix
            x1_ref[...] = _layer_norm(z, lng_ref[...], lnb_ref[...])

    gate_terms(xbn_ref[...])


def _mlstm_outproj(qkvo, xb2d, w_gates_t_bf, b_gates_row, gain_row, x2d, y_conv, w_out_bf,
                   ln_g, ln_b, *, seq, nh, dh, alpha):
    m, d = x2d.shape
    w = nh * dh
    wc = y_conv.shape[1]
    assert wc + w == w_out_bf.shape[0] and wc == w
    chunk = MLSTM_CHUNK
    n_chunks = m // chunk
    last = n_chunks - 1

    def cur(s):
        return jnp.minimum(s, last)

    def prev(s):
        return jnp.maximum(s - 1, 0)

    def blk(kind):
        return pl.BlockSpec((chunk, w), lambda s: (cur(s), kind))

    const = pl.Buffered(1)
    piece = OUT_COL_BLOCK
    assert d % piece == 0
    n_split = d // piece
    kern = functools.partial(_mlstm_outproj_kernel, nh=nh, dh=dh, chunks_per_seq=seq // chunk,
                             alpha=alpha, n_split=n_split)
    return pl.pallas_call(
        kern,
        out_shape=jax.ShapeDtypeStruct((m, d), F32),
        grid=(n_chunks + 1,),
        in_specs=[blk(0), blk(1), blk(2), blk(3),
                  pl.BlockSpec((chunk, d), lambda s: (0, 0)),
                  pl.BlockSpec((chunk, d), lambda s: (jnp.minimum(s + 1, last), 0)),
                  pl.BlockSpec((LANES, d), lambda s: (0, 0)),
                  pl.BlockSpec((1, LANES), lambda s: (0, 0)),
                  pl.BlockSpec((1, w), lambda s: (0, 0)),
                  pl.BlockSpec((chunk, d), lambda s: (prev(s), 0)),
                  pl.BlockSpec((chunk, wc), lambda s: (prev(s), 0)),
                  *[pl.BlockSpec((wc, piece), lambda s, p=p: (0, p), pipeline_mode=const)
                    for p in range(n_split)],
                  *[pl.BlockSpec((w, piece), lambda s, p=p: (1, p), pipeline_mode=const)
                    for p in range(n_split)],
                  pl.BlockSpec((1, d), lambda s: (0, 0)),
                  pl.BlockSpec((1, d), lambda s: (0, 0))],
        out_specs=pl.BlockSpec((chunk, d), lambda s: (prev(s), 0)),
        scratch_shapes=[
            pltpu.VMEM((nh, dh, dh), F32),
            pltpu.VMEM((nh, SUBLANES, dh), F32),
            pltpu.VMEM((nh, SUBLANES, LANES), F32),
            pltpu.VMEM((2, chunk, LANES), F32),
            pltpu.VMEM((2, LANES, chunk), F32),
            pltpu.VMEM((chunk, w), BF16),
        ],
        compiler_params=_params(("arbitrary",)),
        name="mlstm_outproj_ln",
    )(qkvo, qkvo, qkvo, qkvo, xb2d, xb2d, w_gates_t_bf, b_gates_row, gain_row,
      x2d, y_conv, *([w_out_bf] * (2 * n_split)), ln_g, ln_b)


def _convffn_ln_kernel(x_ref, wup_hbm, cw_ref, cb_ref, wdn_hbm,
                       g_ref, b_ref, o_ref, xb_ref, acc_ref, carry_v_ref, carry_g_ref,
                       buf_v_ref, buf_g_ref, wv_ring, wg_ring, wd_ring, sems, *,
                       alpha, tiles_per_seq, chunk, nf):
    i = pl.program_id(0)
    n_steps = pl.num_programs(0) * nf
    rows = x_ref.shape[0]
    tf = wv_ring.shape[2]
    first = (i % tiles_per_seq) == 0

    def weight_copies(to_slot, blk):
        return (
            pltpu.make_async_copy(wup_hbm.at[blk], wv_ring.at[to_slot], sems.at[0, to_slot]),
            pltpu.make_async_copy(wup_hbm.at[nf + blk], wg_ring.at[to_slot], sems.at[1, to_slot]),
            pltpu.make_async_copy(wdn_hbm.at[pl.ds(pl.multiple_of(blk * tf, tf), tf), :],
                                  wd_ring.at[to_slot], sems.at[2, to_slot]),
        )

    @pl.when(i == 0)
    def _():
        for ahead in range(WEIGHT_SLOTS - 1):
            for copy in weight_copies(ahead, ahead % nf):
                copy.start()

    def conv(buf_ref, cols, r0):
        h = buf_ref[SUBLANES + r0:SUBLANES + r0 + chunk, :]
        hm1 = buf_ref[SUBLANES - 1 + r0:SUBLANES - 1 + r0 + chunk, :]
        hm2 = buf_ref[SUBLANES - 2 + r0:SUBLANES - 2 + r0 + chunk, :]
        w = cw_ref[:, cols]
        return w[0:1, :] * hm2 + w[1:2, :] * hm1 + w[2:3, :] * h + cb_ref[:, cols]

    def block_step(f, is_first, is_last):
        step_id = i * nf + f
        slot = step_id % WEIGHT_SLOTS
        for copy in weight_copies(slot, f):
            copy.wait()

        @pl.when(step_id + (WEIGHT_SLOTS - 1) < n_steps)
        def _():
            ahead = step_id + (WEIGHT_SLOTS - 1)
            for copy in weight_copies(ahead % WEIGHT_SLOTS, ahead % nf):
                copy.start()

        wv_ref = wv_ring.at[slot]
        wg_ref = wg_ring.at[slot]
        wd_ref = wd_ring.at[slot]
        cols_v = pl.ds(pl.multiple_of(f * tf, tf), tf)
        cols_g = pl.ds(pl.multiple_of((nf + f) * tf, tf), tf)

        @pl.when(first)
        def _():
            carry_v_ref[f] = jnp.zeros(carry_v_ref.shape[1:], F32)
            carry_g_ref[f] = jnp.zeros(carry_g_ref.shape[1:], F32)

        row_starts = list(range(0, rows, chunk))
        buf_v_ref[0:SUBLANES, :] = carry_v_ref[f]
        buf_g_ref[0:SUBLANES, :] = carry_g_ref[f]
        for r0 in row_starts:
            if is_first:
                xb_ref[r0:r0 + chunk, :] = x_ref[r0:r0 + chunk, :].astype(BF16)
            xb = xb_ref[r0:r0 + chunk, :]
            buf_v_ref[SUBLANES + r0:SUBLANES + r0 + chunk, :] = _dot(xb, wv_ref[...])
            buf_g_ref[SUBLANES + r0:SUBLANES + r0 + chunk, :] = _dot(xb, wg_ref[...])
        for r0 in row_starts:
            rs = slice(r0, r0 + chunk)
            val = conv(buf_v_ref, cols_v, r0)
            gate = conv(buf_g_ref, cols_g, r0)
            act = (jax.nn.silu(gate) * val).astype(BF16)
            part = _dot(act, wd_ref[...])
            if is_first:
                acc_ref[rs, :] = part
            elif is_last:
                z = alpha * x_ref[rs, :] + (acc_ref[rs, :] + part)
                o_ref[rs, :] = _layer_norm(z, g_ref[...], b_ref[...])
            else:
                acc_ref[rs, :] += part
        carry_v_ref[f] = buf_v_ref[rows:rows + SUBLANES, :]
        carry_g_ref[f] = buf_g_ref[rows:rows + SUBLANES, :]

    block_step(0, True, False)

    @pl.loop(1, nf - 1)
    def _(f):
        block_step(f, False, False)

    block_step(nf - 1, False, True)


def _convffn_ln(x2d, w_up_bf, w_ffn_conv, b_ffn_conv_row, w_down_bf, ln_g, ln_b, *,
                alpha, seq, d_ff):
    m, d = x2d.shape
    tm, tf = TM_FFN, TF_FFN
    nf = d_ff // tf
    kern = functools.partial(_convffn_ln_kernel, alpha=alpha, tiles_per_seq=seq // tm,
                             chunk=TR_FFN, nf=nf)
    return pl.pallas_call(
        kern,
        out_shape=jax.ShapeDtypeStruct((m, d), F32),
        grid=(m // tm,),
        in_specs=[
            pl.BlockSpec((tm, d), lambda i: (i, 0)),
            pl.BlockSpec(memory_space=pl.ANY),
            pl.BlockSpec((CONV_WIDTH, 2 * d_ff), lambda i: (0, 0)),
            pl.BlockSpec((1, 2 * d_ff), lambda i: (0, 0)),
            pl.BlockSpec(memory_space=pl.ANY),
            pl.BlockSpec((1, d), lambda i: (0, 0)),
            pl.BlockSpec((1, d), lambda i: (0, 0)),
        ],
        out_specs=pl.BlockSpec((tm, d), lambda i: (i, 0)),
        scratch_shapes=[
            pltpu.VMEM((tm, d), BF16),
            pltpu.VMEM((tm, d), F32),
            pltpu.VMEM((nf, SUBLANES, tf), F32),
            pltpu.VMEM((nf, SUBLANES, tf), F32),
            pltpu.VMEM((tm + SUBLANES, tf), F32),
            pltpu.VMEM((tm + SUBLANES, tf), F32),
            pltpu.VMEM((WEIGHT_SLOTS, d, tf), BF16),
            pltpu.VMEM((WEIGHT_SLOTS, d, tf), BF16),
            pltpu.VMEM((WEIGHT_SLOTS, tf, d), BF16),
            pltpu.SemaphoreType.DMA((N_WEIGHT_STREAMS, WEIGHT_SLOTS)),
        ],
        compiler_params=_params(("arbitrary",)),
        name="convffn_ln",
    )(x2d, w_up_bf, w_ffn_conv, b_ffn_conv_row, w_down_bf, ln_g, ln_b)


def kernel(x, w_in, b_gates, w_sc_conv, mh_gain, w_out, ln1_g, ln1_b,
           w_up, w_ffn_conv, b_ffn_conv, w_down, ln2_g, ln2_b):
    batch, seq, d = x.shape
    depth = w_in.shape[0]
    nh = N_HEADS_M
    w_conv = w_sc_conv.shape[-1]
    w_mlstm = mh_gain.shape[-1]
    dh = w_mlstm // nh
    d_ff = w_down.shape[1]
    n_gates = 2 * nh
    gate_col0 = 3 * w_conv + 4 * w_mlstm
    assert w_in.shape[-1] == gate_col0 + n_gates and w_conv + w_mlstm == d
    assert seq % TM_CONV == 0 and seq % TM_FFN == 0 and seq % MLSTM_CHUNK == 0
    assert w_conv % TC_CONV == 0 and d_ff % TF_FFN == 0
    assert (3 * w_conv) % TN_QKVO == 0 and (4 * w_mlstm) % TN_QKVO == 0
    assert (batch * seq) % TM_QKVO == 0
    alpha = (2 * depth) ** 0.25

    x2d = x.reshape(batch * seq, d)
    for l in range(depth):
        w_in_t = w_in[l].T
        w_gates_t_bf = jnp.pad(w_in_t[gate_col0:], ((0, LANES - n_gates), (0, 0))).astype(BF16)
        b_gates_row = jnp.pad(b_gates[l], (0, LANES - n_gates)).reshape(1, LANES)

        y_conv, xb2d, w_down_bf, w_out_bf = _inproj_conv(
            x2d, w_in_t, w_sc_conv[l], w_down[l], w_out[l], seq=seq, w_conv=w_conv)
        qkvo, w_up_bf = _inproj_qkvo(xb2d, w_in_t, w_up[l], col0=3 * w_conv, width=4 * w_mlstm,
                                     tf=TF_FFN)
        x2d = _mlstm_outproj(qkvo, xb2d, w_gates_t_bf, b_gates_row, mh_gain[l].reshape(1, w_mlstm),
                             x2d, y_conv, w_out_bf, ln1_g[l].reshape(1, d), ln1_b[l].reshape(1, d),
                             seq=seq, nh=nh, dh=dh, alpha=alpha)
        x2d = _convffn_ln(x2d, w_up_bf, w_ffn_conv[l],
                          b_ffn_conv[l].reshape(1, 2 * d_ff), w_down_bf,
                          ln2_g[l].reshape(1, d), ln2_b[l].reshape(1, d),
                          alpha=alpha, seq=seq, d_ff=d_ff)
    return x2d.reshape(batch, seq, d)
```

```python
import functools

import jax
import jax.numpy as jnp
from jax import lax
from jax.experimental import pallas as pl
from jax.experimental.pallas import tpu as pltpu

F32 = jnp.float32
BF16 = jnp.bfloat16

N_HEADS_M = 4
CONV_WIDTH = 3
LN_EPS = 1e-5
HEAD_NORM_EPS = 1e-6

SUBLANES = 8
LANES = 128
BF16_SUBLANES = 16
VMEM_LIMIT_BYTES = 56 * 1024 * 1024

TM_CONV = 512
TC_CONV = 512
TR_CONV = 256
TM_QKVO = 1024
TN_QKVO = 1024
INPUT_SLOTS = 3
MLSTM_CHUNK = 256
HEAD_GROUP = 2
OUT_COL_BLOCK = 512
TM_FFN = 512
TF_FFN = 512
TR_FFN = 256
WEIGHT_SLOTS = 3
N_WEIGHT_STREAMS = 3


def _dot(a, b):
    return jnp.dot(a, b, preferred_element_type=F32)


def _dot_t(a, b_t):
    return lax.dot_general(a, b_t, (((1,), (1,)), ((), ())), preferred_element_type=F32)


def _params(semantics):
    return pltpu.CompilerParams(dimension_semantics=semantics,
                                vmem_limit_bytes=VMEM_LIMIT_BYTES)


def _cast_rows_spec(rows, cols, n_steps, step_of):
    assert rows % n_steps == 0 and (rows // n_steps) % BF16_SUBLANES == 0
    return pl.BlockSpec((rows // n_steps, cols), lambda *ids: (step_of(*ids), 0))


def _inproj_conv_kernel(x_hbm, wb_ref, wc_ref, wh_ref, cw_ref, wdn_hbm, wout_hbm,
                        y_ref, xb_ref, wdn_bf_ref, wout_bf_ref,
                        wbf_ref, carry_ref, buf_ref, x_ring, wdn_ring, wout_ring, sems, *,
                        tiles_per_seq, chunk):
    i = pl.program_id(1)
    ni = pl.num_programs(1)
    n_steps = pl.num_programs(0) * ni
    step_id = pl.program_id(0) * ni + i
    slot = step_id % INPUT_SLOTS
    rows = x_ring.shape[1]

    def input_copies(to_slot, step):
        def row_block(ref, n, block):
            return ref.at[pl.ds(pl.multiple_of(block * n, n), n), :]
        return (
            pltpu.make_async_copy(row_block(x_hbm, rows, step % ni),
                                  x_ring.at[to_slot], sems.at[0, to_slot]),
            pltpu.make_async_copy(row_block(wdn_hbm, wdn_ring.shape[1], step),
                                  wdn_ring.at[to_slot], sems.at[1, to_slot]),
            pltpu.make_async_copy(row_block(wout_hbm, wout_ring.shape[1], step),
                                  wout_ring.at[to_slot], sems.at[2, to_slot]),
        )

    @pl.when(step_id == 0)
    def _():
        for ahead in range(INPUT_SLOTS - 1):
            for copy in input_copies(ahead, ahead):
                copy.start()

    for copy in input_copies(slot, step_id):
        copy.wait()

    @pl.when(step_id + (INPUT_SLOTS - 1) < n_steps)
    def _():
        ahead = step_id + (INPUT_SLOTS - 1)
        for copy in input_copies(ahead % INPUT_SLOTS, ahead):
            copy.start()

    x_ref = x_ring.at[slot]
    wdn_ref = wdn_ring.at[slot]
    wout_ref = wout_ring.at[slot]

    @pl.when(i == 0)
    def _():
        wbf_ref[0] = wb_ref[...].astype(BF16)
        wbf_ref[1] = wc_ref[...].astype(BF16)
        wbf_ref[2] = wh_ref[...].astype(BF16)

    @pl.when(pl.program_id(0) == 0)
    def _():
        xb_ref[...] = x_ref[...].astype(BF16)

    @pl.when(i % tiles_per_seq == 0)
    def _():
        carry_ref[...] = jnp.zeros(carry_ref.shape, F32)

    wdn_bf_ref[...] = wdn_ref[...].astype(BF16)
    wout_bf_ref[...] = wout_ref[...].astype(BF16)

    buf_ref[0:SUBLANES, :] = carry_ref[...]
    w = cw_ref[...]
    row_starts = list(range(0, rows, chunk))
    xbs = [x_ref[r0:r0 + chunk, :].astype(BF16) for r0 in row_starts]
    for r0, xb in zip(row_starts, xbs):
        u = _dot_t(xb, wbf_ref[1]) * _dot_t(xb, wbf_ref[2])
        buf_ref[SUBLANES + r0:SUBLANES + r0 + chunk, :] = u
    c_bs = [_dot_t(xb, wbf_ref[0]) for xb in xbs]
    for r0, c_b in zip(row_starts, c_bs):
        u = buf_ref[SUBLANES + r0:SUBLANES + r0 + chunk, :]
        um2 = buf_ref[SUBLANES - 2 + r0:SUBLANES - 2 + r0 + chunk, :]
        um1 = buf_ref[SUBLANES - 1 + r0:SUBLANES - 1 + r0 + chunk, :]
        y = w[0:1, :] * um2 + w[1:2, :] * um1 + w[2:3, :] * u
        y_ref[r0:r0 + chunk, :] = (c_b * y).astype(y_ref.dtype)
    carry_ref[...] = buf_ref[rows:rows + SUBLANES, :]


def _inproj_conv(x2d, w_in_t, w_sc_conv, w_down, w_out, *, seq, w_conv):
    m, d = x2d.shape
    tm, tc = TM_CONV, TC_CONV
    nj, ni = w_conv // tc, m // tm
    n_steps = nj * ni
    step_of = lambda j, i: j * ni + i
    once = pl.Buffered(1)
    kern = functools.partial(_inproj_conv_kernel, tiles_per_seq=seq // tm, chunk=TR_CONV)
    return pl.pallas_call(
        kern,
        out_shape=(jax.ShapeDtypeStruct((m, w_conv), BF16),
                   jax.ShapeDtypeStruct((m, d), BF16),
                   jax.ShapeDtypeStruct(w_down.shape, BF16),
                   jax.ShapeDtypeStruct(w_out.shape, BF16)),
        grid=(nj, ni),
        in_specs=[
            pl.BlockSpec(memory_space=pl.ANY),
            pl.BlockSpec((tc, d), lambda j, i: (j, 0), pipeline_mode=once),
            pl.BlockSpec((tc, d), lambda j, i: (nj + j, 0), pipeline_mode=once),
            pl.BlockSpec((tc, d), lambda j, i: (2 * nj + j, 0), pipeline_mode=once),
            pl.BlockSpec((CONV_WIDTH, tc), lambda j, i: (0, j)),
            pl.BlockSpec(memory_space=pl.ANY),
            pl.BlockSpec(memory_space=pl.ANY),
        ],
        out_specs=(pl.BlockSpec((tm, tc), lambda j, i: (i, j)),
                   pl.BlockSpec((tm, d), lambda j, i: (jnp.where(j == 0, i, ni - 1), 0)),
                   _cast_rows_spec(*w_down.shape, n_steps, step_of),
                   _cast_rows_spec(*w_out.shape, n_steps, step_of)),
        scratch_shapes=[
            pltpu.VMEM((3, tc, d), BF16),
            pltpu.VMEM((SUBLANES, tc), F32),
            pltpu.VMEM((tm + SUBLANES, tc), F32),
            pltpu.VMEM((INPUT_SLOTS, tm, d), F32),
            pltpu.VMEM((INPUT_SLOTS, w_down.shape[0] // n_steps, w_down.shape[1]), F32),
            pltpu.VMEM((INPUT_SLOTS, w_out.shape[0] // n_steps, w_out.shape[1]), F32),
            pltpu.SemaphoreType.DMA((3, INPUT_SLOTS)),
        ],
        compiler_params=_params(("arbitrary", "arbitrary")),
        name="inproj_shortconv",
    )(x2d, w_in_t, w_in_t, w_in_t, w_sc_conv, w_down, w_out)


def _inproj_qkvo_kernel(xb_hbm, w_ref, wup_hbm, o_ref, wup_bf_ref, wbf_ref, xb_ring, wup_ring, sems):
    ni = pl.num_programs(1)
    n_steps = pl.num_programs(0) * ni
    step_id = pl.program_id(0) * ni + pl.program_id(1)
    slot = step_id % INPUT_SLOTS
    tm = xb_ring.shape[1]
    up_rows = wup_ring.shape[1]

    def input_copies(to_slot, step):
        row_tile = step % ni
        return (
            pltpu.make_async_copy(xb_hbm.at[pl.ds(pl.multiple_of(row_tile * tm, tm), tm), :],
                                  xb_ring.at[to_slot], sems.at[0, to_slot]),
            pltpu.make_async_copy(wup_hbm.at[pl.ds(pl.multiple_of(step * up_rows, up_rows), up_rows), :],
                                  wup_ring.at[to_slot], sems.at[1, to_slot]),
        )

    @pl.when(step_id == 0)
    def _():
        for ahead in range(INPUT_SLOTS - 1):
            for copy in input_copies(ahead, ahead):
                copy.start()

    for copy in input_copies(slot, step_id):
        copy.wait()

    @pl.when(step_id + (INPUT_SLOTS - 1) < n_steps)
    def _():
        ahead = step_id + (INPUT_SLOTS - 1)
        for copy in input_copies(ahead % INPUT_SLOTS, ahead):
            copy.start()

    @pl.when(pl.program_id(1) == 0)
    def _():
        wbf_ref[...] = w_ref[...].astype(BF16)

    wup_ref = wup_ring.at[slot]
    n_blocks, _, tf = wup_bf_ref.shape
    for n in range(n_blocks):
        wup_bf_ref[n] = wup_ref[:, n * tf:(n + 1) * tf].astype(BF16)
    o_ref[...] = _dot_t(xb_ring[slot], wbf_ref[...]).astype(o_ref.dtype)


def _inproj_qkvo(xb2d, w_in_t, w_up, *, col0, width, tf):
    m, d = xb2d.shape
    tm, tn = TM_QKVO, TN_QKVO
    j0 = col0 // tn
    nj, ni = width // tn, m // tm
    n_steps = nj * ni
    step_of = lambda j, i: j * ni + i
    up_rows, up_cols = w_up.shape
    n_blocks = up_cols // tf
    rows_per_step = up_rows // n_steps
    assert up_cols % tf == 0 and up_rows % n_steps == 0 and rows_per_step % BF16_SUBLANES == 0
    return pl.pallas_call(
        _inproj_qkvo_kernel,
        out_shape=(jax.ShapeDtypeStruct((m, width), BF16),
                   jax.ShapeDtypeStruct((n_blocks, up_rows, tf), BF16)),
        grid=(nj, ni),
        in_specs=[
            pl.BlockSpec(memory_space=pl.ANY),
            pl.BlockSpec((tn, d), lambda j, i: (j0 + j, 0)),
            pl.BlockSpec(memory_space=pl.ANY),
        ],
        out_specs=(pl.BlockSpec((tm, tn), lambda j, i: (i, j)),
                   pl.BlockSpec((n_blocks, rows_per_step, tf),
                                lambda j, i: (0, step_of(j, i), 0))),
        scratch_shapes=[pltpu.VMEM((tn, d), BF16),
                        pltpu.VMEM((INPUT_SLOTS, tm, d), BF16),
                        pltpu.VMEM((INPUT_SLOTS, rows_per_step, up_cols), F32),
                        pltpu.SemaphoreType.DMA((2, INPUT_SLOTS))],
        compiler_params=_params(("arbitrary", "arbitrary")),
        name="inproj_mlstm",
    )(xb2d, w_in_t, w_up)


def _log_sigmoid(x):
    return jnp.minimum(x, 0.0) - jnp.log1p(jnp.exp(-jnp.abs(x)))


def _layer_norm(z, g, b):
    mu = jnp.mean(z, axis=-1, keepdims=True)
    zc = z - mu
    var = jnp.mean(zc * zc, axis=-1, keepdims=True)
    return zc * lax.rsqrt(var + LN_EPS) * g + b


def _mlstm_outproj_kernel(*refs, nh, dh, chunks_per_seq, alpha, n_split):
    (q_ref, k_ref, v_ref, o_ref, xb0_ref, xbn_ref, wg_ref, bg_ref, gain_ref,
     x_ref, yc_ref) = refs[:11]
    wc_refs = refs[11:11 + n_split]
    wm_refs = refs[11 + n_split:11 + 2 * n_split]
    (lng_ref, lnb_ref, x1_ref,
     c_ref, n_ref, m_ref, gates_ref, gates_t_ref, ym_ref) = refs[11 + 2 * n_split:]
    chunk = q_ref.shape[0]
    step = pl.program_id(0)

    @pl.when(step % chunks_per_seq == 0)
    def _():
        c_ref[...] = jnp.zeros(c_ref.shape, F32)
        n_ref[...] = jnp.zeros(n_ref.shape, F32)
        m_ref[...] = jnp.zeros(m_ref.shape, F32)

    row = lax.broadcasted_iota(jnp.int32, (chunk, chunk), 0)
    col = lax.broadcasted_iota(jnp.int32, (chunk, chunk), 1)
    causal = row >= col

    def gate_terms(xb):
        g = _dot_t(xb, wg_ref[...]) + bg_ref[...]
        lf = _log_sigmoid(g)
        lf_hi = lf.astype(BF16)
        lf_r1 = lf - lf_hi.astype(F32)
        lf_mid = lf_r1.astype(BF16)
        lf_lo = (lf_r1 - lf_mid.astype(F32)).astype(BF16)
        parts = _dot(causal.astype(BF16), jnp.concatenate([lf_hi, lf_mid, lf_lo], axis=1))
        bcum = parts[:, :LANES] + parts[:, LANES:2 * LANES] + parts[:, 2 * LANES:]
        gates_ref[0] = g
        gates_ref[1] = bcum
        gates_t_ref[0] = g.T
        gates_t_ref[1] = bcum.T

    @pl.when(step == 0)
    def _():
        gate_terms(xb0_ref[...])
        ym_ref[...] = jnp.zeros(ym_ref.shape, ym_ref.dtype)

    g = gates_ref[0]
    bcum = gates_ref[1]
    g_t = gates_t_ref[0]
    bcum_t = gates_t_ref[1]
    neg_inf = jnp.float32(-jnp.inf)
    cols = [slice(h * dh, (h + 1) * dh) for h in range(nh)]

    yc_prev = yc_ref[...]
    ym_prev = ym_ref[...]
    mix = jnp.concatenate(
        [_dot(yc_prev, wc[...]) + _dot(ym_prev, wm[...])
         for wc, wm in zip(wc_refs, wm_refs)], axis=1)

    def recurrence(heads):
        q = {h: q_ref[:, cols[h]] * jnp.asarray(dh ** -0.5, BF16) for h in heads}
        k = {h: k_ref[:, cols[h]] for h in heads}
        v = {h: v_ref[:, cols[h]] for h in heads}
        i_col = {h: g[:, h:h + 1] for h in heads}
        b_col = {h: bcum[:, nh + h:nh + h + 1] for h in heads}
        i_row = {h: g_t[h:h + 1, :] for h in heads}
        b_row = {h: bcum_t[nh + h:nh + h + 1, :] for h in heads}
        g_tot = {h: b_col[h][chunk - 1:chunk, :] for h in heads}
        m_prev = {h: m_ref[h, 0:1, 0:1] for h in heads}
        c_prev = {h: c_ref[h] for h in heads}
        n_prev = {h: n_ref[h, 0:1, :] for h in heads}

        qk = {h: _dot_t(q[h], k[h]) for h in heads}
        inter = {h: _dot(q[h], c_prev[h].astype(BF16)) for h in heads}
        qn = {h: jnp.sum(q[h].astype(F32) * n_prev[h], axis=-1, keepdims=True) for h in heads}

        log_d = {h: jnp.where(causal, b_col[h] - (b_row[h] - i_row[h]), neg_inf) for h in heads}
        inter_log = {h: b_col[h] + m_prev[h] for h in heads}
        m_t = {h: jnp.maximum(inter_log[h], jnp.max(log_d[h], axis=-1, keepdims=True))
               for h in heads}
        scores = {h: qk[h] * jnp.exp(log_d[h] - m_t[h]) for h in heads}
        inter_w = {h: jnp.exp(inter_log[h] - m_t[h]) for h in heads}
        num = {h: _dot(scores[h].astype(BF16), v[h]) + inter_w[h] * inter[h] for h in heads}
        den = {h: jnp.sum(scores[h], axis=-1, keepdims=True) + inter_w[h] * qn[h] for h in heads}
        hh = {h: num[h] / jnp.maximum(jnp.abs(den[h]), jnp.exp(-m_t[h])) for h in heads}

        w_log = {h: g_tot[h] - b_col[h] + i_col[h] for h in heads}
        m_new = {h: jnp.maximum(g_tot[h] + m_prev[h], jnp.max(w_log[h], axis=0, keepdims=True))
                 for h in heads}
        decay = {h: jnp.exp(g_tot[h] + m_prev[h] - m_new[h]) for h in heads}
        kw = {h: k[h].astype(F32) * jnp.exp(w_log[h] - m_new[h]) for h in heads}
        for h in heads:
            c_ref[h] = decay[h] * c_prev[h] + lax.dot_general(
                kw[h].astype(BF16), v[h], (((0,), (0,)), ((), ())), preferred_element_type=F32)
            n_ref[h, 0:1, :] = decay[h] * n_prev[h] + jnp.sum(kw[h], axis=0, keepdims=True)
            m_ref[h] = jnp.broadcast_to(m_new[h], m_ref.shape[1:])

        return hh

    def head_norm_gate(heads, hh):
        mu = {h: jnp.mean(hh[h], axis=-1, keepdims=True) for h in heads}
        hc = {h: hh[h] - mu[h] for h in heads}
        var = {h: jnp.mean(hc[h] * hc[h], axis=-1, keepdims=True) for h in heads}
        for h in heads:
            hn = hc[h] * lax.rsqrt(var[h] + HEAD_NORM_EPS) * gain_ref[:, cols[h]]
            o = o_ref[:, cols[h]].astype(F32)
            ym_ref[:, cols[h]] = (jax.nn.sigmoid(o) * hn).astype(ym_ref.dtype)

    groups = [tuple(range(h0, min(h0 + HEAD_GROUP, nh))) for h0 in range(0, nh, HEAD_GROUP)]
    for g_idx, group in enumerate(groups):
        head_norm_gate(group, recurrence(group))
        if g_idx == 0:
            z = alpha * x_ref[...] + mix
            x1_ref[...] = _layer_norm(z, lng_ref[...], lnb_ref[...])

    gate_terms(xbn_ref[...])


def _mlstm_outproj(qkvo, xb2d, w_gates_t_bf, b_gates_row, gain_row, x2d, y_conv, w_out_bf,
                   ln_g, ln_b, *, seq, nh, dh, alpha):
    m, d = x2d.shape
    w = nh * dh
    wc = y_conv.shape[1]
    assert wc + w == w_out_bf.shape[0] and wc == w
    chunk = MLSTM_CHUNK
    n_chunks = m // chunk
    last = n_chunks - 1

    def cur(s):
        return jnp.minimum(s, last)

    def prev(s):
        return jnp.maximum(s - 1, 0)

    def blk(kind):
        return pl.BlockSpec((chunk, w), lambda s: (cur(s), kind))

    const = pl.Buffered(1)
    piece = OUT_COL_BLOCK
    assert d % piece == 0
    n_split = d // piece
    kern = functools.partial(_mlstm_outproj_kernel, nh=nh, dh=dh, chunks_per_seq=seq // chunk,
                             alpha=alpha, n_split=n_split)
    return pl.pallas_call(
        kern,
        out_shape=jax.ShapeDtypeStruct((m, d), F32),
        grid=(n_chunks + 1,),
        in_specs=[blk(0), blk(1), blk(2), blk(3),
                  pl.BlockSpec((chunk, d), lambda s: (0, 0)),
                  pl.BlockSpec((chunk, d), lambda s: (jnp.minimum(s + 1, last), 0)),
                  pl.BlockSpec((LANES, d), lambda s: (0, 0)),
                  pl.BlockSpec((1, LANES), lambda s: (0, 0)),
                  pl.BlockSpec((1, w), lambda s: (0, 0)),
                  pl.BlockSpec((chunk, d), lambda s: (prev(s), 0)),
                  pl.BlockSpec((chunk, wc), lambda s: (prev(s), 0)),
                  *[pl.BlockSpec((wc, piece), lambda s, p=p: (0, p), pipeline_mode=const)
                    for p in range(n_split)],
                  *[pl.BlockSpec((w, piece), lambda s, p=p: (1, p), pipeline_mode=const)
                    for p in range(n_split)],
                  pl.BlockSpec((1, d), lambda s: (0, 0)),
                  pl.BlockSpec((1, d), lambda s: (0, 0))],
        out_specs=pl.BlockSpec((chunk, d), lambda s: (prev(s), 0)),
        scratch_shapes=[
            pltpu.VMEM((nh, dh, dh), F32),
            pltpu.VMEM((nh, SUBLANES, dh), F32),
            pltpu.VMEM((nh, SUBLANES, LANES), F32),
            pltpu.VMEM((2, chunk, LANES), F32),
            pltpu.VMEM((2, LANES, chunk), F32),
            pltpu.VMEM((chunk, w), BF16),
        ],
        compiler_params=_params(("arbitrary",)),
        name="mlstm_outproj_ln",
    )(qkvo, qkvo, qkvo, qkvo, xb2d, xb2d, w_gates_t_bf, b_gates_row, gain_row,
      x2d, y_conv, *([w_out_bf] * (2 * n_split)), ln_g, ln_b)


def _convffn_ln_kernel(x_ref, wup_hbm, cw_ref, cb_ref, wdn_hbm,
                       g_ref, b_ref, o_ref, xb_ref, acc_ref, carry_v_ref, carry_g_ref,
                       buf_v_ref, buf_g_ref, wv_ring, wg_ring, wd_ring, sems, *,
                       alpha, tiles_per_seq, chunk, nf):
    i = pl.program_id(0)
    n_steps = pl.num_programs(0) * nf
    rows = x_ref.shape[0]
    tf = wv_ring.shape[2]
    first = (i % tiles_per_seq) == 0

    def weight_copies(to_slot, blk):
        return (
            pltpu.make_async_copy(wup_hbm.at[blk], wv_ring.at[to_slot], sems.at[0, to_slot]),
            pltpu.make_async_copy(wup_hbm.at[nf + blk], wg_ring.at[to_slot], sems.at[1, to_slot]),
            pltpu.make_async_copy(wdn_hbm.at[pl.ds(pl.multiple_of(blk * tf, tf), tf), :],
                                  wd_ring.at[to_slot], sems.at[2, to_slot]),
        )

    @pl.when(i == 0)
    def _():
        for ahead in range(WEIGHT_SLOTS - 1):
            for copy in weight_copies(ahead, ahead % nf):
                copy.start()

    def conv(buf_ref, cols, r0):
        h = buf_ref[SUBLANES + r0:SUBLANES + r0 + chunk, :]
        hm1 = buf_ref[SUBLANES - 1 + r0:SUBLANES - 1 + r0 + chunk, :]
        hm2 = buf_ref[SUBLANES - 2 + r0:SUBLANES - 2 + r0 + chunk, :]
        w = cw_ref[:, cols]
        return w[0:1, :] * hm2 + w[1:2, :] * hm1 + w[2:3, :] * h + cb_ref[:, cols]

    def block_step(f, is_first, is_last):
        step_id = i * nf + f
        slot = step_id % WEIGHT_SLOTS
        for copy in weight_copies(slot, f):
            copy.wait()

        @pl.when(step_id + (WEIGHT_SLOTS - 1) < n_steps)
        def _():
            ahead = step_id + (WEIGHT_SLOTS - 1)
            for copy in weight_copies(ahead % WEIGHT_SLOTS, ahead % nf):
                copy.start()

        wv_ref = wv_ring.at[slot]
        wg_ref = wg_ring.at[slot]
        wd_ref = wd_ring.at[slot]
        cols_v = pl.ds(pl.multiple_of(f * tf, tf), tf)
        cols_g = pl.ds(pl.multiple_of((nf + f) * tf, tf), tf)

        @pl.when(first)
        def _():
            carry_v_ref[f] = jnp.zeros(carry_v_ref.shape[1:], F32)
            carry_g_ref[f] = jnp.zeros(carry_g_ref.shape[1:], F32)

        row_starts = list(range(0, rows, chunk))
        buf_v_ref[0:SUBLANES, :] = carry_v_ref[f]
        buf_g_ref[0:SUBLANES, :] = carry_g_ref[f]
        for r0 in row_starts:
            if is_first:
                xb_ref[r0:r0 + chunk, :] = x_ref[r0:r0 + chunk, :].astype(BF16)
            xb = xb_ref[r0:r0 + chunk, :]
            buf_v_ref[SUBLANES + r0:SUBLANES + r0 + chunk, :] = _dot(xb, wv_ref[...])
            buf_g_ref[SUBLANES + r0:SUBLANES + r0 + chunk, :] = _dot(xb, wg_ref[...])
        for r0 in row_starts:
            rs = slice(r0, r0 + chunk)
            val = conv(buf_v_ref, cols_v, r0)
            gate = conv(buf_g_ref, cols_g, r0)
            act = (jax.nn.silu(gate) * val).astype(BF16)
            part = _dot(act, wd_ref[...])
            if is_first:
                acc_ref[rs, :] = part
            elif is_last:
                z = alpha * x_ref[rs, :] + (acc_ref[rs, :] + part)
                o_ref[rs, :] = _layer_norm(z, g_ref[...], b_ref[...])
            else:
                acc_ref[rs, :] += part
        carry_v_ref[f] = buf_v_ref[rows:rows + SUBLANES, :]
        carry_g_ref[f] = buf_g_ref[rows:rows + SUBLANES, :]

    block_step(0, True, False)

    @pl.loop(1, nf - 1)
    def _(f):
        block_step(f, False, False)

    block_step(nf - 1, False, True)


def _convffn_ln(x2d, w_up_bf, w_ffn_conv, b_ffn_conv_row, w_down_bf, ln_g, ln_b, *,
                alpha, seq, d_ff):
    m, d = x2d.shape
    tm, tf = TM_FFN, TF_FFN
    nf = d_ff // tf
    kern = functools.partial(_convffn_ln_kernel, alpha=alpha, tiles_per_seq=seq // tm,
                             chunk=TR_FFN, nf=nf)
    return pl.pallas_call(
        kern,
        out_shape=jax.ShapeDtypeStruct((m, d), F32),
        grid=(m // tm,),
        in_specs=[
            pl.BlockSpec((tm, d), lambda i: (i, 0)),
            pl.BlockSpec(memory_space=pl.ANY),
            pl.BlockSpec((CONV_WIDTH, 2 * d_ff), lambda i: (0, 0)),
            pl.BlockSpec((1, 2 * d_ff), lambda i: (0, 0)),
            pl.BlockSpec(memory_space=pl.ANY),
            pl.BlockSpec((1, d), lambda i: (0, 0)),
            pl.BlockSpec((1, d), lambda i: (0, 0)),
        ],
        out_specs=pl.BlockSpec((tm, d), lambda i: (i, 0)),
        scratch_shapes=[
            pltpu.VMEM((tm, d), BF16),
            pltpu.VMEM((tm, d), F32),
            pltpu.VMEM((nf, SUBLANES, tf), F32),
            pltpu.VMEM((nf, SUBLANES, tf), F32),
            pltpu.VMEM((tm + SUBLANES, tf), F32),
            pltpu.VMEM((tm + SUBLANES, tf), F32),
            pltpu.VMEM((WEIGHT_SLOTS, d, tf), BF16),
            pltpu.VMEM((WEIGHT_SLOTS, d, tf), BF16),
            pltpu.VMEM((WEIGHT_SLOTS, tf, d), BF16),
            pltpu.SemaphoreType.DMA((N_WEIGHT_STREAMS, WEIGHT_SLOTS)),
        ],
        compiler_params=_params(("arbitrary",)),
        name="convffn_ln",
    )(x2d, w_up_bf, w_ffn_conv, b_ffn_conv_row, w_down_bf, ln_g, ln_b)


def kernel(x, w_in, b_gates, w_sc_conv, mh_gain, w_out, ln1_g, ln1_b,
           w_up, w_ffn_conv, b_ffn_conv, w_down, ln2_g, ln2_b):
    batch, seq, d = x.shape
    depth = w_in.shape[0]
    nh = N_HEADS_M
    w_conv = w_sc_conv.shape[-1]
    w_mlstm = mh_gain.shape[-1]
    dh = w_mlstm // nh
    d_ff = w_down.shape[1]
    n_gates = 2 * nh
    gate_col0 = 3 * w_conv + 4 * w_mlstm
    assert w_in.shape[-1] == gate_col0 + n_gates and w_conv + w_mlstm == d
    assert seq % TM_CONV == 0 and seq % TM_FFN == 0 and seq % MLSTM_CHUNK == 0
    assert w_conv % TC_CONV == 0 and d_ff % TF_FFN == 0
    assert (3 * w_conv) % TN_QKVO == 0 and (4 * w_mlstm) % TN_QKVO == 0
    assert (batch * seq) % TM_QKVO == 0
    alpha = (2 * depth) ** 0.25

    x2d = x.reshape(batch * seq, d)
    for l in range(depth):
        w_in_t = w_in[l].T
        w_gates_t_bf = jnp.pad(w_in_t[gate_col0:], ((0, LANES - n_gates), (0, 0))).astype(BF16)
        b_gates_row = jnp.pad(b_gates[l], (0, LANES - n_gates)).reshape(1, LANES)

        y_conv, xb2d, w_down_bf, w_out_bf = _inproj_conv(
            x2d, w_in_t, w_sc_conv[l], w_down[l], w_out[l], seq=seq, w_conv=w_conv)
        qkvo, w_up_bf = _inproj_qkvo(xb2d, w_in_t, w_up[l], col0=3 * w_conv, width=4 * w_mlstm,
                                     tf=TF_FFN)
        x2d = _mlstm_outproj(qkvo, xb2d, w_gates_t_bf, b_gates_row, mh_gain[l].reshape(1, w_mlstm),
                             x2d, y_conv, w_out_bf, ln1_g[l].reshape(1, d), ln1_b[l].reshape(1, d),
                             seq=seq, nh=nh, dh=dh, alpha=alpha)
        x2d = _convffn_ln(x2d, w_up_bf, w_ffn_conv[l],
                          b_ffn_conv[l].reshape(1, 2 * d_ff), w_down_bf,
                          ln2_g[l].reshape(1, d), ln2_b[l].reshape(1, d),
                          alpha=alpha, seq=seq, d_ff=d_ff)
    return x2d.reshape(batch, seq, d)
```

```python
import functools

import jax
import jax.numpy as jnp
from jax import lax
from jax.experimental import pallas as pl
from jax.experimental.pallas import tpu as pltpu

F32 = jnp.float32
BF16 = jnp.bfloat16

N_HEADS_M = 4
CONV_WIDTH = 3
LN_EPS = 1e-5
HEAD_NORM_EPS = 1e-6

SUBLANES = 8
LANES = 128
BF16_SUBLANES = 16
VMEM_LIMIT_BYTES = 56 * 1024 * 1024

TM_CONV = 512
TC_CONV = 512
TR_CONV = 256
TM_QKVO = 1024
TN_QKVO = 1024
INPUT_SLOTS = 3
MLSTM_CHUNK = 256
HEAD_GROUP = 2
OUT_COL_BLOCK = 512
TM_FFN = 512
TF_FFN = 512
TR_FFN = 256
WEIGHT_SLOTS = 3
N_WEIGHT_STREAMS = 3


def _dot(a, b):
    return jnp.dot(a, b, preferred_element_type=F32)


def _dot_t(a, b_t):
    return lax.dot_general(a, b_t, (((1,), (1,)), ((), ())), preferred_element_type=F32)


def _params(semantics):
    return pltpu.CompilerParams(dimension_semantics=semantics,
                                vmem_limit_bytes=VMEM_LIMIT_BYTES)


def _cast_rows_spec(rows, cols, n_steps, step_of):
    assert rows % n_steps == 0 and (rows // n_steps) % BF16_SUBLANES == 0
    return pl.BlockSpec((rows // n_steps, cols), lambda *ids: (step_of(*ids), 0))


def _inproj_conv_kernel(x_ref, wb_ref, wc_ref, wh_ref, cw_ref, wdn_ref, wout_ref,
                        y_ref, xb_ref, wdn_bf_ref, wout_bf_ref,
                        wbf_ref, carry_ref, buf_ref, *, tiles_per_seq, chunk):
    i = pl.program_id(1)
    rows = x_ref.shape[0]

    @pl.when(i == 0)
    def _():
        wbf_ref[0] = wb_ref[...].astype(BF16)
        wbf_ref[1] = wc_ref[...].astype(BF16)
        wbf_ref[2] = wh_ref[...].astype(BF16)

    @pl.when(pl.program_id(0) == 0)
    def _():
        xb_ref[...] = x_ref[...].astype(BF16)

    @pl.when(i % tiles_per_seq == 0)
    def _():
        carry_ref[...] = jnp.zeros(carry_ref.shape, F32)

    wdn_bf_ref[...] = wdn_ref[...].astype(BF16)
    wout_bf_ref[...] = wout_ref[...].astype(BF16)

    buf_ref[0:SUBLANES, :] = carry_ref[...]
    w = cw_ref[...]
    row_starts = list(range(0, rows, chunk))
    xbs = [x_ref[r0:r0 + chunk, :].astype(BF16) for r0 in row_starts]
    for r0, xb in zip(row_starts, xbs):
        u = _dot_t(xb, wbf_ref[1]) * _dot_t(xb, wbf_ref[2])
        buf_ref[SUBLANES + r0:SUBLANES + r0 + chunk, :] = u
    c_bs = [_dot_t(xb, wbf_ref[0]) for xb in xbs]
    for r0, c_b in zip(row_starts, c_bs):
        u = buf_ref[SUBLANES + r0:SUBLANES + r0 + chunk, :]
        um2 = buf_ref[SUBLANES - 2 + r0:SUBLANES - 2 + r0 + chunk, :]
        um1 = buf_ref[SUBLANES - 1 + r0:SUBLANES - 1 + r0 + chunk, :]
        y = w[0:1, :] * um2 + w[1:2, :] * um1 + w[2:3, :] * u
        y_ref[r0:r0 + chunk, :] = (c_b * y).astype(y_ref.dtype)
    carry_ref[...] = buf_ref[rows:rows + SUBLANES, :]


def _inproj_conv(x2d, w_in_t, w_sc_conv, w_down, w_out, *, seq, w_conv):
    m, d = x2d.shape
    tm, tc = TM_CONV, TC_CONV
    nj, ni = w_conv // tc, m // tm
    n_steps = nj * ni
    step_of = lambda j, i: j * ni + i
    kern = functools.partial(_inproj_conv_kernel, tiles_per_seq=seq // tm, chunk=TR_CONV)
    return pl.pallas_call(
        kern,
        out_shape=(jax.ShapeDtypeStruct((m, w_conv), BF16),
                   jax.ShapeDtypeStruct((m, d), BF16),
                   jax.ShapeDtypeStruct(w_down.shape, BF16),
                   jax.ShapeDtypeStruct(w_out.shape, BF16)),
        grid=(nj, ni),
        in_specs=[
            pl.BlockSpec((tm, d), lambda j, i: (i, 0)),
            pl.BlockSpec((tc, d), lambda j, i: (j, 0)),
            pl.BlockSpec((tc, d), lambda j, i: (nj + j, 0)),
            pl.BlockSpec((tc, d), lambda j, i: (2 * nj + j, 0)),
            pl.BlockSpec((CONV_WIDTH, tc), lambda j, i: (0, j)),
            _cast_rows_spec(*w_down.shape, n_steps, step_of),
            _cast_rows_spec(*w_out.shape, n_steps, step_of),
        ],
        out_specs=(pl.BlockSpec((tm, tc), lambda j, i: (i, j)),
                   pl.BlockSpec((tm, d), lambda j, i: (jnp.where(j == 0, i, ni - 1), 0)),
                   _cast_rows_spec(*w_down.shape, n_steps, step_of),
                   _cast_rows_spec(*w_out.shape, n_steps, step_of)),
        scratch_shapes=[
            pltpu.VMEM((3, tc, d), BF16),
            pltpu.VMEM((SUBLANES, tc), F32),
            pltpu.VMEM((tm + SUBLANES, tc), F32),
        ],
        compiler_params=_params(("arbitrary", "arbitrary")),
        name="inproj_shortconv",
    )(x2d, w_in_t, w_in_t, w_in_t, w_sc_conv, w_down, w_out)


def _inproj_qkvo_kernel(xb_hbm, w_ref, wup_hbm, o_ref, wup_bf_ref, wbf_ref, xb_ring, wup_ring, sems):
    ni = pl.num_programs(1)
    n_steps = pl.num_programs(0) * ni
    step_id = pl.program_id(0) * ni + pl.program_id(1)
    slot = step_id % INPUT_SLOTS
    tm = xb_ring.shape[1]
    up_rows = wup_ring.shape[1]

    def input_copies(to_slot, step):
        row_tile = step % ni
        return (
            pltpu.make_async_copy(xb_hbm.at[pl.ds(pl.multiple_of(row_tile * tm, tm), tm), :],
                                  xb_ring.at[to_slot], sems.at[0, to_slot]),
            pltpu.make_async_copy(wup_hbm.at[pl.ds(pl.multiple_of(step * up_rows, up_rows), up_rows), :],
                                  wup_ring.at[to_slot], sems.at[1, to_slot]),
        )

    @pl.when(step_id == 0)
    def _():
        for ahead in range(INPUT_SLOTS - 1):
            for copy in input_copies(ahead, ahead):
                copy.start()

    for copy in input_copies(slot, step_id):
        copy.wait()

    @pl.when(step_id + (INPUT_SLOTS - 1) < n_steps)
    def _():
        ahead = step_id + (INPUT_SLOTS - 1)
        for copy in input_copies(ahead % INPUT_SLOTS, ahead):
            copy.start()

    @pl.when(pl.program_id(1) == 0)
    def _():
        wbf_ref[...] = w_ref[...].astype(BF16)

    wup_ref = wup_ring.at[slot]
    n_blocks, _, tf = wup_bf_ref.shape
    for n in range(n_blocks):
        wup_bf_ref[n] = wup_ref[:, n * tf:(n + 1) * tf].astype(BF16)
    o_ref[...] = _dot_t(xb_ring[slot], wbf_ref[...]).astype(o_ref.dtype)


def _inproj_qkvo(xb2d, w_in_t, w_up, *, col0, width, tf):
    m, d = xb2d.shape
    tm, tn = TM_QKVO, TN_QKVO
    j0 = col0 // tn
    nj, ni = width // tn, m // tm
    n_steps = nj * ni
    step_of = lambda j, i: j * ni + i
    up_rows, up_cols = w_up.shape
    n_blocks = up_cols // tf
    rows_per_step = up_rows // n_steps
    assert up_cols % tf == 0 and up_rows % n_steps == 0 and rows_per_step % BF16_SUBLANES == 0
    return pl.pallas_call(
        _inproj_qkvo_kernel,
        out_shape=(jax.ShapeDtypeStruct((m, width), BF16),
                   jax.ShapeDtypeStruct((n_blocks, up_rows, tf), BF16)),
        grid=(nj, ni),
        in_specs=[
            pl.BlockSpec(memory_space=pl.ANY),
            pl.BlockSpec((tn, d), lambda j, i: (j0 + j, 0)),
            pl.BlockSpec(memory_space=pl.ANY),
        ],
        out_specs=(pl.BlockSpec((tm, tn), lambda j, i: (i, j)),
                   pl.BlockSpec((n_blocks, rows_per_step, tf),
                                lambda j, i: (0, step_of(j, i), 0))),
        scratch_shapes=[pltpu.VMEM((tn, d), BF16),
                        pltpu.VMEM((INPUT_SLOTS, tm, d), BF16),
                        pltpu.VMEM((INPUT_SLOTS, rows_per_step, up_cols), F32),
                        pltpu.SemaphoreType.DMA((2, INPUT_SLOTS))],
        compiler_params=_params(("arbitrary", "arbitrary")),
        name="inproj_mlstm",
    )(xb2d, w_in_t, w_up)


def _log_sigmoid(x):
    return jnp.minimum(x, 0.0) - jnp.log1p(jnp.exp(-jnp.abs(x)))


def _layer_norm(z, g, b):
    mu = jnp.mean(z, axis=-1, keepdims=True)
    zc = z - mu
    var = jnp.mean(zc * zc, axis=-1, keepdims=True)
    return zc * lax.rsqrt(var + LN_EPS) * g + b


def _mlstm_outproj_kernel(*refs, nh, dh, chunks_per_seq, alpha, n_split):
    (q_ref, k_ref, v_ref, o_ref, xb0_ref, xbn_ref, wg_ref, bg_ref, gain_ref,
     x_ref, yc_ref) = refs[:11]
    wc_refs = refs[11:11 + n_split]
    wm_refs = refs[11 + n_split:11 + 2 * n_split]
    (lng_ref, lnb_ref, x1_ref,
     c_ref, n_ref, m_ref, gates_ref, gates_t_ref, ym_ref) = refs[11 + 2 * n_split:]
    chunk = q_ref.shape[0]
    step = pl.program_id(0)

    @pl.when(step % chunks_per_seq == 0)
    def _():
        c_ref[...] = jnp.zeros(c_ref.shape, F32)
        n_ref[...] = jnp.zeros(n_ref.shape, F32)
        m_ref[...] = jnp.zeros(m_ref.shape, F32)

    row = lax.broadcasted_iota(jnp.int32, (chunk, chunk), 0)
    col = lax.broadcasted_iota(jnp.int32, (chunk, chunk), 1)
    causal = row >= col

    def gate_terms(xb):
        g = _dot_t(xb, wg_ref[...]) + bg_ref[...]
        lf = _log_sigmoid(g)
        lf_hi = lf.astype(BF16)
        lf_r1 = lf - lf_hi.astype(F32)
        lf_mid = lf_r1.astype(BF16)
        lf_lo = (lf_r1 - lf_mid.astype(F32)).astype(BF16)
        parts = _dot(causal.astype(BF16), jnp.concatenate([lf_hi, lf_mid, lf_lo], axis=1))
        bcum = parts[:, :LANES] + parts[:, LANES:2 * LANES] + parts[:, 2 * LANES:]
        gates_ref[0] = g
        gates_ref[1] = bcum
        gates_t_ref[0] = g.T
        gates_t_ref[1] = bcum.T

    @pl.when(step == 0)
    def _():
        gate_terms(xb0_ref[...])
        ym_ref[...] = jnp.zeros(ym_ref.shape, ym_ref.dtype)

    g = gates_ref[0]
    bcum = gates_ref[1]
    g_t = gates_t_ref[0]
    bcum_t = gates_t_ref[1]
    neg_inf = jnp.float32(-jnp.inf)
    cols = [slice(h * dh, (h + 1) * dh) for h in range(nh)]

    yc_prev = yc_ref[...]
    ym_prev = ym_ref[...]
    mix = jnp.concatenate(
        [_dot(yc_prev, wc[...]) + _dot(ym_prev, wm[...])
         for wc, wm in zip(wc_refs, wm_refs)], axis=1)

    def recurrence(heads):
        q = {h: q_ref[:, cols[h]] * jnp.asarray(dh ** -0.5, BF16) for h in heads}
        k = {h: k_ref[:, cols[h]] for h in heads}
        v = {h: v_ref[:, cols[h]] for h in heads}
        i_col = {h: g[:, h:h + 1] for h in heads}
        b_col = {h: bcum[:, nh + h:nh + h + 1] for h in heads}
        i_row = {h: g_t[h:h + 1, :] for h in heads}
        b_row = {h: bcum_t[nh + h:nh + h + 1, :] for h in heads}
        g_tot = {h: b_col[h][chunk - 1:chunk, :] for h in heads}
        m_prev = {h: m_ref[h, 0:1, 0:1] for h in heads}
        c_prev = {h: c_ref[h] for h in heads}
        n_prev = {h: n_ref[h, 0:1, :] for h in heads}

        qk = {h: _dot_t(q[h], k[h]) for h in heads}
        inter = {h: _dot(q[h], c_prev[h].astype(BF16)) for h in heads}
        qn = {h: jnp.sum(q[h].astype(F32) * n_prev[h], axis=-1, keepdims=True) for h in heads}

        log_d = {h: jnp.where(causal, b_col[h] - (b_row[h] - i_row[h]), neg_inf) for h in heads}
        inter_log = {h: b_col[h] + m_prev[h] for h in heads}
        m_t = {h: jnp.maximum(inter_log[h], jnp.max(log_d[h], axis=-1, keepdims=True))
               for h in heads}
        scores = {h: qk[h] * jnp.exp(log_d[h] - m_t[h]) for h in heads}
        inter_w = {h: jnp.exp(inter_log[h] - m_t[h]) for h in heads}
        num = {h: _dot(scores[h].astype(BF16), v[h]) + inter_w[h] * inter[h] for h in heads}
        den = {h: jnp.sum(scores[h], axis=-1, keepdims=True) + inter_w[h] * qn[h] for h in heads}
        hh = {h: num[h] / jnp.maximum(jnp.abs(den[h]), jnp.exp(-m_t[h])) for h in heads}

        w_log = {h: g_tot[h] - b_col[h] + i_col[h] for h in heads}
        m_new = {h: jnp.maximum(g_tot[h] + m_prev[h], jnp.max(w_log[h], axis=0, keepdims=True))
                 for h in heads}
        decay = {h: jnp.exp(g_tot[h] + m_prev[h] - m_new[h]) for h in heads}
        kw = {h: k[h].astype(F32) * jnp.exp(w_log[h] - m_new[h]) for h in heads}
        for h in heads:
            c_ref[h] = decay[h] * c_prev[h] + lax.dot_general(
                kw[h].astype(BF16), v[h], (((0,), (0,)), ((), ())), preferred_element_type=F32)
            n_ref[h, 0:1, :] = decay[h] * n_prev[h] + jnp.sum(kw[h], axis=0, keepdims=True)
            m_ref[h] = jnp.broadcast_to(m_new[h], m_ref.shape[1:])

        return hh

    def head_norm_gate(heads, hh):
        mu = {h: jnp.mean(hh[h], axis=-1, keepdims=True) for h in heads}
        hc = {h: hh[h] - mu[h] for h in heads}
        var = {h: jnp.mean(hc[h] * hc[h], axis=-1, keepdims=True) for h in heads}
        for h in heads:
            hn = hc[h] * lax.rsqrt(var[h] + HEAD_NORM_EPS) * gain_ref[:, cols[h]]
            o = o_ref[:, cols[h]].astype(F32)
            ym_ref[:, cols[h]] = (jax.nn.sigmoid(o) * hn).astype(ym_ref.dtype)

    groups = [tuple(range(h0, min(h0 + HEAD_GROUP, nh))) for h0 in range(0, nh, HEAD_GROUP)]
    for g_idx, group in enumerate(groups):
        head_norm_gate(group, recurrence(group))
        if g_idx == 0:
            z = alpha * x_ref[...] + mix
            x1_ref[...] = _layer_norm(z, lng_ref[...], lnb_ref[...])

    gate_terms(xbn_ref[...])


def _mlstm_outproj(qkvo, xb2d, w_gates_t_bf, b_gates_row, gain_row, x2d, y_conv, w_out_bf,
                   ln_g, ln_b, *, seq, nh, dh, alpha):
    m, d = x2d.shape
    w = nh * dh
    wc = y_conv.shape[1]
    assert wc + w == w_out_bf.shape[0] and wc == w
    chunk = MLSTM_CHUNK
    n_chunks = m // chunk
    last = n_chunks - 1

    def cur(s):
        return jnp.minimum(s, last)

    def prev(s):
        return jnp.maximum(s - 1, 0)

    def blk(kind):
        return pl.BlockSpec((chunk, w), lambda s: (cur(s), kind))

    const = pl.Buffered(1)
    piece = OUT_COL_BLOCK
    assert d % piece == 0
    n_split = d // piece
    kern = functools.partial(_mlstm_outproj_kernel, nh=nh, dh=dh, chunks_per_seq=seq // chunk,
                             alpha=alpha, n_split=n_split)
    return pl.pallas_call(
        kern,
        out_shape=jax.ShapeDtypeStruct((m, d), F32),
        grid=(n_chunks + 1,),
        in_specs=[blk(0), blk(1), blk(2), blk(3),
                  pl.BlockSpec((chunk, d), lambda s: (0, 0)),
                  pl.BlockSpec((chunk, d), lambda s: (jnp.minimum(s + 1, last), 0)),
                  pl.BlockSpec((LANES, d), lambda s: (0, 0)),
                  pl.BlockSpec((1, LANES), lambda s: (0, 0)),
                  pl.BlockSpec((1, w), lambda s: (0, 0)),
                  pl.BlockSpec((chunk, d), lambda s: (prev(s), 0)),
                  pl.BlockSpec((chunk, wc), lambda s: (prev(s), 0)),
                  *[pl.BlockSpec((wc, piece), lambda s, p=p: (0, p), pipeline_mode=const)
                    for p in range(n_split)],
                  *[pl.BlockSpec((w, piece), lambda s, p=p: (1, p), pipeline_mode=const)
                    for p in range(n_split)],
                  pl.BlockSpec((1, d), lambda s: (0, 0)),
                  pl.BlockSpec((1, d), lambda s: (0, 0))],
        out_specs=pl.BlockSpec((chunk, d), lambda s: (prev(s), 0)),
        scratch_shapes=[
            pltpu.VMEM((nh, dh, dh), F32),
            pltpu.VMEM((nh, SUBLANES, dh), F32),
            pltpu.VMEM((nh, SUBLANES, LANES), F32),
            pltpu.VMEM((2, chunk, LANES), F32),
            pltpu.VMEM((2, LANES, chunk), F32),
            pltpu.VMEM((chunk, w), BF16),
        ],
        compiler_params=_params(("arbitrary",)),
        name="mlstm_outproj_ln",
    )(qkvo, qkvo, qkvo, qkvo, xb2d, xb2d, w_gates_t_bf, b_gates_row, gain_row,
      x2d, y_conv, *([w_out_bf] * (2 * n_split)), ln_g, ln_b)


def _convffn_ln_kernel(x_ref, wup_hbm, cw_ref, cb_ref, wdn_hbm,
                       g_ref, b_ref, o_ref, xb_ref, acc_ref, carry_v_ref, carry_g_ref,
                       buf_v_ref, buf_g_ref, wv_ring, wg_ring, wd_ring, sems, *,
                       alpha, tiles_per_seq, chunk, nf):
    i = pl.program_id(0)
    n_steps = pl.num_programs(0) * nf
    rows = x_ref.shape[0]
    tf = wv_ring.shape[2]
    first = (i % tiles_per_seq) == 0

    def weight_copies(to_slot, blk):
        return (
            pltpu.make_async_copy(wup_hbm.at[blk], wv_ring.at[to_slot], sems.at[0, to_slot]),
            pltpu.make_async_copy(wup_hbm.at[nf + blk], wg_ring.at[to_slot], sems.at[1, to_slot]),
            pltpu.make_async_copy(wdn_hbm.at[pl.ds(pl.multiple_of(blk * tf, tf), tf), :],
                                  wd_ring.at[to_slot], sems.at[2, to_slot]),
        )

    @pl.when(i == 0)
    def _():
        for ahead in range(WEIGHT_SLOTS - 1):
            for copy in weight_copies(ahead, ahead % nf):
                copy.start()

    def conv(buf_ref, cols, r0):
        h = buf_ref[SUBLANES + r0:SUBLANES + r0 + chunk, :]
        hm1 = buf_ref[SUBLANES - 1 + r0:SUBLANES - 1 + r0 + chunk, :]
        hm2 = buf_ref[SUBLANES - 2 + r0:SUBLANES - 2 + r0 + chunk, :]
        w = cw_ref[:, cols]
        return w[0:1, :] * hm2 + w[1:2, :] * hm1 + w[2:3, :] * h + cb_ref[:, cols]

    def block_step(f, is_first, is_last):
        step_id = i * nf + f
        slot = step_id % WEIGHT_SLOTS
        for copy in weight_copies(slot, f):
            copy.wait()

        @pl.when(step_id + (WEIGHT_SLOTS - 1) < n_steps)
        def _():
            ahead = step_id + (WEIGHT_SLOTS - 1)
            for copy in weight_copies(ahead % WEIGHT_SLOTS, ahead % nf):
                copy.start()

        wv_ref = wv_ring.at[slot]
        wg_ref = wg_ring.at[slot]
        wd_ref = wd_ring.at[slot]
        cols_v = pl.ds(pl.multiple_of(f * tf, tf), tf)
        cols_g = pl.ds(pl.multiple_of((nf + f) * tf, tf), tf)

        @pl.when(first)
        def _():
            carry_v_ref[f] = jnp.zeros(carry_v_ref.shape[1:], F32)
            carry_g_ref[f] = jnp.zeros(carry_g_ref.shape[1:], F32)

        row_starts = list(range(0, rows, chunk))
        buf_v_ref[0:SUBLANES, :] = carry_v_ref[f]
        buf_g_ref[0:SUBLANES, :] = carry_g_ref[f]
        for r0 in row_starts:
            if is_first:
                xb_ref[r0:r0 + chunk, :] = x_ref[r0:r0 + chunk, :].astype(BF16)
            xb = xb_ref[r0:r0 + chunk, :]
            buf_v_ref[SUBLANES + r0:SUBLANES + r0 + chunk, :] = _dot(xb, wv_ref[...])
            buf_g_ref[SUBLANES + r0:SUBLANES + r0 + chunk, :] = _dot(xb, wg_ref[...])
        for r0 in row_starts:
            rs = slice(r0, r0 + chunk)
            val = conv(buf_v_ref, cols_v, r0)
            gate = conv(buf_g_ref, cols_g, r0)
            act = (jax.nn.silu(gate) * val).astype(BF16)
            part = _dot(act, wd_ref[...])
            if is_first:
                acc_ref[rs, :] = part
            elif is_last:
                z = alpha * x_ref[rs, :] + (acc_ref[rs, :] + part)
                o_ref[rs, :] = _layer_norm(z, g_ref[...], b_ref[...])
            else:
                acc_ref[rs, :] += part
        carry_v_ref[f] = buf_v_ref[rows:rows + SUBLANES, :]
        carry_g_ref[f] = buf_g_ref[rows:rows + SUBLANES, :]

    block_step(0, True, False)

    @pl.loop(1, nf - 1)
    def _(f):
        block_step(f, False, False)

    block_step(nf - 1, False, True)


def _convffn_ln(x2d, w_up_bf, w_ffn_conv, b_ffn_conv_row, w_down_bf, ln_g, ln_b, *,
                alpha, seq, d_ff):
    m, d = x2d.shape
    tm, tf = TM_FFN, TF_FFN
    nf = d_ff // tf
    kern = functools.partial(_convffn_ln_kernel, alpha=alpha, tiles_per_seq=seq // tm,
                             chunk=TR_FFN, nf=nf)
    return pl.pallas_call(
        kern,
        out_shape=jax.ShapeDtypeStruct((m, d), F32),
        grid=(m // tm,),
        in_specs=[
            pl.BlockSpec((tm, d), lambda i: (i, 0)),
            pl.BlockSpec(memory_space=pl.ANY),
            pl.BlockSpec((CONV_WIDTH, 2 * d_ff), lambda i: (0, 0)),
            pl.BlockSpec((1, 2 * d_ff), lambda i: (0, 0)),
            pl.BlockSpec(memory_space=pl.ANY),
            pl.BlockSpec((1, d), lambda i: (0, 0)),
            pl.BlockSpec((1, d), lambda i: (0, 0)),
        ],
        out_specs=pl.BlockSpec((tm, d), lambda i: (i, 0)),
        scratch_shapes=[
            pltpu.VMEM((tm, d), BF16),
            pltpu.VMEM((tm, d), F32),
            pltpu.VMEM((nf, SUBLANES, tf), F32),
            pltpu.VMEM((nf, SUBLANES, tf), F32),
            pltpu.VMEM((tm + SUBLANES, tf), F32),
            pltpu.VMEM((tm + SUBLANES, tf), F32),
            pltpu.VMEM((WEIGHT_SLOTS, d, tf), BF16),
            pltpu.VMEM((WEIGHT_SLOTS, d, tf), BF16),
            pltpu.VMEM((WEIGHT_SLOTS, tf, d), BF16),
            pltpu.SemaphoreType.DMA((N_WEIGHT_STREAMS, WEIGHT_SLOTS)),
        ],
        compiler_params=_params(("arbitrary",)),
        name="convffn_ln",
    )(x2d, w_up_bf, w_ffn_conv, b_ffn_conv_row, w_down_bf, ln_g, ln_b)


def kernel(x, w_in, b_gates, w_sc_conv, mh_gain, w_out, ln1_g, ln1_b,
           w_up, w_ffn_conv, b_ffn_conv, w_down, ln2_g, ln2_b):
    batch, seq, d = x.shape
    depth = w_in.shape[0]
    nh = N_HEADS_M
    w_conv = w_sc_conv.shape[-1]
    w_mlstm = mh_gain.shape[-1]
    dh = w_mlstm // nh
    d_ff = w_down.shape[1]
    n_gates = 2 * nh
    gate_col0 = 3 * w_conv + 4 * w_mlstm
    assert w_in.shape[-1] == gate_col0 + n_gates and w_conv + w_mlstm == d
    assert seq % TM_CONV == 0 and seq % TM_FFN == 0 and seq % MLSTM_CHUNK == 0
    assert w_conv % TC_CONV == 0 and d_ff % TF_FFN == 0
    assert (3 * w_conv) % TN_QKVO == 0 and (4 * w_mlstm) % TN_QKVO == 0
    assert (batch * seq) % TM_QKVO == 0
    alpha = (2 * depth) ** 0.25

    x2d = x.reshape(batch * seq, d)
    for l in range(depth):
        w_in_t = w_in[l].T
        w_gates_t_bf = jnp.pad(w_in_t[gate_col0:], ((0, LANES - n_gates), (0, 0))).astype(BF16)
        b_gates_row = jnp.pad(b_gates[l], (0, LANES - n_gates)).reshape(1, LANES)

        y_conv, xb2d, w_down_bf, w_out_bf = _inproj_conv(
            x2d, w_in_t, w_sc_conv[l], w_down[l], w_out[l], seq=seq, w_conv=w_conv)
        qkvo, w_up_bf = _inproj_qkvo(xb2d, w_in_t, w_up[l], col0=3 * w_conv, width=4 * w_mlstm,
                                     tf=TF_FFN)
        x2d = _mlstm_outproj(qkvo, xb2d, w_gates_t_bf, b_gates_row, mh_gain[l].reshape(1, w_mlstm),
                             x2d, y_conv, w_out_bf, ln1_g[l].reshape(1, d), ln1_b[l].reshape(1, d),
                             seq=seq, nh=nh, dh=dh, alpha=alpha)
        x2d = _convffn_ln(x2d, w_up_bf, w_ffn_conv[l],
                          b_ffn_conv[l].reshape(1, 2 * d_ff), w_down_bf,
                          ln2_g[l].reshape(1, d), ln2_b[l].reshape(1, d),
                          alpha=alpha, seq=seq, d_ff=d_ff)
    return x2d.reshape(batch, seq, d)
```

```python
import functools

import jax
import jax.numpy as jnp
from jax import lax
from jax.experimental import pallas as pl
from jax.experimental.pallas import tpu as pltpu

F32 = jnp.float32
BF16 = jnp.bfloat16

N_HEADS_M = 4
CONV_WIDTH = 3
LN_EPS = 1e-5
HEAD_NORM_EPS = 1e-6

SUBLANES = 8
LANES = 128
BF16_SUBLANES = 16
VMEM_LIMIT_BYTES = 56 * 1024 * 1024

TM_CONV = 512
TC_CONV = 512
TR_CONV = 256
TM_QKVO = 1024
TN_QKVO = 1024
INPUT_SLOTS = 4
MLSTM_CHUNK = 256
HEAD_GROUP = 2
OUT_COL_BLOCK = 512
TM_FFN = 512
TF_FFN = 512
TR_FFN = 256
WEIGHT_SLOTS = 3
N_WEIGHT_STREAMS = 3


def _dot(a, b):
    return jnp.dot(a, b, preferred_element_type=F32)


def _dot_t(a, b_t):
    return lax.dot_general(a, b_t, (((1,), (1,)), ((), ())), preferred_element_type=F32)


def _params(semantics):
    return pltpu.CompilerParams(dimension_semantics=semantics,
                                vmem_limit_bytes=VMEM_LIMIT_BYTES)


def _cast_rows_spec(rows, cols, n_steps, step_of):
    assert rows % n_steps == 0 and (rows // n_steps) % BF16_SUBLANES == 0
    return pl.BlockSpec((rows // n_steps, cols), lambda *ids: (step_of(*ids), 0))


def _inproj_conv_kernel(x_ref, wb_ref, wc_ref, wh_ref, cw_ref, wdn_ref, wout_ref,
                        y_ref, xb_ref, wdn_bf_ref, wout_bf_ref,
                        wbf_ref, carry_ref, buf_ref, *, tiles_per_seq, chunk):
    i = pl.program_id(1)
    rows = x_ref.shape[0]

    @pl.when(i == 0)
    def _():
        wbf_ref[0] = wb_ref[...].astype(BF16)
        wbf_ref[1] = wc_ref[...].astype(BF16)
        wbf_ref[2] = wh_ref[...].astype(BF16)

    @pl.when(pl.program_id(0) == 0)
    def _():
        xb_ref[...] = x_ref[...].astype(BF16)

    @pl.when(i % tiles_per_seq == 0)
    def _():
        carry_ref[...] = jnp.zeros(carry_ref.shape, F32)

    wdn_bf_ref[...] = wdn_ref[...].astype(BF16)
    wout_bf_ref[...] = wout_ref[...].astype(BF16)

    buf_ref[0:SUBLANES, :] = carry_ref[...]
    w = cw_ref[...]
    row_starts = list(range(0, rows, chunk))
    xbs = [x_ref[r0:r0 + chunk, :].astype(BF16) for r0 in row_starts]
    for r0, xb in zip(row_starts, xbs):
        u = _dot_t(xb, wbf_ref[1]) * _dot_t(xb, wbf_ref[2])
        buf_ref[SUBLANES + r0:SUBLANES + r0 + chunk, :] = u
    c_bs = [_dot_t(xb, wbf_ref[0]) for xb in xbs]
    for r0, c_b in zip(row_starts, c_bs):
        u = buf_ref[SUBLANES + r0:SUBLANES + r0 + chunk, :]
        um2 = buf_ref[SUBLANES - 2 + r0:SUBLANES - 2 + r0 + chunk, :]
        um1 = buf_ref[SUBLANES - 1 + r0:SUBLANES - 1 + r0 + chunk, :]
        y = w[0:1, :] * um2 + w[1:2, :] * um1 + w[2:3, :] * u
        y_ref[r0:r0 + chunk, :] = (c_b * y).astype(y_ref.dtype)
    carry_ref[...] = buf_ref[rows:rows + SUBLANES, :]


def _inproj_conv(x2d, w_in_t, w_sc_conv, w_down, w_out, *, seq, w_conv):
    m, d = x2d.shape
    tm, tc = TM_CONV, TC_CONV
    nj, ni = w_conv // tc, m // tm
    n_steps = nj * ni
    step_of = lambda j, i: j * ni + i
    kern = functools.partial(_inproj_conv_kernel, tiles_per_seq=seq // tm, chunk=TR_CONV)
    return pl.pallas_call(
        kern,
        out_shape=(jax.ShapeDtypeStruct((m, w_conv), BF16),
                   jax.ShapeDtypeStruct((m, d), BF16),
                   jax.ShapeDtypeStruct(w_down.shape, BF16),
                   jax.ShapeDtypeStruct(w_out.shape, BF16)),
        grid=(nj, ni),
        in_specs=[
            pl.BlockSpec((tm, d), lambda j, i: (i, 0)),
            pl.BlockSpec((tc, d), lambda j, i: (j, 0)),
            pl.BlockSpec((tc, d), lambda j, i: (nj + j, 0)),
            pl.BlockSpec((tc, d), lambda j, i: (2 * nj + j, 0)),
            pl.BlockSpec((CONV_WIDTH, tc), lambda j, i: (0, j)),
            _cast_rows_spec(*w_down.shape, n_steps, step_of),
            _cast_rows_spec(*w_out.shape, n_steps, step_of),
        ],
        out_specs=(pl.BlockSpec((tm, tc), lambda j, i: (i, j)),
                   pl.BlockSpec((tm, d), lambda j, i: (jnp.where(j == 0, i, ni - 1), 0)),
                   _cast_rows_spec(*w_down.shape, n_steps, step_of),
                   _cast_rows_spec(*w_out.shape, n_steps, step_of)),
        scratch_shapes=[
            pltpu.VMEM((3, tc, d), BF16),
            pltpu.VMEM((SUBLANES, tc), F32),
            pltpu.VMEM((tm + SUBLANES, tc), F32),
        ],
        compiler_params=_params(("arbitrary", "arbitrary")),
        name="inproj_shortconv",
    )(x2d, w_in_t, w_in_t, w_in_t, w_sc_conv, w_down, w_out)


def _inproj_qkvo_kernel(xb_hbm, w_hbm, wup_hbm, o_ref, wup_bf_ref, wbf_ref, wf32_ref, xb_ring, wup_ring,
                        sems, wsem, *, j0):
    j = pl.program_id(0)
    nj = pl.num_programs(0)
    ni = pl.num_programs(1)
    n_steps = nj * ni
    step_id = j * ni + pl.program_id(1)
    slot = step_id % INPUT_SLOTS
    tm = xb_ring.shape[1]
    up_rows = wup_ring.shape[1]
    tn = wf32_ref.shape[0]

    def weight_copy(block):
        rows = pl.ds(pl.multiple_of((j0 + block) * tn, tn), tn)
        return pltpu.make_async_copy(w_hbm.at[rows, :], wf32_ref, wsem.at[0])

    def input_copies(to_slot, step):
        row_tile = step % ni
        return (
            pltpu.make_async_copy(xb_hbm.at[pl.ds(pl.multiple_of(row_tile * tm, tm), tm), :],
                                  xb_ring.at[to_slot], sems.at[0, to_slot]),
            pltpu.make_async_copy(wup_hbm.at[pl.ds(pl.multiple_of(step * up_rows, up_rows), up_rows), :],
                                  wup_ring.at[to_slot], sems.at[1, to_slot]),
        )

    @pl.when(step_id == 0)
    def _():
        weight_copy(0).start()
        for ahead in range(INPUT_SLOTS - 1):
            for copy in input_copies(ahead, ahead):
                copy.start()

    for copy in input_copies(slot, step_id):
        copy.wait()

    @pl.when(step_id + (INPUT_SLOTS - 1) < n_steps)
    def _():
        ahead = step_id + (INPUT_SLOTS - 1)
        for copy in input_copies(ahead % INPUT_SLOTS, ahead):
            copy.start()

    @pl.when(pl.program_id(1) == 0)
    def _():
        weight_copy(j).wait()
        wbf_ref[...] = wf32_ref[...].astype(BF16)

        @pl.when(j + 1 < nj)
        def _():
            weight_copy(j + 1).start()

    wup_ref = wup_ring.at[slot]
    n_blocks, _, tf = wup_bf_ref.shape
    for n in range(n_blocks):
        wup_bf_ref[n] = wup_ref[:, n * tf:(n + 1) * tf].astype(BF16)
    o_ref[...] = _dot_t(xb_ring[slot], wbf_ref[...]).astype(o_ref.dtype)


def _inproj_qkvo(xb2d, w_in_t, w_up, *, col0, width, tf):
    m, d = xb2d.shape
    tm, tn = TM_QKVO, TN_QKVO
    j0 = col0 // tn
    nj, ni = width // tn, m // tm
    n_steps = nj * ni
    step_of = lambda j, i: j * ni + i
    up_rows, up_cols = w_up.shape
    n_blocks = up_cols // tf
    rows_per_step = up_rows // n_steps
    assert up_cols % tf == 0 and up_rows % n_steps == 0 and rows_per_step % BF16_SUBLANES == 0
    assert col0 % tn == 0
    return pl.pallas_call(
        functools.partial(_inproj_qkvo_kernel, j0=j0),
        out_shape=(jax.ShapeDtypeStruct((m, width), BF16),
                   jax.ShapeDtypeStruct((n_blocks, up_rows, tf), BF16)),
        grid=(nj, ni),
        in_specs=[
            pl.BlockSpec(memory_space=pl.ANY),
            pl.BlockSpec(memory_space=pl.ANY),
            pl.BlockSpec(memory_space=pl.ANY),
        ],
        out_specs=(pl.BlockSpec((tm, tn), lambda j, i: (i, j)),
                   pl.BlockSpec((n_blocks, rows_per_step, tf),
                                lambda j, i: (0, step_of(j, i), 0))),
        scratch_shapes=[pltpu.VMEM((tn, d), BF16),
                        pltpu.VMEM((tn, d), F32),
                        pltpu.VMEM((INPUT_SLOTS, tm, d), BF16),
                        pltpu.VMEM((INPUT_SLOTS, rows_per_step, up_cols), F32),
                        pltpu.SemaphoreType.DMA((2, INPUT_SLOTS)),
                        pltpu.SemaphoreType.DMA((1,))],
        compiler_params=_params(("arbitrary", "arbitrary")),
        name="inproj_mlstm",
    )(xb2d, w_in_t, w_up)


def _log_sigmoid(x):
    return jnp.minimum(x, 0.0) - jnp.log1p(jnp.exp(-jnp.abs(x)))


def _layer_norm(z, g, b):
    mu = jnp.mean(z, axis=-1, keepdims=True)
    zc = z - mu
    var = jnp.mean(zc * zc, axis=-1, keepdims=True)
    return zc * lax.rsqrt(var + LN_EPS) * g + b


def _mlstm_outproj_kernel(*refs, nh, dh, chunks_per_seq, alpha, n_split):
    (q_ref, k_ref, v_ref, o_ref, xb0_ref, xbn_ref, wg_ref, bg_ref, gain_ref,
     x_ref, yc_ref) = refs[:11]
    wc_refs = refs[11:11 + n_split]
    wm_refs = refs[11 + n_split:11 + 2 * n_split]
    (lng_ref, lnb_ref, x1_ref,
     c_ref, n_ref, m_ref, gates_ref, gates_t_ref, ym_ref) = refs[11 + 2 * n_split:]
    chunk = q_ref.shape[0]
    step = pl.program_id(0)

    @pl.when(step % chunks_per_seq == 0)
    def _():
        c_ref[...] = jnp.zeros(c_ref.shape, F32)
        n_ref[...] = jnp.zeros(n_ref.shape, F32)
        m_ref[...] = jnp.zeros(m_ref.shape, F32)

    row = lax.broadcasted_iota(jnp.int32, (chunk, chunk), 0)
    col = lax.broadcasted_iota(jnp.int32, (chunk, chunk), 1)
    causal = row >= col

    def gate_terms(xb):
        g = _dot_t(xb, wg_ref[...]) + bg_ref[...]
        lf = _log_sigmoid(g)
        lf_hi = lf.astype(BF16)
        lf_r1 = lf - lf_hi.astype(F32)
        lf_mid = lf_r1.astype(BF16)
        lf_lo = (lf_r1 - lf_mid.astype(F32)).astype(BF16)
        parts = _dot(causal.astype(BF16), jnp.concatenate([lf_hi, lf_mid, lf_lo], axis=1))
        bcum = parts[:, :LANES] + parts[:, LANES:2 * LANES] + parts[:, 2 * LANES:]
        gates_ref[0] = g
        gates_ref[1] = bcum
        gates_t_ref[0] = g.T
        gates_t_ref[1] = bcum.T

    @pl.when(step == 0)
    def _():
        gate_terms(xb0_ref[...])
        ym_ref[...] = jnp.zeros(ym_ref.shape, ym_ref.dtype)

    g = gates_ref[0]
    bcum = gates_ref[1]
    g_t = gates_t_ref[0]
    bcum_t = gates_t_ref[1]
    neg_inf = jnp.float32(-jnp.inf)
    cols = [slice(h * dh, (h + 1) * dh) for h in range(nh)]

    yc_prev = yc_ref[...]
    ym_prev = ym_ref[...]
    mix = jnp.concatenate(
        [_dot(yc_prev, wc[...]) + _dot(ym_prev, wm[...])
         for wc, wm in zip(wc_refs, wm_refs)], axis=1)

    def recurrence(heads):
        q = {h: q_ref[:, cols[h]] * jnp.asarray(dh ** -0.5, BF16) for h in heads}
        k = {h: k_ref[:, cols[h]] for h in heads}
        v = {h: v_ref[:, cols[h]] for h in heads}
        i_col = {h: g[:, h:h + 1] for h in heads}
        b_col = {h: bcum[:, nh + h:nh + h + 1] for h in heads}
        i_row = {h: g_t[h:h + 1, :] for h in heads}
        b_row = {h: bcum_t[nh + h:nh + h + 1, :] for h in heads}
        g_tot = {h: b_col[h][chunk - 1:chunk, :] for h in heads}
        m_prev = {h: m_ref[h, 0:1, 0:1] for h in heads}
        c_prev = {h: c_ref[h] for h in heads}
        n_prev = {h: n_ref[h, 0:1, :] for h in heads}

        qk = {h: _dot_t(q[h], k[h]) for h in heads}
        inter = {h: _dot(q[h], c_prev[h].astype(BF16)) for h in heads}
        qn = {h: jnp.sum(q[h].astype(F32) * n_prev[h], axis=-1, keepdims=True) for h in heads}

        log_d = {h: jnp.where(causal, b_col[h] - (b_row[h] - i_row[h]), neg_inf) for h in heads}
        inter_log = {h: b_col[h] + m_prev[h] for h in heads}
        m_t = {h: jnp.maximum(inter_log[h], jnp.max(log_d[h], axis=-1, keepdims=True))
               for h in heads}
        scores = {h: qk[h] * jnp.exp(log_d[h] - m_t[h]) for h in heads}
        inter_w = {h: jnp.exp(inter_log[h] - m_t[h]) for h in heads}
        num = {h: _dot(scores[h].astype(BF16), v[h]) + inter_w[h] * inter[h] for h in heads}
        den = {h: jnp.sum(scores[h], axis=-1, keepdims=True) + inter_w[h] * qn[h] for h in heads}
        hh = {h: num[h] / jnp.maximum(jnp.abs(den[h]), jnp.exp(-m_t[h])) for h in heads}

        w_log = {h: g_tot[h] - b_col[h] + i_col[h] for h in heads}
        m_new = {h: jnp.maximum(g_tot[h] + m_prev[h], jnp.max(w_log[h], axis=0, keepdims=True))
                 for h in heads}
        decay = {h: jnp.exp(g_tot[h] + m_prev[h] - m_new[h]) for h in heads}
        kw = {h: k[h].astype(F32) * jnp.exp(w_log[h] - m_new[h]) for h in heads}
        for h in heads:
            c_ref[h] = decay[h] * c_prev[h] + lax.dot_general(
                kw[h].astype(BF16), v[h], (((0,), (0,)), ((), ())), preferred_element_type=F32)
            n_ref[h, 0:1, :] = decay[h] * n_prev[h] + jnp.sum(kw[h], axis=0, keepdims=True)
            m_ref[h] = jnp.broadcast_to(m_new[h], m_ref.shape[1:])

        return hh

    def head_norm_gate(heads, hh):
        mu = {h: jnp.mean(hh[h], axis=-1, keepdims=True) for h in heads}
        hc = {h: hh[h] - mu[h] for h in heads}
        var = {h: jnp.mean(hc[h] * hc[h], axis=-1, keepdims=True) for h in heads}
        for h in heads:
            hn = hc[h] * lax.rsqrt(var[h] + HEAD_NORM_EPS) * gain_ref[:, cols[h]]
            o = o_ref[:, cols[h]].astype(F32)
            ym_ref[:, cols[h]] = (jax.nn.sigmoid(o) * hn).astype(ym_ref.dtype)

    groups = [tuple(range(h0, min(h0 + HEAD_GROUP, nh))) for h0 in range(0, nh, HEAD_GROUP)]
    for g_idx, group in enumerate(groups):
        head_norm_gate(group, recurrence(group))
        if g_idx == 0:
            z = alpha * x_ref[...] + mix
            x1_ref[...] = _layer_norm(z, lng_ref[...], lnb_ref[...])

    gate_terms(xbn_ref[...])


def _mlstm_outproj(qkvo, xb2d, w_gates_t_bf, b_gates_row, gain_row, x2d, y_conv, w_out_bf,
                   ln_g, ln_b, *, seq, nh, dh, alpha):
    m, d = x2d.shape
    w = nh * dh
    wc = y_conv.shape[1]
    assert wc + w == w_out_bf.shape[0] and wc == w
    chunk = MLSTM_CHUNK
    n_chunks = m // chunk
    last = n_chunks - 1

    def cur(s):
        return jnp.minimum(s, last)

    def prev(s):
        return jnp.maximum(s - 1, 0)

    def blk(kind):
        return pl.BlockSpec((chunk, w), lambda s: (cur(s), kind))

    const = pl.Buffered(1)
    piece = OUT_COL_BLOCK
    assert d % piece == 0
    n_split = d // piece
    kern = functools.partial(_mlstm_outproj_kernel, nh=nh, dh=dh, chunks_per_seq=seq // chunk,
                             alpha=alpha, n_split=n_split)
    return pl.pallas_call(
        kern,
        out_shape=jax.ShapeDtypeStruct((m, d), F32),
        grid=(n_chunks + 1,),
        in_specs=[blk(0), blk(1), blk(2), blk(3),
                  pl.BlockSpec((chunk, d), lambda s: (0, 0)),
                  pl.BlockSpec((chunk, d), lambda s: (jnp.minimum(s + 1, last), 0)),
                  pl.BlockSpec((LANES, d), lambda s: (0, 0)),
                  pl.BlockSpec((1, LANES), lambda s: (0, 0)),
                  pl.BlockSpec((1, w), lambda s: (0, 0)),
                  pl.BlockSpec((chunk, d), lambda s: (prev(s), 0)),
                  pl.BlockSpec((chunk, wc), lambda s: (prev(s), 0)),
                  *[pl.BlockSpec((wc, piece), lambda s, p=p: (0, p), pipeline_mode=const)
                    for p in range(n_split)],
                  *[pl.BlockSpec((w, piece), lambda s, p=p: (1, p), pipeline_mode=const)
                    for p in range(n_split)],
                  pl.BlockSpec((1, d), lambda s: (0, 0)),
                  pl.BlockSpec((1, d), lambda s: (0, 0))],
        out_specs=pl.BlockSpec((chunk, d), lambda s: (prev(s), 0)),
        scratch_shapes=[
            pltpu.VMEM((nh, dh, dh), F32),
            pltpu.VMEM((nh, SUBLANES, dh), F32),
            pltpu.VMEM((nh, SUBLANES, LANES), F32),
            pltpu.VMEM((2, chunk, LANES), F32),
            pltpu.VMEM((2, LANES, chunk), F32),
            pltpu.VMEM((chunk, w), BF16),
        ],
        compiler_params=_params(("arbitrary",)),
        name="mlstm_outproj_ln",
    )(qkvo, qkvo, qkvo, qkvo, xb2d, xb2d, w_gates_t_bf, b_gates_row, gain_row,
      x2d, y_conv, *([w_out_bf] * (2 * n_split)), ln_g, ln_b)


def _convffn_ln_kernel(x_ref, wup_hbm, cw_ref, cb_ref, wdn_hbm,
                       g_ref, b_ref, o_ref, xb_ref, acc_ref, carry_v_ref, carry_g_ref,
                       buf_v_ref, buf_g_ref, wv_ring, wg_ring, wd_ring, sems, *,
                       alpha, tiles_per_seq, chunk, nf):
    i = pl.program_id(0)
    n_steps = pl.num_programs(0) * nf
    rows = x_ref.shape[0]
    tf = wv_ring.shape[2]
    first = (i % tiles_per_seq) == 0

    def weight_copies(to_slot, blk):
        return (
            pltpu.make_async_copy(wup_hbm.at[blk], wv_ring.at[to_slot], sems.at[0, to_slot]),
            pltpu.make_async_copy(wup_hbm.at[nf + blk], wg_ring.at[to_slot], sems.at[1, to_slot]),
            pltpu.make_async_copy(wdn_hbm.at[pl.ds(pl.multiple_of(blk * tf, tf), tf), :],
                                  wd_ring.at[to_slot], sems.at[2, to_slot]),
        )

    @pl.when(i == 0)
    def _():
        for ahead in range(WEIGHT_SLOTS - 1):
            for copy in weight_copies(ahead, ahead % nf):
                copy.start()

    def conv(buf_ref, cols, r0):
        h = buf_ref[SUBLANES + r0:SUBLANES + r0 + chunk, :]
        hm1 = buf_ref[SUBLANES - 1 + r0:SUBLANES - 1 + r0 + chunk, :]
        hm2 = buf_ref[SUBLANES - 2 + r0:SUBLANES - 2 + r0 + chunk, :]
        w = cw_ref[:, cols]
        return w[0:1, :] * hm2 + w[1:2, :] * hm1 + w[2:3, :] * h + cb_ref[:, cols]

    def block_step(f, is_first, is_last):
        step_id = i * nf + f
        slot = step_id % WEIGHT_SLOTS
        for copy in weight_copies(slot, f):
            copy.wait()

        @pl.when(step_id + (WEIGHT_SLOTS - 1) < n_steps)
        def _():
            ahead = step_id + (WEIGHT_SLOTS - 1)
            for copy in weight_copies(ahead % WEIGHT_SLOTS, ahead % nf):
                copy.start()

        wv_ref = wv_ring.at[slot]
        wg_ref = wg_ring.at[slot]
        wd_ref = wd_ring.at[slot]
        cols_v = pl.ds(pl.multiple_of(f * tf, tf), tf)
        cols_g = pl.ds(pl.multiple_of((nf + f) * tf, tf), tf)

        @pl.when(first)
        def _():
            carry_v_ref[f] = jnp.zeros(carry_v_ref.shape[1:], F32)
            carry_g_ref[f] = jnp.zeros(carry_g_ref.shape[1:], F32)

        row_starts = list(range(0, rows, chunk))
        buf_v_ref[0:SUBLANES, :] = carry_v_ref[f]
        buf_g_ref[0:SUBLANES, :] = carry_g_ref[f]
        for r0 in row_starts:
            if is_first:
                xb_ref[r0:r0 + chunk, :] = x_ref[r0:r0 + chunk, :].astype(BF16)
            xb = xb_ref[r0:r0 + chunk, :]
            buf_v_ref[SUBLANES + r0:SUBLANES + r0 + chunk, :] = _dot(xb, wv_ref[...])
            buf_g_ref[SUBLANES + r0:SUBLANES + r0 + chunk, :] = _dot(xb, wg_ref[...])
        for r0 in row_starts:
            rs = slice(r0, r0 + chunk)
            val = conv(buf_v_ref, cols_v, r0)
            gate = conv(buf_g_ref, cols_g, r0)
            act = (jax.nn.silu(gate) * val).astype(BF16)
            part = _dot(act, wd_ref[...])
            if is_first:
                acc_ref[rs, :] = part
            elif is_last:
                z = alpha * x_ref[rs, :] + (acc_ref[rs, :] + part)
                o_ref[rs, :] = _layer_norm(z, g_ref[...], b_ref[...])
            else:
                acc_ref[rs, :] += part
        carry_v_ref[f] = buf_v_ref[rows:rows + SUBLANES, :]
        carry_g_ref[f] = buf_g_ref[rows:rows + SUBLANES, :]

    block_step(0, True, False)

    @pl.loop(1, nf - 1)
    def _(f):
        block_step(f, False, False)

    block_step(nf - 1, False, True)


def _convffn_ln(x2d, w_up_bf, w_ffn_conv, b_ffn_conv_row, w_down_bf, ln_g, ln_b, *,
                alpha, seq, d_ff):
    m, d = x2d.shape
    tm, tf = TM_FFN, TF_FFN
    nf = d_ff // tf
    kern = functools.partial(_convffn_ln_kernel, alpha=alpha, tiles_per_seq=seq // tm,
                             chunk=TR_FFN, nf=nf)
    return pl.pallas_call(
        kern,
        out_shape=jax.ShapeDtypeStruct((m, d), F32),
        grid=(m // tm,),
        in_specs=[
            pl.BlockSpec((tm, d), lambda i: (i, 0)),
            pl.BlockSpec(memory_space=pl.ANY),
            pl.BlockSpec((CONV_WIDTH, 2 * d_ff), lambda i: (0, 0)),
            pl.BlockSpec((1, 2 * d_ff), lambda i: (0, 0)),
            pl.BlockSpec(memory_space=pl.ANY),
            pl.BlockSpec((1, d), lambda i: (0, 0)),
            pl.BlockSpec((1, d), lambda i: (0, 0)),
        ],
        out_specs=pl.BlockSpec((tm, d), lambda i: (i, 0)),
        scratch_shapes=[
            pltpu.VMEM((tm, d), BF16),
            pltpu.VMEM((tm, d), F32),
            pltpu.VMEM((nf, SUBLANES, tf), F32),
            pltpu.VMEM((nf, SUBLANES, tf), F32),
            pltpu.VMEM((tm + SUBLANES, tf), F32),
            pltpu.VMEM((tm + SUBLANES, tf), F32),
            pltpu.VMEM((WEIGHT_SLOTS, d, tf), BF16),
            pltpu.VMEM((WEIGHT_SLOTS, d, tf), BF16),
            pltpu.VMEM((WEIGHT_SLOTS, tf, d), BF16),
            pltpu.SemaphoreType.DMA((N_WEIGHT_STREAMS, WEIGHT_SLOTS)),
        ],
        compiler_params=_params(("arbitrary",)),
        name="convffn_ln",
    )(x2d, w_up_bf, w_ffn_conv, b_ffn_conv_row, w_down_bf, ln_g, ln_b)


def kernel(x, w_in, b_gates, w_sc_conv, mh_gain, w_out, ln1_g, ln1_b,
           w_up, w_ffn_conv, b_ffn_conv, w_down, ln2_g, ln2_b):
    batch, seq, d = x.shape
    depth = w_in.shape[0]
    nh = N_HEADS_M
    w_conv = w_sc_conv.shape[-1]
    w_mlstm = mh_gain.shape[-1]
    dh = w_mlstm // nh
    d_ff = w_down.shape[1]
    n_gates = 2 * nh
    gate_col0 = 3 * w_conv + 4 * w_mlstm
    assert w_in.shape[-1] == gate_col0 + n_gates and w_conv + w_mlstm == d
    assert seq % TM_CONV == 0 and seq % TM_FFN == 0 and seq % MLSTM_CHUNK == 0
    assert w_conv % TC_CONV == 0 and d_ff % TF_FFN == 0
    assert (3 * w_conv) % TN_QKVO == 0 and (4 * w_mlstm) % TN_QKVO == 0
    assert (batch * seq) % TM_QKVO == 0
    alpha = (2 * depth) ** 0.25

    x2d = x.reshape(batch * seq, d)
    for l in range(depth):
        w_in_t = w_in[l].T
        w_gates_t_bf = jnp.pad(w_in_t[gate_col0:], ((0, LANES - n_gates), (0, 0))).astype(BF16)
        b_gates_row = jnp.pad(b_gates[l], (0, LANES - n_gates)).reshape(1, LANES)

        y_conv, xb2d, w_down_bf, w_out_bf = _inproj_conv(
            x2d, w_in_t, w_sc_conv[l], w_down[l], w_out[l], seq=seq, w_conv=w_conv)
        qkvo, w_up_bf = _inproj_qkvo(xb2d, w_in_t, w_up[l], col0=3 * w_conv, width=4 * w_mlstm,
                                     tf=TF_FFN)
        x2d = _mlstm_outproj(qkvo, xb2d, w_gates_t_bf, b_gates_row, mh_gain[l].reshape(1, w_mlstm),
                             x2d, y_conv, w_out_bf, ln1_g[l].reshape(1, d), ln1_b[l].reshape(1, d),
                             seq=seq, nh=nh, dh=dh, alpha=alpha)
        x2d = _convffn_ln(x2d, w_up_bf, w_ffn_conv[l],
                          b_ffn_conv[l].reshape(1, 2 * d_ff), w_down_bf,
                          ln2_g[l].reshape(1, d), ln2_b[l].reshape(1, d),
                          alpha=alpha, seq=seq, d_ff=d_ff)
    return x2d.reshape(batch, seq, d)
```
